```python
import math
import jax, jax.numpy as jnp
from jax import lax
import numpy as np

D_MODEL = 1024
BATCH = 2
SEQ = 8192
DEPTH = 2
DEC_BATCH = 128
DEC_SEQ = 1
PAST_LEN = 16384
PAGE_SIZE = 128

HEAD_DIM = 64
H_A = 8
KV_A = 2
G_A = H_A // KV_A
D_A = H_A * HEAD_DIM
WINDOW = 128
ROT_DIM = HEAD_DIM // 4
ROPE_THETA = 500000.0
G_B = 4
D_B = G_B * HEAD_DIM
CHUNK_B = 128
H_C = 4
DK_C = 64
DV_C = 64
D_C = H_C * DV_C
CHUNK_C = 32
D_MIX = D_A + D_B + D_C
IN_SIZES = (D_A, KV_A * HEAD_DIM, KV_A * HEAD_DIM, D_B, D_B, H_C * DK_C, H_C * DK_C, D_C, D_C)
D_IN = 2304
D_FF = 2816
ALPHA = (2 * DEPTH) ** 0.25
BETA = (8 * DEPTH) ** -0.25
LN_EPS = 1e-5
RMS_EPS = 1e-6
NEG_BIG = -1e30
N_MOD = 9

kernel_name = 'hymba_swa_gmlp_hgrn2_macaron_deepnorm_step'


def layer_norm(x, g, b):
    xf = x.astype(jnp.float32)
    mu = jnp.mean(xf, axis=-1, keepdims=True)
    var = jnp.mean(jnp.square(xf - mu), axis=-1, keepdims=True)
    return ((xf - mu) * lax.rsqrt(var + LN_EPS) * g.astype(jnp.float32) + b.astype(jnp.float32)).astype(x.dtype)


def rms_norm(x, g):
    xf = x.astype(jnp.float32)
    return xf * lax.rsqrt(jnp.mean(jnp.square(xf), axis=-1, keepdims=True) + RMS_EPS) * g.astype(jnp.float32)


def modulate(x, shift, scale):
    return x * (1 + scale) + shift


def post_norm(x, y, gate, g, b):
    return layer_norm(ALPHA * x + (1 + gate) * y, g, b)


def swiglu(h, w_in, w_out):
    a, g = jnp.split(h @ w_in, 2, axis=-1)
    return (jax.nn.silu(g) * a) @ w_out


def rope(x, pos):
    half = ROT_DIM // 2
    inv = ROPE_THETA ** (-jnp.arange(half, dtype=jnp.float32) * 2.0 / ROT_DIM)
    ang = pos.astype(jnp.float32)[:, None] * inv[None, :]
    cos = jnp.cos(ang)[None, :, None, :]
    sin = jnp.sin(ang)[None, :, None, :]
    xr = x[..., :ROT_DIM].astype(jnp.float32)
    x1, x2 = xr[..., :half], xr[..., half:]
    rot = jnp.concatenate([x1 * cos - x2 * sin, x2 * cos + x1 * sin], axis=-1)
    return jnp.concatenate([rot.astype(x.dtype), x[..., ROT_DIM:]], axis=-1)


def sink_probs(s, sink):
    sk = jnp.broadcast_to(sink.astype(jnp.float32).reshape(KV_A, G_A, 1, 1), s.shape[:-1] + (1,))
    return jax.nn.softmax(jnp.concatenate([s, sk], axis=-1), axis=-1)[..., :-1]


def swa_prompt(q, k, v, sink):
    B, T = q.shape[:2]
    nb = T // WINDOW
    qb = q.reshape(B, nb, WINDOW, KV_A, G_A, HEAD_DIM)

    def with_prev(t):
        t = t.reshape(B, nb, WINDOW, KV_A, HEAD_DIM)
        prev = jnp.pad(t, ((0, 0), (1, 0), (0, 0), (0, 0), (0, 0)))[:, :-1]
        return jnp.concatenate([prev, t], axis=2)

    kk, vv = with_prev(k), with_prev(v)
    s = jnp.einsum('bnqkgd,bnskd->bnkgqs', qb, kk, preferred_element_type=jnp.float32) * (HEAD_DIM ** -0.5)
    qi = jnp.arange(WINDOW)[:, None] + WINDOW
    kj = jnp.arange(2 * WINDOW)[None, :]
    band = (kj <= qi) & (qi - kj < WINDOW)
    mask = band[None] & ((jnp.arange(nb)[:, None, None] > 0) | (kj >= WINDOW)[None])
    s = jnp.where(mask[None, :, None, None], s, NEG_BIG)
    p = sink_probs(s, sink)
    o = jnp.einsum('bnkgqs,bnskd->bnqkgd', p.astype(vv.dtype), vv)
    return o.reshape(B, T, D_A)


def swa_sample(q, k, v, sink, k_cache, v_cache):
    B, T = q.shape[:2]
    W = k_cache.shape[1]
    kk = jnp.concatenate([k_cache.astype(k.dtype), k], axis=1)
    vv = jnp.concatenate([v_cache.astype(v.dtype), v], axis=1)
    qpos = PAST_LEN + jnp.arange(T)
    kpos = PAST_LEN - W + jnp.arange(W + T)
    mask = (kpos[None, :] <= qpos[:, None]) & (qpos[:, None] - kpos[None, :] < WINDOW)
    qh = q.reshape(B, T, KV_A, G_A, HEAD_DIM)
    s = jnp.einsum('btkgd,bskd->bkgts', qh, kk, preferred_element_type=jnp.float32) * (HEAD_DIM ** -0.5)
    s = jnp.where(mask, s, NEG_BIG)
    p = sink_probs(s, sink)
    o = jnp.einsum('bkgts,bskd->btkgd', p.astype(vv.dtype), vv)
    return o.reshape(B, T, D_A)


def spatial_gate(u, v, ws, bs):
    B, T = u.shape[:2]
    Tp = -(-T // CHUNK_B) * CHUNK_B
    vp = jnp.pad(v, ((0, 0), (0, Tp - T), (0, 0))).reshape(B, Tp // CHUNK_B, CHUNK_B, G_B, HEAD_DIM)
    causal = jnp.tril(jnp.ones((CHUNK_B, CHUNK_B), dtype=bool))
    wm = jnp.where(causal[None], ws, 0)
    mixed = jnp.einsum('gts,bnsgc->bntgc', wm, vp) + bs.T[None, None, :, :, None]
    mixed = mixed.reshape(B, Tp, D_B)[:, :T]
    return u * mixed


def hgrn2(q, lf, k, i, S0):
    B, T = q.shape[:2]
    L = min(CHUNK_C, T)
    Tp = -(-T // L) * L

    def chunks(t):
        t = jnp.pad(t, ((0, 0), (0, Tp - T), (0, 0), (0, 0)))
        return t.reshape(B, Tp // L, L, H_C, t.shape[-1]).transpose(1, 0, 3, 2, 4)

    causal = jnp.tril(jnp.ones((L, L), dtype=bool))[:, :, None]

    def step(S, xs):
        qc, lfc, kc, ic = xs
        b = jnp.cumsum(lfc, axis=2)
        inter = jnp.einsum('bhtk,bhkv->bhtv', qc * jnp.exp(b), S)
        diff = b[:, :, :, None, :] - b[:, :, None, :, :]
        decay = jnp.where(causal, jnp.exp(jnp.where(causal, diff, 0.0)), 0.0)
        att = jnp.einsum('bhtk,bhsk,bhtsk->bhts', qc, kc, decay)
        o = inter + jnp.einsum('bhts,bhsv->bhtv', att, ic)
        bl = b[:, :, -1]
        S = jnp.exp(bl)[..., None] * S + jnp.einsum('bhsk,bhsv->bhkv', kc * jnp.exp(bl[:, :, None] - b), ic)
        return S, o

    S, o = lax.scan(step, S0, (chunks(q), chunks(lf), chunks(k), chunks(i)))
    o = o.transpose(1, 0, 3, 2, 4).reshape(B, Tp, H_C, DV_C)[:, :T]
    return o, S


def trunk(x, c, positions, k_cache, v_cache, s_cache, w_in, w_out, attn_sinks, gmlp_ln_g, gmlp_ln_b,
          gmlp_ws, gmlp_bs, hgrn_lb, hgrn_norm_g, ffn1_in, ffn1_out, ffn2_in, ffn2_out, ada_w, ada_b,
          ln_g, ln_b):
    sample = k_cache is not None
    B, T = x.shape[:2]
    idx = [int(n) for n in np.cumsum(IN_SIZES)[:-1]]
    P = jax.nn.softmax(hgrn_lb.astype(jnp.float32), axis=0)
    lb_all = jnp.cumsum(P, axis=0) - P[0]
    ks_, vs_, gs_, ss_ = [], [], [], []
    for l in range(DEPTH):
        m = jnp.split((jax.nn.silu(c) @ ada_w[l] + ada_b[l])[:, None, :], N_MOD, axis=-1)
        x = post_norm(x, 0.5 * swiglu(modulate(x, m[0], m[1]), ffn1_in[l], ffn1_out[l]), m[2], ln_g[l, 0], ln_b[l, 0])
        h = modulate(x, m[3], m[4])
        q, k, v, u, gv, cq, cf, ci, cg = jnp.split(h @ w_in[l], idx, axis=-1)
        q = rope(q.reshape(B, T, H_A, HEAD_DIM), positions)
        k = rope(k.reshape(B, T, KV_A, HEAD_DIM), positions)
        v = v.reshape(B, T, KV_A, HEAD_DIM)
        if sample:
            o_a = swa_sample(q, k, v, attn_sinks[l], k_cache[l], v_cache[l])
            ks_.append(k)
            vs_.append(v)
            S0 = s_cache[l].astype(jnp.float32)
        else:
            o_a = swa_prompt(q, k, v, attn_sinks[l])
            wb = min(WINDOW, T)
            ks_.append(k[:, T - wb:])
            vs_.append(v[:, T - wb:])
            S0 = jnp.zeros((B, H_C, DK_C, DV_C), jnp.float32)
        vn = layer_norm(gv, gmlp_ln_g[l], gmlp_ln_b[l])
        o_b = spatial_gate(u, vn, gmlp_ws[l], gmlp_bs[l])
        if sample:
            gs_.append(vn)
        lb = lb_all[l]
        f = lb + (1.0 - lb) * jax.nn.sigmoid(cf.astype(jnp.float32))
        log_f = jnp.log(f).reshape(B, T, H_C, DK_C)
        key_c = (1.0 - f).reshape(B, T, H_C, DK_C)
        o_c, S = hgrn2(cq.reshape(B, T, H_C, DK_C).astype(jnp.float32), log_f, key_c,
                       ci.reshape(B, T, H_C, DV_C).astype(jnp.float32), S0)
        ss_.append(S)
        o_c = rms_norm(o_c, hgrn_norm_g[l]) * jax.nn.silu(cg.reshape(B, T, H_C, DV_C).astype(jnp.float32))
        o_c = o_c.reshape(B, T, D_C).astype(x.dtype)
        mix = jnp.concatenate([o_a, o_b, o_c], axis=-1) @ w_out[l]
        x = post_norm(x, mix, m[5], ln_g[l, 1], ln_b[l, 1])
        x = post_norm(x, 0.5 * swiglu(modulate(x, m[6], m[7]), ffn2_in[l], ffn2_out[l]), m[8], ln_g[l, 2], ln_b[l, 2])
    g_state = jnp.stack(gs_, axis=0) if sample else None
    return x, jnp.stack(ks_, axis=0), jnp.stack(vs_, axis=0), g_state, jnp.stack(ss_, axis=0)


def setup_inputs(seed: int = 0) -> dict:
    key = jax.random.key(seed)
    ks = jax.random.split(key, 26)
    f32 = jnp.float32

    def nrm(k, shape, scale):
        return jax.random.normal(k, shape, f32) * scale

    W_BUF = min(WINDOW, PAST_LEN)
    return {
        'x_prompt': nrm(ks[0], (BATCH, SEQ, D_MODEL), 1.0),
        'x_sample': nrm(ks[1], (DEC_BATCH, DEC_SEQ, D_MODEL), 1.0),
        'cache_k': nrm(ks[2], (DEPTH, DEC_BATCH, W_BUF, KV_A, HEAD_DIM), 1.0),
        'cache_v': nrm(ks[3], (DEPTH, DEC_BATCH, W_BUF, KV_A, HEAD_DIM), 1.0),
        'state_hgrn': nrm(ks[4], (DEPTH, DEC_BATCH, H_C, DK_C, DV_C), 0.5),
        'c_prompt': nrm(ks[5], (BATCH, D_MODEL), 1.0),
        'c_sample': nrm(ks[6], (DEC_BATCH, D_MODEL), 1.0),
        'w_in': nrm(ks[7], (DEPTH, D_MODEL, D_IN), D_MODEL ** -0.5),
        'w_out': nrm(ks[8], (DEPTH, D_MIX, D_MODEL), BETA * D_MIX ** -0.5),
        'attn_sinks': nrm(ks[9], (DEPTH, H_A), 0.5),
        'gmlp_ln_g': 1.0 + nrm(ks[10], (DEPTH, D_B), 0.02),
        'gmlp_ln_b': nrm(ks[11], (DEPTH, D_B), 0.02),
        'gmlp_ws': nrm(ks[12], (DEPTH, G_B, CHUNK_B, CHUNK_B), CHUNK_B ** -0.5),
        'gmlp_bs': 1.0 + nrm(ks[13], (DEPTH, G_B, CHUNK_B), 0.02),
        'hgrn_lb': nrm(ks[14], (DEPTH, H_C * DK_C), 1.0),
        'hgrn_norm_g': 1.0 + nrm(ks[15], (DEPTH, DV_C), 0.02),
        'ffn1_in': nrm(ks[16], (DEPTH, D_MODEL, 2 * D_FF), D_MODEL ** -0.5),
        'ffn1_out': nrm(ks[17], (DEPTH, D_FF, D_MODEL), BETA * D_FF ** -0.5),
        'ffn2_in': nrm(ks[18], (DEPTH, D_MODEL, 2 * D_FF), D_MODEL ** -0.5),
        'ffn2_out': nrm(ks[19], (DEPTH, D_FF, D_MODEL), BETA * D_FF ** -0.5),
        'ada_w': nrm(ks[20], (DEPTH, D_MODEL, N_MOD * D_MODEL), 0.2 * D_MODEL ** -0.5),
        'ada_b': nrm(ks[21], (DEPTH, N_MOD * D_MODEL), 0.02),
        'ln_g': 1.0 + nrm(ks[22], (DEPTH, 3, D_MODEL), 0.02),
        'ln_b': nrm(ks[23], (DEPTH, 3, D_MODEL), 0.02),
    }


def reference(x_prompt, x_sample, cache_k, cache_v, state_hgrn, c_prompt, c_sample, w_in, w_out,
              attn_sinks, gmlp_ln_g, gmlp_ln_b, gmlp_ws, gmlp_bs, hgrn_lb, hgrn_norm_g, ffn1_in,
              ffn1_out, ffn2_in, ffn2_out, ada_w, ada_b, ln_g, ln_b):
    pos_p = jnp.arange(x_prompt.shape[1])
    pos_s = PAST_LEN + jnp.arange(x_sample.shape[1])
    y_prompt, k_win_prompt, v_win_prompt, _, hgrn_state_prompt = trunk(
        x_prompt, c_prompt, pos_p, None, None, None, w_in, w_out, attn_sinks, gmlp_ln_g, gmlp_ln_b,
        gmlp_ws, gmlp_bs, hgrn_lb, hgrn_norm_g, ffn1_in, ffn1_out, ffn2_in, ffn2_out, ada_w, ada_b,
        ln_g, ln_b)
    y_sample, k_new_sample, v_new_sample, gmlp_v_sample, hgrn_state_sample = trunk(
        x_sample, c_sample, pos_s, cache_k, cache_v, state_hgrn, w_in, w_out, attn_sinks, gmlp_ln_g,
        gmlp_ln_b, gmlp_ws, gmlp_bs, hgrn_lb, hgrn_norm_g, ffn1_in, ffn1_out, ffn2_in, ffn2_out,
        ada_w, ada_b, ln_g, ln_b)
    return (y_prompt, y_sample, k_win_prompt, v_win_prompt, hgrn_state_prompt,
            k_new_sample, v_new_sample, gmlp_v_sample, hgrn_state_sample)
```

```python
import functools

import numpy as np
import jax
import jax.numpy as jnp
from jax import lax
from jax.experimental import pallas as pl
from jax.experimental.pallas import tpu as pltpu

F32 = jnp.float32
BF16 = jnp.bfloat16

D_MODEL = 1024
HEAD_DIM = 64
H_A = 8
KV_A = 2
G_A = H_A // KV_A
D_A = H_A * HEAD_DIM
D_KV = KV_A * HEAD_DIM
WINDOW = 128
ROT_DIM = HEAD_DIM // 4
ROPE_THETA = 500000.0
G_B = 4
D_B = G_B * HEAD_DIM
CHUNK_B = 128
H_C = 4
DK_C = 64
DV_C = 64
D_C = H_C * DV_C
D_IN = 2304
D_FF = 2816
N_MOD = 9
LN_EPS = 1e-5
RMS_EPS = 1e-6
NEG_BIG = -1e30
PAST_LEN = 16384

OFF_Q = 0
OFF_K = OFF_Q + D_A
OFF_V = OFF_K + D_KV
OFF_U = OFF_V + D_KV
OFF_GV = OFF_U + D_B
OFF_CQ = OFF_GV + D_B
OFF_CF = OFF_CQ + D_C
OFF_CI = OFF_CF + D_C
OFF_CG = OFF_CI + D_C

SUBLANES = 8
LANES = 128
MXU_DIM = 256
VMEM_LIMIT_BYTES = 56 * 1024 * 1024

FFN_ROWS = 1024
FF_CHUNK = MXU_DIM
ADA_COLS = 1024
MIX_ROWS = 256
HG_CHUNK = 16
PAD_ROWS = 16
MOD_ROWS = SUBLANES
SAMPLE_SEQS = 32


def _dot(a, b):
    return jnp.dot(a, b, preferred_element_type=F32)


def _dot_t(a, b):
    return lax.dot_general(a, b, (((1,), (1,)), ((), ())), preferred_element_type=F32)


def _tdot(a, b):
    return lax.dot_general(a, b, (((0,), (0,)), ((), ())), preferred_element_type=F32)


def _layer_norm(z, g, b):
    mu = jnp.mean(z, axis=-1, keepdims=True)
    zc = z - mu
    var = jnp.mean(zc * zc, axis=-1, keepdims=True)
    return zc * lax.rsqrt(var + LN_EPS) * g + b


def _silu(x):
    return x * jax.nn.sigmoid(x)


def _head_block_ones():
    r = lax.broadcasted_iota(jnp.int32, (D_C, D_C), 0) >> 6
    c = lax.broadcasted_iota(jnp.int32, (D_C, D_C), 1) >> 6
    return jnp.where(r == c, 1.0, 0.0).astype(BF16)


def _head_sum(x, ones_bd):
    hi = x.astype(BF16)
    lo = (x - hi.astype(F32)).astype(BF16)
    return _dot(hi, ones_bd) + _dot(lo, ones_bd)


def _rope(x, cos, sin_lo, sin_hi):
    n = x.shape[-1]
    reps = n // LANES
    if reps > 1:
        cos = jnp.concatenate([cos] * reps, axis=-1)
        sin_lo = jnp.concatenate([sin_lo] * reps, axis=-1)
        sin_hi = jnp.concatenate([sin_hi] * reps, axis=-1)
    half = ROT_DIM // 2
    return x * cos + pltpu.roll(x, n - half, 1) * sin_lo + pltpu.roll(x, half, 1) * sin_hi


def _hgrn_lower_bound(lb_ref, layer):
    w = lb_ref[...]
    e = jnp.exp(w - jnp.max(w, axis=0, keepdims=True))
    p = e / jnp.sum(e, axis=0, keepdims=True)
    lb = jnp.zeros((1, D_C), F32)
    for j in range(1, layer + 1):
        lb = lb + p[j:j + 1, :]
    return lb


def _adaln_kernel(c_ref, w_ref, b_ref, op_ref, os_ref):
    s = _silu(c_ref[...]).astype(BF16)
    r = _dot(s, w_ref[...].astype(BF16)) + b_ref[...]
    op_ref[...] = r[:MOD_ROWS]
    os_ref[...] = r[MOD_ROWS:]


def _adaln(c_all, ada_w, ada_b):
    depth, d, n = ada_w.shape
    rows = c_all.shape[0]
    n_s = rows - MOD_ROWS
    return pl.pallas_call(
        _adaln_kernel,
        grid=(depth, n // ADA_COLS),
        in_specs=[
            pl.BlockSpec((rows, d), lambda l, j: (0, 0)),
            pl.BlockSpec((None, d, ADA_COLS), lambda l, j: (l, 0, j)),
            pl.BlockSpec((None, 1, ADA_COLS), lambda l, j: (l, 0, j)),
        ],
        out_specs=[
            pl.BlockSpec((None, MOD_ROWS, ADA_COLS), lambda l, j: (l, 0, j)),
            pl.BlockSpec((None, n_s, ADA_COLS), lambda l, j: (l, 0, j)),
        ],
        out_shape=[
            jax.ShapeDtypeStruct((depth, MOD_ROWS, n), F32),
            jax.ShapeDtypeStruct((depth, n_s, n), F32),
        ],
        compiler_params=pltpu.CompilerParams(
            dimension_semantics=("arbitrary", "arbitrary"), vmem_limit_bytes=VMEM_LIMIT_BYTES),
        name="adaln",
    )(c_all, ada_w, ada_b.reshape(depth, 1, n))


def _mod_rows(ref, per_row, batch):
    return ref[...] if per_row else ref[pl.ds(batch, 1), :]


def _ffn_kernel(x_ref, sh_ref, sc_ref, gt_ref, win_ref, wout_ref, lng_ref, lnb_ref, o_ref, act_ref,
                *, per_row, tiles_per_batch, alpha):
    batch = pl.program_id(0) // tiles_per_batch
    x = x_ref[...]
    shift = _mod_rows(sh_ref, per_row, batch)
    scale = _mod_rows(sc_ref, per_row, batch)
    gate = _mod_rows(gt_ref, per_row, batch)
    h = (x * (1.0 + scale) + shift).astype(BF16)
    for c in range(D_FF // FF_CHUNK):
        lo = c * FF_CHUNK
        a = _dot(h, win_ref[:, lo:lo + FF_CHUNK])
        g = _dot(h, win_ref[:, D_FF + lo:D_FF + lo + FF_CHUNK])
        act_ref[:, lo:lo + FF_CHUNK] = (_silu(g) * a).astype(BF16)
    y = _dot(act_ref[...], wout_ref[...])
    z = alpha * x + (1.0 + gate) * (0.5 * y)
    o_ref[...] = _layer_norm(z, lng_ref[...], lnb_ref[...])


def _ffn(x, mods, layer, sub, w_in, w_out, ln_g, ln_b, *, per_row, rows_per_batch, alpha):
    m, d = x.shape
    tm = min(FFN_ROWS, rows_per_batch)
    assert m % tm == 0 and rows_per_batch % tm == 0
    mod_rows = mods.shape[1]
    if per_row:
        mod_spec = lambda j: pl.BlockSpec((None, tm, d), lambda i: (layer, i, j))
    else:
        mod_spec = lambda j: pl.BlockSpec((None, mod_rows, d), lambda i: (layer, 0, j))
    const = dict(pipeline_mode=pl.Buffered(1))
    return pl.pallas_call(
        functools.partial(_ffn_kernel, per_row=per_row, tiles_per_batch=rows_per_batch // tm, alpha=alpha),
        grid=(m // tm,),
        in_specs=[
            pl.BlockSpec((tm, d), lambda i: (i, 0)),
            mod_spec(3 * sub), mod_spec(3 * sub + 1), mod_spec(3 * sub + 2),
            pl.BlockSpec((None, d, 2 * D_FF), lambda i: (layer, 0, 0), **const),
            pl.BlockSpec((None, D_FF, d), lambda i: (layer, 0, 0), **const),
            pl.BlockSpec((None, 1, d), lambda i: (layer * 3 + sub, 0, 0)),
            pl.BlockSpec((None, 1, d), lambda i: (layer * 3 + sub, 0, 0)),
        ],
        out_specs=pl.BlockSpec((tm, d), lambda i: (i, 0)),
        out_shape=jax.ShapeDtypeStruct((m, d), F32),
        scratch_shapes=[pltpu.VMEM((tm, D_FF), BF16)],
        compiler_params=pltpu.CompilerParams(
            dimension_semantics=("arbitrary",), vmem_limit_bytes=VMEM_LIMIT_BYTES),
        name="ffn_rows" if per_row else "ffn_batch",
    )(x, mods, mods, mods, w_in, w_out, ln_g, ln_b)


def _attention_block(q_blk, keys, vals, mask, sink_ref, layer):
    outs = []
    rowgrp = lax.broadcasted_iota(jnp.int32, (G_A * WINDOW, 1), 0) >> 7
    for j in range(KV_A):
        kj = keys[:, j * HEAD_DIM:(j + 1) * HEAD_DIM].astype(BF16)
        vj = vals[:, j * HEAD_DIM:(j + 1) * HEAD_DIM].astype(BF16)
        qj = jnp.concatenate(
            [q_blk[:, (j * G_A + g) * HEAD_DIM:(j * G_A + g + 1) * HEAD_DIM] for g in range(G_A)],
            axis=0).astype(BF16)
        s = jnp.where(mask, _dot_t(qj, kj), NEG_BIG)
        sink = jnp.zeros((G_A * WINDOW, 1), F32)
        for g in range(G_A):
            sink = jnp.where(rowgrp == g, sink_ref[layer, j * G_A + g], sink)
        m = jnp.maximum(jnp.max(s, axis=-1, keepdims=True), sink)
        p = jnp.exp(s - m)
        den = jnp.sum(p, axis=-1, keepdims=True) + jnp.exp(sink - m)
        o = _dot((p * (1.0 / den)).astype(BF16), vj)
        outs.extend(o[g * WINDOW:(g + 1) * WINDOW] for g in range(G_A))
    return jnp.concatenate(outs, axis=-1)


def _mixer_prompt_kernel(
        x_ref, sh_ref, sc_ref, gt_ref, win_ref, wout_ref, cos_ref, sl_ref, sh2_ref, sink_ref,
        gg_ref, gb_ref, ws_ref, bs_ref, lb_ref, ng_ref, lng_ref, lnb_ref,
        y_ref, kwin_ref, vwin_ref, st_out_ref,
        kprev_ref, vprev_ref, st_ref, scan_ref, bpad_ref, kpad_ref, ipad_ref, e_ref,
        *, layer, tq, alpha):
    b_idx = pl.program_id(0)
    t_idx = pl.program_id(1)
    n_t = pl.num_programs(1)

    @pl.when(t_idx == 0)
    def _():
        kprev_ref[...] = jnp.zeros_like(kprev_ref)
        vprev_ref[...] = jnp.zeros_like(vprev_ref)
        st_ref[...] = jnp.zeros_like(st_ref)

    zpad = jnp.zeros((PAD_ROWS, D_C), F32)
    scan_ref[0:PAD_ROWS, :] = zpad
    scan_ref[PAD_ROWS + tq:, :] = zpad
    bpad_ref[0:PAD_ROWS, :] = zpad
    kpad_ref[0:PAD_ROWS, :] = zpad
    ipad_ref[0:PAD_ROWS, :] = zpad

    x = x_ref[...]
    shift = sh_ref[pl.ds(b_idx, 1), :]
    scale = sc_ref[pl.ds(b_idx, 1), :]
    gate = gt_ref[pl.ds(b_idx, 1), :]
    h = (x * (1.0 + scale) + shift).astype(BF16)
    proj = _dot(h, win_ref[...])

    cos, s_lo, s_hi = cos_ref[...], sl_ref[...], sh2_ref[...]
    q = _rope(proj[:, OFF_Q:OFF_K], cos, s_lo, s_hi) * (HEAD_DIM ** -0.5)
    k = _rope(proj[:, OFF_K:OFF_V], cos, s_lo, s_hi)
    v = proj[:, OFF_V:OFF_U]

    qi = lax.broadcasted_iota(jnp.int32, (G_A * WINDOW, 2 * WINDOW), 0) & (WINDOW - 1)
    kj = lax.broadcasted_iota(jnp.int32, (G_A * WINDOW, 2 * WINDOW), 1)
    dist = kj - qi
    band = (dist - 1).astype(jnp.uint32) < jnp.uint32(WINDOW)
    o_a = []
    for blk in range(tq // WINDOW):
        r0 = blk * WINDOW
        if blk == 0:
            k_prev, v_prev = kprev_ref[...], vprev_ref[...]
            lower = jnp.where(t_idx == 0, WINDOW, 0)
            mask = jnp.logical_and(band, kj >= lower)
        else:
            k_prev, v_prev = k[r0 - WINDOW:r0], v[r0 - WINDOW:r0]
            mask = band
        keys = jnp.concatenate([k_prev, k[r0:r0 + WINDOW]], axis=0)
        vals = jnp.concatenate([v_prev, v[r0:r0 + WINDOW]], axis=0)
        o_a.append(_attention_block(q[r0:r0 + WINDOW], keys, vals, mask, sink_ref, layer))
    o_a = jnp.concatenate(o_a, axis=0)
    kprev_ref[...] = k[tq - WINDOW:]
    vprev_ref[...] = v[tq - WINDOW:]
    kwin_ref[...] = k[tq - WINDOW:]
    vwin_ref[...] = v[tq - WINDOW:]

    vn = _layer_norm(proj[:, OFF_GV:OFF_CQ], gg_ref[...], gb_ref[...])
    u = proj[:, OFF_U:OFF_GV]
    tr = lax.broadcasted_iota(jnp.int32, (CHUNK_B, CHUNK_B), 0)
    tc = lax.broadcasted_iota(jnp.int32, (CHUNK_B, CHUNK_B), 1)
    wm = [jnp.where(tr >= tc, ws_ref[g], 0.0).astype(BF16) for g in range(G_B)]
    o_b = []
    for c in range(tq // CHUNK_B):
        vc = vn[c * CHUNK_B:(c + 1) * CHUNK_B].astype(BF16)
        mixed = jnp.concatenate(
            [_dot(wm[g], vc[:, g * HEAD_DIM:(g + 1) * HEAD_DIM]) for g in range(G_B)], axis=-1)
        o_b.append(u[c * CHUNK_B:(c + 1) * CHUNK_B] * (mixed + bs_ref[...]))
    o_b = jnp.concatenate(o_b, axis=0)

    cq = proj[:, OFF_CQ:OFF_CF]
    ci = proj[:, OFF_CI:OFF_CG]
    lb = _hgrn_lower_bound(lb_ref, layer)
    f = lb + (1.0 - lb) * jax.nn.sigmoid(proj[:, OFF_CF:OFF_CI])
    lf = jnp.log(f)
    kc = 1.0 - f
    pos = lax.broadcasted_iota(jnp.int32, (tq, D_C), 0) & (HG_CHUNK - 1)

    def shifted(ref, d):
        return ref[PAD_ROWS - d:PAD_ROWS - d + tq, :]

    bsum = lf
    step = 1
    while step < HG_CHUNK:
        scan_ref[PAD_ROWS:PAD_ROWS + tq, :] = bsum
        bsum = bsum + jnp.where(pos >= step, shifted(scan_ref, step), 0.0)
        step *= 2
    rs = lf
    step = 1
    while step < HG_CHUNK:
        scan_ref[PAD_ROWS:PAD_ROWS + tq, :] = rs
        rs = rs + jnp.where(pos < HG_CHUNK - step, shifted(scan_ref, -step), 0.0)
        step *= 2
    scan_ref[PAD_ROWS:PAD_ROWS + tq, :] = rs
    rsum = jnp.where(pos < HG_CHUNK - 1, shifted(scan_ref, -1), 0.0)

    qb = (cq * jnp.exp(bsum)).astype(BF16)
    kb = (kc * jnp.exp(rsum)).astype(BF16)
    ci_b = ci.astype(BF16)
    bpad_ref[PAD_ROWS:, :] = bsum
    kpad_ref[PAD_ROWS:, :] = kc
    ipad_ref[PAD_ROWS:, :] = ci

    ones_bd = _head_block_ones()
    e_ref[0:tq, :] = (cq * kc).astype(BF16)
    for d in range(1, HG_CHUNK):
        e = cq * shifted(kpad_ref, d) * jnp.exp(bsum - shifted(bpad_ref, d))
        e_ref[d * tq:(d + 1) * tq, :] = jnp.where(pos >= d, e, 0.0).astype(BF16)
    att = _dot(e_ref[...], ones_bd)
    o_c = att[0:tq] * ci
    for d in range(1, HG_CHUNK):
        o_c = o_c + att[d * tq:(d + 1) * tq] * shifted(ipad_ref, d)

    rblk = lax.broadcasted_iota(jnp.int32, (D_C, D_C), 0) >> 6
    cblk = lax.broadcasted_iota(jnp.int32, (D_C, D_C), 1) >> 6
    diag = rblk == cblk
    cblk_s = lax.broadcasted_iota(jnp.int32, (DV_C, D_C), 1) >> 6
    st = st_ref[...]
    o_inter = []
    for n in range(tq // HG_CHUNK):
        r0 = n * HG_CHUNK
        bdt = jnp.where(diag, jnp.concatenate([st] * H_C, axis=0), 0.0).astype(BF16)
        o_inter.append(_dot_t(qb[r0:r0 + HG_CHUNK], bdt))
        upd = _tdot(ci_b[r0:r0 + HG_CHUNK], kb[r0:r0 + HG_CHUNK])
        ut = jnp.zeros((DV_C, D_C), F32)
        for hh in range(H_C):
            ut = ut + jnp.where(cblk_s == hh, upd[hh * DV_C:(hh + 1) * DV_C], 0.0)
        decay = jnp.exp(bsum[r0 + HG_CHUNK - 1:r0 + HG_CHUNK, :])
        st = decay * st + ut
    st_ref[...] = st
    o_c = o_c + jnp.concatenate(o_inter, axis=0)

    @pl.when(t_idx == n_t - 1)
    def _():
        st_out_ref[...] = st.T

    ms = _head_sum(o_c * o_c, ones_bd) * (1.0 / DV_C)
    o_c = o_c * lax.rsqrt(ms + RMS_EPS) * ng_ref[...] * _silu(proj[:, OFF_CG:])

    mix = _dot(jnp.concatenate([o_a, o_b, o_c], axis=-1).astype(BF16), wout_ref[...])
    z = alpha * x + (1.0 + gate) * mix
    y_ref[...] = _layer_norm(z, lng_ref[...], lnb_ref[...])


def _mixer_prompt(x, mods, layer, w_in, w_out, tables, sinks, gln_g, gln_b, ws, bs_exp, lb, ng_exp,
                  ln_g, ln_b, *, batch, seq, alpha):
    m, d = x.shape
    tq = min(MIX_ROWS, seq)
    assert seq % tq == 0 and tq % WINDOW == 0
    n_t = seq // tq
    depth = w_in.shape[0]
    cos, s_lo, s_hi = tables
    mod_spec = lambda j: pl.BlockSpec((None, MOD_ROWS, d), lambda b, t: (layer, 0, j))
    tab_spec = pl.BlockSpec((tq, LANES), lambda b, t: (t, 0))
    lay3 = lambda shape: pl.BlockSpec((None,) + shape, lambda b, t: (layer,) + (0,) * len(shape))
    return pl.pallas_call(
        functools.partial(_mixer_prompt_kernel, layer=layer, tq=tq, alpha=alpha),
        grid=(batch, n_t),
        in_specs=[
            pl.BlockSpec((tq, d), lambda b, t: (b * n_t + t, 0)),
            mod_spec(3), mod_spec(4), mod_spec(5),
            lay3((d, D_IN)), lay3((d, d)),
            tab_spec, tab_spec, tab_spec,
            pl.BlockSpec(memory_space=pltpu.SMEM),
            lay3((1, D_B)), lay3((1, D_B)),
            lay3((G_B, CHUNK_B, CHUNK_B)), lay3((CHUNK_B, D_B)),
            pl.BlockSpec((depth, D_C), lambda b, t: (0, 0)),
            lay3((1, D_C)),
            pl.BlockSpec((None, 1, d), lambda b, t: (layer * 3 + 1, 0, 0)),
            pl.BlockSpec((None, 1, d), lambda b, t: (layer * 3 + 1, 0, 0)),
        ],
        out_specs=[
            pl.BlockSpec((tq, d), lambda b, t: (b * n_t + t, 0)),
            pl.BlockSpec((None, WINDOW, D_KV), lambda b, t: (b, 0, 0)),
            pl.BlockSpec((None, WINDOW, D_KV), lambda b, t: (b, 0, 0)),
            pl.BlockSpec((None, D_C, DV_C), lambda b, t: (b, 0, 0)),
        ],
        out_shape=[
            jax.ShapeDtypeStruct((m, d), F32),
            jax.ShapeDtypeStruct((batch, WINDOW, D_KV), F32),
            jax.ShapeDtypeStruct((batch, WINDOW, D_KV), F32),
            jax.ShapeDtypeStruct((batch, D_C, DV_C), F32),
        ],
        scratch_shapes=[
            pltpu.VMEM((WINDOW, D_KV), F32),
            pltpu.VMEM((WINDOW, D_KV), F32),
            pltpu.VMEM((DV_C, D_C), F32),
            pltpu.VMEM((PAD_ROWS + tq + PAD_ROWS, D_C), F32),
            pltpu.VMEM((PAD_ROWS + tq, D_C), F32),
            pltpu.VMEM((PAD_ROWS + tq, D_C), F32),
            pltpu.VMEM((PAD_ROWS + tq, D_C), F32),
            pltpu.VMEM((HG_CHUNK * tq, D_C), BF16),
        ],
        compiler_params=pltpu.CompilerParams(
            dimension_semantics=("arbitrary", "arbitrary"), vmem_limit_bytes=VMEM_LIMIT_BYTES),
        name="mixer_prompt",
    )(x, mods, mods, mods, w_in, w_out, cos, s_lo, s_hi, sinks, gln_g, gln_b, ws, bs_exp, lb, ng_exp,
      ln_g, ln_b)


def _mixer_sample_kernel(
        x_ref, sh_ref, sc_ref, gt_ref, win_ref, wout_ref, cos_ref, sl_ref, sh2_ref, sink_ref,
        gg_ref, gb_ref, w0_ref, bs_ref, lb_ref, ng_ref, lng_ref, lnb_ref, ck_ref, cv_ref, s0_ref,
        y_ref, knew_ref, vnew_ref, gv_ref, s1_ref,
        qt_ref, sc_scr, oc_ref,
        *, layer, nb, alpha):
    x = x_ref[...]
    h = (x * (1.0 + sc_ref[...]) + sh_ref[...]).astype(BF16)
    proj = _dot(h, win_ref[...])
    cos, s_lo, s_hi = cos_ref[...], sl_ref[...], sh2_ref[...]
    q = _rope(proj[:, OFF_Q:OFF_K], cos, s_lo, s_hi) * (HEAD_DIM ** -0.5)
    k = _rope(proj[:, OFF_K:OFF_V], cos, s_lo, s_hi)
    v = proj[:, OFF_V:OFF_U]
    knew_ref[...] = k
    vnew_ref[...] = v

    zeros64 = jnp.zeros((nb, HEAD_DIM), F32)
    for hd in range(H_A):
        qh = q[:, hd * HEAD_DIM:(hd + 1) * HEAD_DIM]
        row = jnp.concatenate([qh, zeros64] if hd < G_A else [zeros64, qh], axis=-1)
        qt_ref[pl.ds(hd, nb, stride=H_A), :] = row
    row0 = lax.broadcasted_iota(jnp.int32, (WINDOW, D_KV), 0) == 0
    for b in range(nb):
        kb = jnp.where(row0, k[b:b + 1, :], ck_ref[b]).astype(BF16)
        sc_scr[b * H_A:(b + 1) * H_A, :] = _dot_t(qt_ref[b * H_A:(b + 1) * H_A, :].astype(BF16), kb)
    s = sc_scr[...]
    hrow = lax.broadcasted_iota(jnp.int32, (nb * H_A, 1), 0) & (H_A - 1)
    sink = jnp.zeros((nb * H_A, 1), F32)
    for hd in range(H_A):
        sink = jnp.where(hrow == hd, sink_ref[layer, hd], sink)
    mx = jnp.maximum(jnp.max(s, axis=-1, keepdims=True), sink)
    p = jnp.exp(s - mx)
    den = jnp.sum(p, axis=-1, keepdims=True) + jnp.exp(sink - mx)
    sc_scr[...] = p * (1.0 / den)
    for b in range(nb):
        vb = jnp.where(row0, v[b:b + 1, :], cv_ref[b]).astype(BF16)
        qt_ref[b * H_A:(b + 1) * H_A, :] = _dot(sc_scr[b * H_A:(b + 1) * H_A, :].astype(BF16), vb)
    o_a = []
    for hd in range(H_A):
        rows = qt_ref[pl.ds(hd, nb, stride=H_A), :]
        o_a.append(rows[:, :HEAD_DIM] if hd < G_A else rows[:, HEAD_DIM:])
    o_a = jnp.concatenate(o_a, axis=-1)

    vn = _layer_norm(proj[:, OFF_GV:OFF_CQ], gg_ref[...], gb_ref[...])
    gv_ref[...] = vn
    o_b = proj[:, OFF_U:OFF_GV] * (vn * w0_ref[...] + bs_ref[0:1, :])

    lb = _hgrn_lower_bound(lb_ref, layer)
    f = lb + (1.0 - lb) * jax.nn.sigmoid(proj[:, OFF_CF:OFF_CI])
    cq = proj[:, OFF_CQ:OFF_CF]
    ci = proj[:, OFF_CI:OFF_CG]
    zrows = jnp.zeros((LANES - nb, D_C), F32)

    def columns(a):
        return jnp.concatenate([a, zrows], axis=0).T

    f_t, k_t, q_t = columns(f), columns(1.0 - f), columns(cq)
    for b in range(nb):
        i_rows = jnp.concatenate(
            [jnp.broadcast_to(ci[b:b + 1, hh * DV_C:(hh + 1) * DV_C], (DK_C, DV_C)) for hh in range(H_C)],
            axis=0)
        s1 = f_t[:, b:b + 1] * s0_ref[b] + k_t[:, b:b + 1] * i_rows
        s1_ref[b] = s1
        w = q_t[:, b:b + 1] * s1
        oc_ref[b:b + 1, :] = jnp.concatenate(
            [jnp.sum(w[hh * DK_C:(hh + 1) * DK_C], axis=0, keepdims=True) for hh in range(H_C)], axis=-1)
    o_c = oc_ref[...]
    ones_bd = _head_block_ones()
    ms = _head_sum(o_c * o_c, ones_bd) * (1.0 / DV_C)
    o_c = o_c * lax.rsqrt(ms + RMS_EPS) * ng_ref[...] * _silu(proj[:, OFF_CG:])

    mix = _dot(jnp.concatenate([o_a, o_b, o_c], axis=-1).astype(BF16), wout_ref[...])
    z = alpha * x + (1.0 + gt_ref[...]) * mix
    y_ref[...] = _layer_norm(z, lng_ref[...], lnb_ref[...])


def _mixer_sample(x, mods, layer, w_in, w_out, tables, sinks, gln_g, gln_b, w0_exp, bs_exp, lb, ng_exp,
                  ln_g, ln_b, cache_k, cache_v, state, *, alpha):
    m, d = x.shape
    nb = min(SAMPLE_SEQS, m)
    assert m % nb == 0 and nb % SUBLANES == 0 and nb <= LANES
    depth = w_in.shape[0]
    cos, s_lo, s_hi = tables
    mod_spec = lambda j: pl.BlockSpec((None, nb, d), lambda i: (layer, i, j))
    tab_spec = pl.BlockSpec((1, LANES), lambda i: (0, 0))
    lay3 = lambda shape: pl.BlockSpec((None,) + shape, lambda i: (layer,) + (0,) * len(shape))
    return pl.pallas_call(
        functools.partial(_mixer_sample_kernel, layer=layer, nb=nb, alpha=alpha),
        grid=(m // nb,),
        in_specs=[
            pl.BlockSpec((nb, d), lambda i: (i, 0)),
            mod_spec(3), mod_spec(4), mod_spec(5),
            lay3((d, D_IN)), lay3((d, d)),
            tab_spec, tab_spec, tab_spec,
            pl.BlockSpec(memory_space=pltpu.SMEM),
            lay3((1, D_B)), lay3((1, D_B)), lay3((1, D_B)), lay3((CHUNK_B, D_B)),
            pl.BlockSpec((depth, D_C), lambda i: (0, 0)),
            lay3((1, D_C)),
            pl.BlockSpec((None, 1, d), lambda i: (layer * 3 + 1, 0, 0)),
            pl.BlockSpec((None, 1, d), lambda i: (layer * 3 + 1, 0, 0)),
            pl.BlockSpec((None, nb, WINDOW, D_KV), lambda i: (layer, i, 0, 0)),
            pl.BlockSpec((None, nb, WINDOW, D_KV), lambda i: (layer, i, 0, 0)),
            pl.BlockSpec((None, nb, D_C, DV_C), lambda i: (layer, i, 0, 0)),
        ],
        out_specs=[
            pl.BlockSpec((nb, d), lambda i: (i, 0)),
            pl.BlockSpec((nb, D_KV), lambda i: (i, 0)),
            pl.BlockSpec((nb, D_KV), lambda i: (i, 0)),
            pl.BlockSpec((nb, D_B), lambda i: (i, 0)),
            pl.BlockSpec((nb, D_C, DV_C), lambda i: (i, 0, 0)),
        ],
        out_shape=[
            jax.ShapeDtypeStruct((m, d), F32),
            jax.ShapeDtypeStruct((m, D_KV), F32),
            jax.ShapeDtypeStruct((m, D_KV), F32),
            jax.ShapeDtypeStruct((m, D_B), F32),
            jax.ShapeDtypeStruct((m, D_C, DV_C), F32),
        ],
        scratch_shapes=[
            pltpu.VMEM((nb * H_A, D_KV), F32),
            pltpu.VMEM((nb * H_A, WINDOW), F32),
            pltpu.VMEM((nb, D_C), F32),
        ],
        compiler_params=pltpu.CompilerParams(
            dimension_semantics=("arbitrary",), vmem_limit_bytes=VMEM_LIMIT_BYTES),
        name="mixer_sample",
    )(x, mods, mods, mods, w_in, w_out, cos, s_lo, s_hi, sinks, gln_g, gln_b, w0_exp, bs_exp, lb, ng_exp,
      ln_g, ln_b, cache_k, cache_v, state)


def _rope_tables(positions):
    half = ROT_DIM // 2
    inv = ROPE_THETA ** (-jnp.arange(half, dtype=F32) * 2.0 / ROT_DIM)
    ang = positions.astype(F32)[:, None] * inv[None, :]
    cos, sin = jnp.cos(ang), jnp.sin(ang)
    n = positions.shape[0]
    rest = HEAD_DIM - ROT_DIM
    cos_h = jnp.concatenate([cos, cos, jnp.ones((n, rest), F32)], axis=-1)
    lo_h = jnp.concatenate([-sin, jnp.zeros((n, half + rest), F32)], axis=-1)
    hi_h = jnp.concatenate([jnp.zeros((n, half), F32), sin, jnp.zeros((n, rest), F32)], axis=-1)
    two = lambda a: jnp.concatenate([a, a], axis=-1)
    return two(cos_h), two(lo_h), two(hi_h)


def kernel(x_prompt, x_sample, cache_k, cache_v, state_hgrn, c_prompt, c_sample, w_in, w_out, attn_sinks,
           gmlp_ln_g, gmlp_ln_b, gmlp_ws, gmlp_bs, hgrn_lb, hgrn_norm_g, ffn1_in, ffn1_out, ffn2_in,
           ffn2_out, ada_w, ada_b, ln_g, ln_b):
    batch, seq, d = x_prompt.shape
    n_s, dec_seq, _ = x_sample.shape
    depth = w_in.shape[0]
    assert d == D_MODEL and dec_seq == 1 and batch <= MOD_ROWS
    assert cache_k.shape[2] == WINDOW
    alpha = (2 * depth) ** 0.25

    w_in_b, w_out_b = w_in.astype(BF16), w_out.astype(BF16)
    f1i, f1o, f2i, f2o = (a.astype(BF16) for a in (ffn1_in, ffn1_out, ffn2_in, ffn2_out))
    ln_g3 = ln_g.reshape(depth * 3, 1, d)
    ln_b3 = ln_b.reshape(depth * 3, 1, d)
    gln_g = gmlp_ln_g.reshape(depth, 1, D_B)
    gln_b = gmlp_ln_b.reshape(depth, 1, D_B)
    bs_exp = jnp.repeat(jnp.swapaxes(gmlp_bs, 1, 2), HEAD_DIM, axis=2)
    w0_exp = jnp.repeat(gmlp_ws[:, :, 0, 0], HEAD_DIM, axis=1).reshape(depth, 1, D_B)
    ng_exp = jnp.tile(hgrn_norm_g, (1, H_C)).reshape(depth, 1, D_C)
    tab_p = _rope_tables(jnp.arange(seq))
    tab_s = _rope_tables(PAST_LEN + jnp.arange(dec_seq))

    c_all = jnp.concatenate([c_prompt, jnp.zeros((MOD_ROWS - batch, d), F32), c_sample], axis=0)
    mod_p, mod_s = _adaln(c_all, ada_w, ada_b)

    ck = cache_k.reshape(depth, n_s, WINDOW, D_KV)
    cv = cache_v.reshape(depth, n_s, WINDOW, D_KV)
    s0 = state_hgrn.reshape(depth, n_s, D_C, DV_C)

    xp = x_prompt.reshape(batch * seq, d)
    xs = x_sample.reshape(n_s, d)
    ffn_p = functools.partial(_ffn, per_row=False, rows_per_batch=seq, alpha=alpha)
    ffn_s = functools.partial(_ffn, per_row=True, rows_per_batch=n_s, alpha=alpha)
    kw, vw, st_p, kn, vnw, gvs, st_s = [], [], [], [], [], [], []
    for l in range(depth):
        xp = ffn_p(xp, mod_p, l, 0, f1i, f1o, ln_g3, ln_b3)
        xp, k_l, v_l, s_l = _mixer_prompt(
            xp, mod_p, l, w_in_b, w_out_b, tab_p, attn_sinks, gln_g, gln_b, gmlp_ws, bs_exp, hgrn_lb,
            ng_exp, ln_g3, ln_b3, batch=batch, seq=seq, alpha=alpha)
        xp = ffn_p(xp, mod_p, l, 2, f2i, f2o, ln_g3, ln_b3)
        kw.append(k_l), vw.append(v_l), st_p.append(s_l)

        xs = ffn_s(xs, mod_s, l, 0, f1i, f1o, ln_g3, ln_b3)
        xs, k_l, v_l, g_l, s_l = _mixer_sample(
            xs, mod_s, l, w_in_b, w_out_b, tab_s, attn_sinks, gln_g, gln_b, w0_exp, bs_exp, hgrn_lb,
            ng_exp, ln_g3, ln_b3, ck, cv, s0, alpha=alpha)
        xs = ffn_s(xs, mod_s, l, 2, f2i, f2o, ln_g3, ln_b3)
        kn.append(k_l), vnw.append(v_l), gvs.append(g_l), st_s.append(s_l)

    stack = lambda parts, shape: jnp.stack(parts, axis=0).reshape((depth,) + shape)
    return (
        xp.reshape(batch, seq, d),
        xs.reshape(n_s, dec_seq, d),
        stack(kw, (batch, WINDOW, KV_A, HEAD_DIM)),
        stack(vw, (batch, WINDOW, KV_A, HEAD_DIM)),
        stack(st_p, (batch, H_C, DK_C, DV_C)),
        stack(kn, (n_s, dec_seq, KV_A, HEAD_DIM)),
        stack(vnw, (n_s, dec_seq, KV_A, HEAD_DIM)),
        stack(gvs, (n_s, dec_seq, D_B)),
        stack(st_s, (n_s, H_C, DK_C, DV_C)),
    )
```

```python
import functools

import numpy as np
import jax
import jax.numpy as jnp
from jax import lax
from jax.experimental import pallas as pl
from jax.experimental.pallas import tpu as pltpu

F32 = jnp.float32
BF16 = jnp.bfloat16

D_MODEL = 1024
HEAD_DIM = 64
H_A = 8
KV_A = 2
G_A = H_A // KV_A
D_A = H_A * HEAD_DIM
D_KV = KV_A * HEAD_DIM
WINDOW = 128
ROT_DIM = HEAD_DIM // 4
ROPE_THETA = 500000.0
G_B = 4
D_B = G_B * HEAD_DIM
CHUNK_B = 128
H_C = 4
DK_C = 64
DV_C = 64
D_C = H_C * DV_C
D_IN = 2304
D_FF = 2816
N_MOD = 9
LN_EPS = 1e-5
RMS_EPS = 1e-6
NEG_BIG = -1e30
PAST_LEN = 16384

OFF_Q = 0
OFF_K = OFF_Q + D_A
OFF_V = OFF_K + D_KV
OFF_U = OFF_V + D_KV
OFF_GV = OFF_U + D_B
OFF_CQ = OFF_GV + D_B
OFF_CF = OFF_CQ + D_C
OFF_CI = OFF_CF + D_C
OFF_CG = OFF_CI + D_C

SUBLANES = 8
LANES = 128
MXU_DIM = 256
VMEM_LIMIT_BYTES = 56 * 1024 * 1024

FFN_ROWS = 1024
FF_CHUNK = MXU_DIM
ADA_COLS = 1024
MIX_ROWS = 256
HG_CHUNK = 16
HG_PAIR_SLABS = HG_CHUNK // 2 + HG_CHUNK // 4
MOD_ROWS = SUBLANES
SAMPLE_SEQS = 32


def _dot(a, b):
    return jnp.dot(a, b, preferred_element_type=F32)


def _dot_t(a, b):
    return lax.dot_general(a, b, (((1,), (1,)), ((), ())), preferred_element_type=F32)


def _tdot(a, b):
    return lax.dot_general(a, b, (((0,), (0,)), ((), ())), preferred_element_type=F32)


def _layer_norm(z, g, b):
    mu = jnp.mean(z, axis=-1, keepdims=True)
    zc = z - mu
    var = jnp.mean(zc * zc, axis=-1, keepdims=True)
    return zc * lax.rsqrt(var + LN_EPS) * g + b


def _silu(x):
    return x * jax.nn.sigmoid(x)


def _head_block_ones():
    r = lax.broadcasted_iota(jnp.int32, (D_C, D_C), 0) >> 6
    c = lax.broadcasted_iota(jnp.int32, (D_C, D_C), 1) >> 6
    return jnp.where(r == c, 1.0, 0.0).astype(BF16)


def _head_sum(x, ones_bd):
    hi = x.astype(BF16)
    lo = (x - hi.astype(F32)).astype(BF16)
    return _dot(hi, ones_bd) + _dot(lo, ones_bd)


def _rope(x, cos, sin_lo, sin_hi):
    n = x.shape[-1]
    reps = n // LANES
    if reps > 1:
        cos = jnp.concatenate([cos] * reps, axis=-1)
        sin_lo = jnp.concatenate([sin_lo] * reps, axis=-1)
        sin_hi = jnp.concatenate([sin_hi] * reps, axis=-1)
    half = ROT_DIM // 2
    return x * cos + pltpu.roll(x, n - half, 1) * sin_lo + pltpu.roll(x, half, 1) * sin_hi


def _hgrn_lower_bound(lb_ref, layer):
    w = lb_ref[...]
    e = jnp.exp(w - jnp.max(w, axis=0, keepdims=True))
    p = e / jnp.sum(e, axis=0, keepdims=True)
    lb = jnp.zeros((1, D_C), F32)
    for j in range(1, layer + 1):
        lb = lb + p[j:j + 1, :]
    return lb


def _adaln_kernel(c_ref, w_ref, b_ref, op_ref, os_ref):
    s = _silu(c_ref[...]).astype(BF16)
    r = _dot(s, w_ref[...].astype(BF16)) + b_ref[...]
    op_ref[...] = r[:MOD_ROWS]
    os_ref[...] = r[MOD_ROWS:]


def _adaln(c_all, ada_w, ada_b):
    depth, d, n = ada_w.shape
    rows = c_all.shape[0]
    n_s = rows - MOD_ROWS
    return pl.pallas_call(
        _adaln_kernel,
        grid=(depth, n // ADA_COLS),
        in_specs=[
            pl.BlockSpec((rows, d), lambda l, j: (0, 0)),
            pl.BlockSpec((None, d, ADA_COLS), lambda l, j: (l, 0, j)),
            pl.BlockSpec((None, 1, ADA_COLS), lambda l, j: (l, 0, j)),
        ],
        out_specs=[
            pl.BlockSpec((None, MOD_ROWS, ADA_COLS), lambda l, j: (l, 0, j)),
            pl.BlockSpec((None, n_s, ADA_COLS), lambda l, j: (l, 0, j)),
        ],
        out_shape=[
            jax.ShapeDtypeStruct((depth, MOD_ROWS, n), F32),
            jax.ShapeDtypeStruct((depth, n_s, n), F32),
        ],
        compiler_params=pltpu.CompilerParams(
            dimension_semantics=("arbitrary", "arbitrary"), vmem_limit_bytes=VMEM_LIMIT_BYTES),
        name="adaln",
    )(c_all, ada_w, ada_b.reshape(depth, 1, n))


def _mod_rows(ref, per_row, batch):
    return ref[...] if per_row else ref[pl.ds(batch, 1), :]


def _ffn_kernel(x_ref, sh_ref, sc_ref, gt_ref, win_ref, wout_ref, lng_ref, lnb_ref, o_ref, act_ref,
                *, per_row, tiles_per_batch, alpha):
    batch = pl.program_id(0) // tiles_per_batch
    x = x_ref[...]
    shift = _mod_rows(sh_ref, per_row, batch)
    scale = _mod_rows(sc_ref, per_row, batch)
    gate = _mod_rows(gt_ref, per_row, batch)
    h = (x * (1.0 + scale) + shift).astype(BF16)
    for c in range(D_FF // FF_CHUNK):
        lo = c * FF_CHUNK
        a = _dot(h, win_ref[:, lo:lo + FF_CHUNK])
        g = _dot(h, win_ref[:, D_FF + lo:D_FF + lo + FF_CHUNK])
        act_ref[:, lo:lo + FF_CHUNK] = (_silu(g) * a).astype(BF16)
    y = _dot(act_ref[...], wout_ref[...])
    z = alpha * x + (1.0 + gate) * (0.5 * y)
    o_ref[...] = _layer_norm(z, lng_ref[...], lnb_ref[...])


def _ffn(x, mods, layer, sub, w_in, w_out, ln_g, ln_b, *, per_row, rows_per_batch, alpha):
    m, d = x.shape
    tm = min(FFN_ROWS, rows_per_batch)
    assert m % tm == 0 and rows_per_batch % tm == 0
    mod_rows = mods.shape[1]
    if per_row:
        mod_spec = lambda j: pl.BlockSpec((None, tm, d), lambda i: (layer, i, j))
    else:
        mod_spec = lambda j: pl.BlockSpec((None, mod_rows, d), lambda i: (layer, 0, j))
    const = dict(pipeline_mode=pl.Buffered(1))
    return pl.pallas_call(
        functools.partial(_ffn_kernel, per_row=per_row, tiles_per_batch=rows_per_batch // tm, alpha=alpha),
        grid=(m // tm,),
        in_specs=[
            pl.BlockSpec((tm, d), lambda i: (i, 0)),
            mod_spec(3 * sub), mod_spec(3 * sub + 1), mod_spec(3 * sub + 2),
            pl.BlockSpec((None, d, 2 * D_FF), lambda i: (layer, 0, 0), **const),
            pl.BlockSpec((None, D_FF, d), lambda i: (layer, 0, 0), **const),
            pl.BlockSpec((None, 1, d), lambda i: (layer * 3 + sub, 0, 0)),
            pl.BlockSpec((None, 1, d), lambda i: (layer * 3 + sub, 0, 0)),
        ],
        out_specs=pl.BlockSpec((tm, d), lambda i: (i, 0)),
        out_shape=jax.ShapeDtypeStruct((m, d), F32),
        scratch_shapes=[pltpu.VMEM((tm, D_FF), BF16)],
        compiler_params=pltpu.CompilerParams(
            dimension_semantics=("arbitrary",), vmem_limit_bytes=VMEM_LIMIT_BYTES),
        name="ffn_rows" if per_row else "ffn_batch",
    )(x, mods, mods, mods, w_in, w_out, ln_g, ln_b)


def _attention_block(q_blk, keys, vals, mask, sink_ref, layer):
    outs = []
    rowgrp = lax.broadcasted_iota(jnp.int32, (G_A * WINDOW, 1), 0) >> 7
    for j in range(KV_A):
        kj = keys[:, j * HEAD_DIM:(j + 1) * HEAD_DIM].astype(BF16)
        vj = vals[:, j * HEAD_DIM:(j + 1) * HEAD_DIM].astype(BF16)
        qj = jnp.concatenate(
            [q_blk[:, (j * G_A + g) * HEAD_DIM:(j * G_A + g + 1) * HEAD_DIM] for g in range(G_A)],
            axis=0).astype(BF16)
        s = jnp.where(mask, _dot_t(qj, kj), NEG_BIG)
        sink = jnp.zeros((G_A * WINDOW, 1), F32)
        for g in range(G_A):
            sink = jnp.where(rowgrp == g, sink_ref[layer, j * G_A + g], sink)
        m = jnp.maximum(jnp.max(s, axis=-1, keepdims=True), sink)
        p = jnp.exp(s - m)
        den = jnp.sum(p, axis=-1, keepdims=True) + jnp.exp(sink - m)
        o = _dot((p * (1.0 / den)).astype(BF16), vj)
        outs.extend(o[g * WINDOW:(g + 1) * WINDOW] for g in range(G_A))
    return jnp.concatenate(outs, axis=-1)


def _mixer_prompt_kernel(
        x_ref, sh_ref, sc_ref, gt_ref, win_ref, wout_ref, cos_ref, sl_ref, sh2_ref, sink_ref,
        gg_ref, gb_ref, ws_ref, bs_ref, lb_ref, ng_ref, lng_ref, lnb_ref,
        y_ref, kwin_ref, vwin_ref, st_out_ref,
        kprev_ref, vprev_ref, st_ref, e_ref,
        *, layer, tq, alpha):
    b_idx = pl.program_id(0)
    t_idx = pl.program_id(1)
    n_t = pl.num_programs(1)

    @pl.when(t_idx == 0)
    def _():
        kprev_ref[...] = jnp.zeros_like(kprev_ref)
        vprev_ref[...] = jnp.zeros_like(vprev_ref)
        st_ref[...] = jnp.zeros_like(st_ref)

    x = x_ref[...]
    shift = sh_ref[pl.ds(b_idx, 1), :]
    scale = sc_ref[pl.ds(b_idx, 1), :]
    gate = gt_ref[pl.ds(b_idx, 1), :]
    h = (x * (1.0 + scale) + shift).astype(BF16)
    proj = _dot(h, win_ref[...])

    cos, s_lo, s_hi = cos_ref[...], sl_ref[...], sh2_ref[...]
    q = _rope(proj[:, OFF_Q:OFF_K], cos, s_lo, s_hi) * (HEAD_DIM ** -0.5)
    k = _rope(proj[:, OFF_K:OFF_V], cos, s_lo, s_hi)
    v = proj[:, OFF_V:OFF_U]

    qi = lax.broadcasted_iota(jnp.int32, (G_A * WINDOW, 2 * WINDOW), 0) & (WINDOW - 1)
    kj = lax.broadcasted_iota(jnp.int32, (G_A * WINDOW, 2 * WINDOW), 1)
    dist = kj - qi
    band = (dist - 1).astype(jnp.uint32) < jnp.uint32(WINDOW)
    o_a = []
    for blk in range(tq // WINDOW):
        r0 = blk * WINDOW
        if blk == 0:
            k_prev, v_prev = kprev_ref[...], vprev_ref[...]
            lower = jnp.where(t_idx == 0, WINDOW, 0)
            mask = jnp.logical_and(band, kj >= lower)
        else:
            k_prev, v_prev = k[r0 - WINDOW:r0], v[r0 - WINDOW:r0]
            mask = band
        keys = jnp.concatenate([k_prev, k[r0:r0 + WINDOW]], axis=0)
        vals = jnp.concatenate([v_prev, v[r0:r0 + WINDOW]], axis=0)
        o_a.append(_attention_block(q[r0:r0 + WINDOW], keys, vals, mask, sink_ref, layer))
    o_a = jnp.concatenate(o_a, axis=0)
    kprev_ref[...] = k[tq - WINDOW:]
    vprev_ref[...] = v[tq - WINDOW:]
    kwin_ref[...] = k[tq - WINDOW:]
    vwin_ref[...] = v[tq - WINDOW:]

    vn = _layer_norm(proj[:, OFF_GV:OFF_CQ], gg_ref[...], gb_ref[...])
    u = proj[:, OFF_U:OFF_GV]
    tr = lax.broadcasted_iota(jnp.int32, (CHUNK_B, CHUNK_B), 0)
    tc = lax.broadcasted_iota(jnp.int32, (CHUNK_B, CHUNK_B), 1)
    wm = [jnp.where(tr >= tc, ws_ref[g], 0.0).astype(BF16) for g in range(G_B)]
    o_b = []
    for c in range(tq // CHUNK_B):
        vc = vn[c * CHUNK_B:(c + 1) * CHUNK_B].astype(BF16)
        mixed = jnp.concatenate(
            [_dot(wm[g], vc[:, g * HEAD_DIM:(g + 1) * HEAD_DIM]) for g in range(G_B)], axis=-1)
        o_b.append(u[c * CHUNK_B:(c + 1) * CHUNK_B] * (mixed + bs_ref[...]))
    o_b = jnp.concatenate(o_b, axis=0)

    n_ch = tq // HG_CHUNK
    half = HG_CHUNK // 2
    cq = proj[:, OFF_CQ:OFF_CF]
    ci = proj[:, OFF_CI:OFF_CG]
    lb = _hgrn_lower_bound(lb_ref, layer)
    f = lb + (1.0 - lb) * jax.nn.sigmoid(proj[:, OFF_CF:OFF_CI])
    lf = jnp.log(f)
    kc = 1.0 - f

    rt = lax.broadcasted_iota(jnp.int32, (tq, tq), 0)
    ct = lax.broadcasted_iota(jnp.int32, (tq, tq), 1)
    ltri = jnp.where(ct <= rt, jnp.where(ct >= (rt & -HG_CHUNK), 1.0, 0.0), 0.0).astype(BF16)
    lf_hi = lf.astype(BF16)
    lf_r = lf - lf_hi.astype(F32)
    lf_mid = lf_r.astype(BF16)
    lf_lo = (lf_r - lf_mid.astype(F32)).astype(BF16)
    bsum = _dot(ltri, lf_hi) + _dot(ltri, lf_mid) + _dot(ltri, lf_lo)

    def chunks(a):
        return a.reshape(n_ch, HG_CHUNK, D_C)

    b3, cq3, kc3, ci3 = chunks(bsum), chunks(cq), chunks(kc), chunks(ci)
    b_end = b3[:, HG_CHUNK - 1:HG_CHUNK, :]
    eb = jnp.exp(bsum)
    qb = cq * eb
    kb = kc * jnp.exp(b_end - b3).reshape(tq, D_C)

    def pair_products(s, lo):
        e = cq3[:, lo:] * kc3[:, s:s + 1, :] * jnp.exp(b3[:, lo:] - b3[:, s:s + 1, :])
        pos = lax.broadcasted_iota(jnp.int32, (n_ch, HG_CHUNK - lo, D_C), 1) + lo
        return jnp.where(pos >= s, e, 0.0)

    for s in range(half):
        e_ref[s * tq:(s + 1) * tq, :] = pair_products(s, 0).reshape(tq, D_C).astype(BF16)
    for j in range(half // 2):
        s = half + 2 * j
        slab = jnp.concatenate([pair_products(s, half), pair_products(s + 1, half)], axis=1)
        e_ref[(half + j) * tq:(half + j + 1) * tq, :] = slab.reshape(tq, D_C).astype(BF16)
    ones_bd = _head_block_ones()
    att = _dot(e_ref[...], ones_bd)
    o_all = jnp.zeros((n_ch, HG_CHUNK, D_C), F32)
    for s in range(half):
        o_all = o_all + chunks(att[s * tq:(s + 1) * tq]) * ci3[:, s:s + 1, :]
    o_up = jnp.zeros((n_ch, half, D_C), F32)
    for j in range(half // 2):
        s = half + 2 * j
        a3 = chunks(att[(half + j) * tq:(half + j + 1) * tq])
        o_up = o_up + a3[:, :half] * ci3[:, s:s + 1, :] + a3[:, half:] * ci3[:, s + 1:s + 2, :]
    o_c = (o_all + jnp.concatenate([jnp.zeros((n_ch, half, D_C), F32), o_up], axis=1)).reshape(tq, D_C)

    lane_head = lax.broadcasted_iota(jnp.int32, (tq, D_C), 1) >> 6
    im_h = [jnp.where(lane_head == hh, ci, 0.0).astype(BF16) for hh in range(H_C)]
    qb_h = [qb[:, hh * DK_C:(hh + 1) * DK_C].astype(BF16) for hh in range(H_C)]
    kb_h = [kb[:, hh * DK_C:(hh + 1) * DK_C].astype(BF16) for hh in range(H_C)]
    eb_t = eb.T
    lane_head_c = lax.broadcasted_iota(jnp.int32, (HG_CHUNK, D_C), 1) >> 6
    st = st_ref[...]
    o_inter = []
    for n in range(n_ch):
        r0, r1 = n * HG_CHUNK, (n + 1) * HG_CHUNK
        read = _dot(jnp.concatenate([a[r0:r1] for a in qb_h], axis=0), st.astype(BF16))
        o_n = jnp.zeros((HG_CHUNK, D_C), F32)
        for hh in range(H_C):
            o_n = o_n + jnp.where(lane_head_c == hh, read[hh * HG_CHUNK:(hh + 1) * HG_CHUNK], 0.0)
        o_inter.append(o_n)
        upd = _tdot(jnp.concatenate([a[r0:r1] for a in kb_h], axis=0),
                    jnp.concatenate([a[r0:r1] for a in im_h], axis=0))
        col = eb_t[:, r1 - 1:r1]
        decay = jnp.concatenate(
            [jnp.broadcast_to(col[hh * DK_C:(hh + 1) * DK_C], (DK_C, DV_C)) for hh in range(H_C)], axis=1)
        st = decay * st + upd
    st_ref[...] = st
    o_c = o_c + jnp.concatenate(o_inter, axis=0)

    @pl.when(t_idx == n_t - 1)
    def _():
        for hh in range(H_C):
            st_out_ref[hh * DK_C:(hh + 1) * DK_C, :] = st[:, hh * DV_C:(hh + 1) * DV_C]

    ms = _head_sum(o_c * o_c, ones_bd) * (1.0 / DV_C)
    o_c = o_c * lax.rsqrt(ms + RMS_EPS) * ng_ref[...] * _silu(proj[:, OFF_CG:])

    mix = _dot(jnp.concatenate([o_a, o_b, o_c], axis=-1).astype(BF16), wout_ref[...])
    z = alpha * x + (1.0 + gate) * mix
    y_ref[...] = _layer_norm(z, lng_ref[...], lnb_ref[...])


def _mixer_prompt(x, mods, layer, w_in, w_out, tables, sinks, gln_g, gln_b, ws, bs_exp, lb, ng_exp,
                  ln_g, ln_b, *, batch, seq, alpha):
    m, d = x.shape
    tq = min(MIX_ROWS, seq)
    assert seq % tq == 0 and tq % WINDOW == 0
    n_t = seq // tq
    depth = w_in.shape[0]
    cos, s_lo, s_hi = tables
    mod_spec = lambda j: pl.BlockSpec((None, MOD_ROWS, d), lambda b, t: (layer, 0, j))
    tab_spec = pl.BlockSpec((tq, LANES), lambda b, t: (t, 0))
    lay3 = lambda shape: pl.BlockSpec((None,) + shape, lambda b, t: (layer,) + (0,) * len(shape))
    return pl.pallas_call(
        functools.partial(_mixer_prompt_kernel, layer=layer, tq=tq, alpha=alpha),
        grid=(batch, n_t),
        in_specs=[
            pl.BlockSpec((tq, d), lambda b, t: (b * n_t + t, 0)),
            mod_spec(3), mod_spec(4), mod_spec(5),
            lay3((d, D_IN)), lay3((d, d)),
            tab_spec, tab_spec, tab_spec,
            pl.BlockSpec(memory_space=pltpu.SMEM),
            lay3((1, D_B)), lay3((1, D_B)),
            lay3((G_B, CHUNK_B, CHUNK_B)), lay3((CHUNK_B, D_B)),
            pl.BlockSpec((depth, D_C), lambda b, t: (0, 0)),
            lay3((1, D_C)),
            pl.BlockSpec((None, 1, d), lambda b, t: (layer * 3 + 1, 0, 0)),
            pl.BlockSpec((None, 1, d), lambda b, t: (layer * 3 + 1, 0, 0)),
        ],
        out_specs=[
            pl.BlockSpec((tq, d), lambda b, t: (b * n_t + t, 0)),
            pl.BlockSpec((None, WINDOW, D_KV), lambda b, t: (b, 0, 0)),
            pl.BlockSpec((None, WINDOW, D_KV), lambda b, t: (b, 0, 0)),
            pl.BlockSpec((None, D_C, DV_C), lambda b, t: (b, 0, 0)),
        ],
        out_shape=[
            jax.ShapeDtypeStruct((m, d), F32),
            jax.ShapeDtypeStruct((batch, WINDOW, D_KV), F32),
            jax.ShapeDtypeStruct((batch, WINDOW, D_KV), F32),
            jax.ShapeDtypeStruct((batch, D_C, DV_C), F32),
        ],
        scratch_shapes=[
            pltpu.VMEM((WINDOW, D_KV), F32),
            pltpu.VMEM((WINDOW, D_KV), F32),
            pltpu.VMEM((DK_C, D_C), F32),
            pltpu.VMEM((HG_PAIR_SLABS * tq, D_C), BF16),
        ],
        compiler_params=pltpu.CompilerParams(
            dimension_semantics=("arbitrary", "arbitrary"), vmem_limit_bytes=VMEM_LIMIT_BYTES),
        name="mixer_prompt",
    )(x, mods, mods, mods, w_in, w_out, cos, s_lo, s_hi, sinks, gln_g, gln_b, ws, bs_exp, lb, ng_exp,
      ln_g, ln_b)


def _mixer_sample_kernel(
        x_ref, sh_ref, sc_ref, gt_ref, win_ref, wout_ref, cos_ref, sl_ref, sh2_ref, sink_ref,
        gg_ref, gb_ref, w0_ref, bs_ref, lb_ref, ng_ref, lng_ref, lnb_ref, ck_ref, cv_ref, s0_ref,
        y_ref, knew_ref, vnew_ref, gv_ref, s1_ref,
        qt_ref, sc_scr, oc_ref,
        *, layer, nb, alpha):
    x = x_ref[...]
    h = (x * (1.0 + sc_ref[...]) + sh_ref[...]).astype(BF16)
    proj = _dot(h, win_ref[...])
    cos, s_lo, s_hi = cos_ref[...], sl_ref[...], sh2_ref[...]
    q = _rope(proj[:, OFF_Q:OFF_K], cos, s_lo, s_hi) * (HEAD_DIM ** -0.5)
    k = _rope(proj[:, OFF_K:OFF_V], cos, s_lo, s_hi)
    v = proj[:, OFF_V:OFF_U]
    knew_ref[...] = k
    vnew_ref[...] = v

    zeros64 = jnp.zeros((nb, HEAD_DIM), F32)
    for hd in range(H_A):
        qh = q[:, hd * HEAD_DIM:(hd + 1) * HEAD_DIM]
        row = jnp.concatenate([qh, zeros64] if hd < G_A else [zeros64, qh], axis=-1)
        qt_ref[pl.ds(hd, nb, stride=H_A), :] = row
    row0 = lax.broadcasted_iota(jnp.int32, (WINDOW, D_KV), 0) == 0
    for b in range(nb):
        kb = jnp.where(row0, k[b:b + 1, :], ck_ref[b]).astype(BF16)
        sc_scr[b * H_A:(b + 1) * H_A, :] = _dot_t(qt_ref[b * H_A:(b + 1) * H_A, :].astype(BF16), kb)
    s = sc_scr[...]
    hrow = lax.broadcasted_iota(jnp.int32, (nb * H_A, 1), 0) & (H_A - 1)
    sink = jnp.zeros((nb * H_A, 1), F32)
    for hd in range(H_A):
        sink = jnp.where(hrow == hd, sink_ref[layer, hd], sink)
    mx = jnp.maximum(jnp.max(s, axis=-1, keepdims=True), sink)
    p = jnp.exp(s - mx)
    den = jnp.sum(p, axis=-1, keepdims=True) + jnp.exp(sink - mx)
    sc_scr[...] = p * (1.0 / den)
    for b in range(nb):
        vb = jnp.where(row0, v[b:b + 1, :], cv_ref[b]).astype(BF16)
        qt_ref[b * H_A:(b + 1) * H_A, :] = _dot(sc_scr[b * H_A:(b + 1) * H_A, :].astype(BF16), vb)
    o_a = []
    for hd in range(H_A):
        rows = qt_ref[pl.ds(hd, nb, stride=H_A), :]
        o_a.append(rows[:, :HEAD_DIM] if hd < G_A else rows[:, HEAD_DIM:])
    o_a = jnp.concatenate(o_a, axis=-1)

    vn = _layer_norm(proj[:, OFF_GV:OFF_CQ], gg_ref[...], gb_ref[...])
    gv_ref[...] = vn
    o_b = proj[:, OFF_U:OFF_GV] * (vn * w0_ref[...] + bs_ref[0:1, :])

    lb = _hgrn_lower_bound(lb_ref, layer)
    f = lb + (1.0 - lb) * jax.nn.sigmoid(proj[:, OFF_CF:OFF_CI])
    cq = proj[:, OFF_CQ:OFF_CF]
    ci = proj[:, OFF_CI:OFF_CG]
    zrows = jnp.zeros((LANES - nb, D_C), F32)

    def columns(a):
        return jnp.concatenate([a, zrows], axis=0).T

    f_t, k_t, q_t = columns(f), columns(1.0 - f), columns(cq)
    for b in range(nb):
        i_rows = jnp.concatenate(
            [jnp.broadcast_to(ci[b:b + 1, hh * DV_C:(hh + 1) * DV_C], (DK_C, DV_C)) for hh in range(H_C)],
            axis=0)
        s1 = f_t[:, b:b + 1] * s0_ref[b] + k_t[:, b:b + 1] * i_rows
        s1_ref[b] = s1
        w = q_t[:, b:b + 1] * s1
        oc_ref[b:b + 1, :] = jnp.concatenate(
            [jnp.sum(w[hh * DK_C:(hh + 1) * DK_C], axis=0, keepdims=True) for hh in range(H_C)], axis=-1)
    o_c = oc_ref[...]
    ones_bd = _head_block_ones()
    ms = _head_sum(o_c * o_c, ones_bd) * (1.0 / DV_C)
    o_c = o_c * lax.rsqrt(ms + RMS_EPS) * ng_ref[...] * _silu(proj[:, OFF_CG:])

    mix = _dot(jnp.concatenate([o_a, o_b, o_c], axis=-1).astype(BF16), wout_ref[...])
    z = alpha * x + (1.0 + gt_ref[...]) * mix
    y_ref[...] = _layer_norm(z, lng_ref[...], lnb_ref[...])


def _mixer_sample(x, mods, layer, w_in, w_out, tables, sinks, gln_g, gln_b, w0_exp, bs_exp, lb, ng_exp,
                  ln_g, ln_b, cache_k, cache_v, state, *, alpha):
    m, d = x.shape
    nb = min(SAMPLE_SEQS, m)
    assert m % nb == 0 and nb % SUBLANES == 0 and nb <= LANES
    depth = w_in.shape[0]
    cos, s_lo, s_hi = tables
    mod_spec = lambda j: pl.BlockSpec((None, nb, d), lambda i: (layer, i, j))
    tab_spec = pl.BlockSpec((1, LANES), lambda i: (0, 0))
    lay3 = lambda shape: pl.BlockSpec((None,) + shape, lambda i: (layer,) + (0,) * len(shape))
    return pl.pallas_call(
        functools.partial(_mixer_sample_kernel, layer=layer, nb=nb, alpha=alpha),
        grid=(m // nb,),
        in_specs=[
            pl.BlockSpec((nb, d), lambda i: (i, 0)),
            mod_spec(3), mod_spec(4), mod_spec(5),
            lay3((d, D_IN)), lay3((d, d)),
            tab_spec, tab_spec, tab_spec,
            pl.BlockSpec(memory_space=pltpu.SMEM),
            lay3((1, D_B)), lay3((1, D_B)), lay3((1, D_B)), lay3((CHUNK_B, D_B)),
            pl.BlockSpec((depth, D_C), lambda i: (0, 0)),
            lay3((1, D_C)),
            pl.BlockSpec((None, 1, d), lambda i: (layer * 3 + 1, 0, 0)),
            pl.BlockSpec((None, 1, d), lambda i: (layer * 3 + 1, 0, 0)),
            pl.BlockSpec((None, nb, WINDOW, D_KV), lambda i: (layer, i, 0, 0)),
            pl.BlockSpec((None, nb, WINDOW, D_KV), lambda i: (layer, i, 0, 0)),
            pl.BlockSpec((None, nb, D_C, DV_C), lambda i: (layer, i, 0, 0)),
        ],
        out_specs=[
            pl.BlockSpec((nb, d), lambda i: (i, 0)),
            pl.BlockSpec((nb, D_KV), lambda i: (i, 0)),
            pl.BlockSpec((nb, D_KV), lambda i: (i, 0)),
            pl.BlockSpec((nb, D_B), lambda i: (i, 0)),
            pl.BlockSpec((nb, D_C, DV_C), lambda i: (i, 0, 0)),
        ],
        out_shape=[
            jax.ShapeDtypeStruct((m, d), F32),
            jax.ShapeDtypeStruct((m, D_KV), F32),
            jax.ShapeDtypeStruct((m, D_KV), F32),
            jax.ShapeDtypeStruct((m, D_B), F32),
            jax.ShapeDtypeStruct((m, D_C, DV_C), F32),
        ],
        scratch_shapes=[
            pltpu.VMEM((nb * H_A, D_KV), F32),
            pltpu.VMEM((nb * H_A, WINDOW), F32),
            pltpu.VMEM((nb, D_C), F32),
        ],
        compiler_params=pltpu.CompilerParams(
            dimension_semantics=("arbitrary",), vmem_limit_bytes=VMEM_LIMIT_BYTES),
        name="mixer_sample",
    )(x, mods, mods, mods, w_in, w_out, cos, s_lo, s_hi, sinks, gln_g, gln_b, w0_exp, bs_exp, lb, ng_exp,
      ln_g, ln_b, cache_k, cache_v, state)


def _rope_tables(positions):
    half = ROT_DIM // 2
    f32 = np.float32
    inv = (ROPE_THETA ** (-np.arange(half, dtype=f32) * 2.0 / ROT_DIM)).astype(f32)
    ang = (positions.astype(f32)[:, None] * inv[None, :]).astype(f32)
    cos, sin = np.cos(ang).astype(f32), np.sin(ang).astype(f32)
    n = positions.shape[0]
    rest = HEAD_DIM - ROT_DIM
    cos_h = np.concatenate([cos, cos, np.ones((n, rest), f32)], axis=-1)
    lo_h = np.concatenate([-sin, np.zeros((n, half + rest), f32)], axis=-1)
    hi_h = np.concatenate([np.zeros((n, half), f32), sin, np.zeros((n, rest), f32)], axis=-1)
    two = lambda a: jnp.asarray(np.concatenate([a, a], axis=-1))
    return two(cos_h), two(lo_h), two(hi_h)


def kernel(x_prompt, x_sample, cache_k, cache_v, state_hgrn, c_prompt, c_sample, w_in, w_out, attn_sinks,
           gmlp_ln_g, gmlp_ln_b, gmlp_ws, gmlp_bs, hgrn_lb, hgrn_norm_g, ffn1_in, ffn1_out, ffn2_in,
           ffn2_out, ada_w, ada_b, ln_g, ln_b):
    batch, seq, d = x_prompt.shape
    n_s, dec_seq, _ = x_sample.shape
    depth = w_in.shape[0]
    assert d == D_MODEL and dec_seq == 1 and batch <= MOD_ROWS
    assert cache_k.shape[2] == WINDOW
    alpha = (2 * depth) ** 0.25

    w_in_b, w_out_b = w_in.astype(BF16), w_out.astype(BF16)
    f1i, f1o, f2i, f2o = (a.astype(BF16) for a in (ffn1_in, ffn1_out, ffn2_in, ffn2_out))
    ln_g3 = ln_g.reshape(depth * 3, 1, d)
    ln_b3 = ln_b.reshape(depth * 3, 1, d)
    gln_g = gmlp_ln_g.reshape(depth, 1, D_B)
    gln_b = gmlp_ln_b.reshape(depth, 1, D_B)
    bs_exp = jnp.repeat(jnp.swapaxes(gmlp_bs, 1, 2), HEAD_DIM, axis=2)
    w0_exp = jnp.repeat(gmlp_ws[:, :, 0, 0], HEAD_DIM, axis=1).reshape(depth, 1, D_B)
    ng_exp = jnp.tile(hgrn_norm_g, (1, H_C)).reshape(depth, 1, D_C)
    tab_p = _rope_tables(np.arange(seq))
    tab_s = _rope_tables(PAST_LEN + np.arange(dec_seq))

    c_all = jnp.concatenate([c_prompt, jnp.zeros((MOD_ROWS - batch, d), F32), c_sample], axis=0)
    mod_p, mod_s = _adaln(c_all, ada_w, ada_b)

    ck = cache_k.reshape(depth, n_s, WINDOW, D_KV)
    cv = cache_v.reshape(depth, n_s, WINDOW, D_KV)
    s0 = state_hgrn.reshape(depth, n_s, D_C, DV_C)

    xp = x_prompt.reshape(batch * seq, d)
    xs = x_sample.reshape(n_s, d)
    ffn_p = functools.partial(_ffn, per_row=False, rows_per_batch=seq, alpha=alpha)
    ffn_s = functools.partial(_ffn, per_row=True, rows_per_batch=n_s, alpha=alpha)
    kw, vw, st_p, kn, vnw, gvs, st_s = [], [], [], [], [], [], []
    for l in range(depth):
        xp = ffn_p(xp, mod_p, l, 0, f1i, f1o, ln_g3, ln_b3)
        xp, k_l, v_l, s_l = _mixer_prompt(
            xp, mod_p, l, w_in_b, w_out_b, tab_p, attn_sinks, gln_g, gln_b, gmlp_ws, bs_exp, hgrn_lb,
            ng_exp, ln_g3, ln_b3, batch=batch, seq=seq, alpha=alpha)
        xp = ffn_p(xp, mod_p, l, 2, f2i, f2o, ln_g3, ln_b3)
        kw.append(k_l), vw.append(v_l), st_p.append(s_l)

        xs = ffn_s(xs, mod_s, l, 0, f1i, f1o, ln_g3, ln_b3)
        xs, k_l, v_l, g_l, s_l = _mixer_sample(
            xs, mod_s, l, w_in_b, w_out_b, tab_s, attn_sinks, gln_g, gln_b, w0_exp, bs_exp, hgrn_lb,
            ng_exp, ln_g3, ln_b3, ck, cv, s0, alpha=alpha)
        xs = ffn_s(xs, mod_s, l, 2, f2i, f2o, ln_g3, ln_b3)
        kn.append(k_l), vnw.append(v_l), gvs.append(g_l), st_s.append(s_l)

    stack = lambda parts, shape: jnp.stack(parts, axis=0).reshape((depth,) + shape)
    return (
        xp.reshape(batch, seq, d),
        xs.reshape(n_s, dec_seq, d),
        stack(kw, (batch, WINDOW, KV_A, HEAD_DIM)),
        stack(vw, (batch, WINDOW, KV_A, HEAD_DIM)),
        stack(st_p, (batch, H_C, DK_C, DV_C)),
        stack(kn, (n_s, dec_seq, KV_A, HEAD_DIM)),
        stack(vnw, (n_s, dec_seq, KV_A, HEAD_DIM)),
        stack(gvs, (n_s, dec_seq, D_B)),
        stack(st_s, (n_s, H_C, DK_C, DV_C)),
    )
```

```python
import functools

import numpy as np
import jax
import jax.numpy as jnp
from jax import lax
from jax.experimental import pallas as pl
from jax.experimental.pallas import tpu as pltpu

F32 = jnp.float32
BF16 = jnp.bfloat16

D_MODEL = 1024
HEAD_DIM = 64
H_A = 8
KV_A = 2
G_A = H_A // KV_A
D_A = H_A * HEAD_DIM
D_KV = KV_A * HEAD_DIM
WINDOW = 128
ROT_DIM = HEAD_DIM // 4
ROPE_THETA = 500000.0
G_B = 4
D_B = G_B * HEAD_DIM
CHUNK_B = 128
H_C = 4
DK_C = 64
DV_C = 64
D_C = H_C * DV_C
D_IN = 2304
D_FF = 2816
N_MOD = 9
LN_EPS = 1e-5
RMS_EPS = 1e-6
NEG_BIG = -1e30
PAST_LEN = 16384

OFF_Q = 0
OFF_K = OFF_Q + D_A
OFF_V = OFF_K + D_KV
OFF_U = OFF_V + D_KV
OFF_GV = OFF_U + D_B
OFF_CQ = OFF_GV + D_B
OFF_CF = OFF_CQ + D_C
OFF_CI = OFF_CF + D_C
OFF_CG = OFF_CI + D_C

SUBLANES = 8
LANES = 128
MXU_DIM = 256
VMEM_LIMIT_BYTES = 56 * 1024 * 1024

FFN_ROWS = 1024
FF_CHUNK = MXU_DIM
ADA_COLS = 1024
MIX_ROWS = 256
HG_CHUNK = 16
HG_PAIR_SLABS = HG_CHUNK // 2 + HG_CHUNK // 4
MOD_ROWS = SUBLANES


def _dot(a, b):
    return jnp.dot(a, b, preferred_element_type=F32)


def _dot_t(a, b):
    return lax.dot_general(a, b, (((1,), (1,)), ((), ())), preferred_element_type=F32)


def _tdot(a, b):
    return lax.dot_general(a, b, (((0,), (0,)), ((), ())), preferred_element_type=F32)


def _layer_norm(z, g, b):
    mu = jnp.mean(z, axis=-1, keepdims=True)
    zc = z - mu
    var = jnp.mean(zc * zc, axis=-1, keepdims=True)
    return zc * lax.rsqrt(var + LN_EPS) * g + b


def _silu(x):
    return x * jax.nn.sigmoid(x)


def _head_block_ones():
    r = lax.broadcasted_iota(jnp.int32, (D_C, D_C), 0) >> 6
    c = lax.broadcasted_iota(jnp.int32, (D_C, D_C), 1) >> 6
    return jnp.where(r == c, 1.0, 0.0).astype(BF16)


def _head_sum(x, ones_bd):
    hi = x.astype(BF16)
    lo = (x - hi.astype(F32)).astype(BF16)
    return _dot(hi, ones_bd) + _dot(lo, ones_bd)


def _rope(x, cos, sin_lo, sin_hi):
    n = x.shape[-1]
    reps = n // LANES
    if reps > 1:
        cos = jnp.concatenate([cos] * reps, axis=-1)
        sin_lo = jnp.concatenate([sin_lo] * reps, axis=-1)
        sin_hi = jnp.concatenate([sin_hi] * reps, axis=-1)
    half = ROT_DIM // 2
    return x * cos + pltpu.roll(x, n - half, 1) * sin_lo + pltpu.roll(x, half, 1) * sin_hi


def _hgrn_lower_bound(lb_ref, layer):
    w = lb_ref[...]
    e = jnp.exp(w - jnp.max(w, axis=0, keepdims=True))
    p = e / jnp.sum(e, axis=0, keepdims=True)
    lb = jnp.zeros((1, D_C), F32)
    for j in range(1, layer + 1):
        lb = lb + p[j:j + 1, :]
    return lb


def _adaln_kernel(c_ref, w_ref, b_ref, op_ref, os_ref):
    s = _silu(c_ref[...]).astype(BF16)
    r = _dot(s, w_ref[...].astype(BF16)) + b_ref[...]
    op_ref[...] = r[:MOD_ROWS]
    os_ref[...] = r[MOD_ROWS:]


def _adaln(c_all, ada_w, ada_b):
    depth, d, n = ada_w.shape
    rows = c_all.shape[0]
    n_s = rows - MOD_ROWS
    return pl.pallas_call(
        _adaln_kernel,
        grid=(depth, n // ADA_COLS),
        in_specs=[
            pl.BlockSpec((rows, d), lambda l, j: (0, 0)),
            pl.BlockSpec((None, d, ADA_COLS), lambda l, j: (l, 0, j)),
            pl.BlockSpec((None, 1, ADA_COLS), lambda l, j: (l, 0, j)),
        ],
        out_specs=[
            pl.BlockSpec((None, MOD_ROWS, ADA_COLS), lambda l, j: (l, 0, j)),
            pl.BlockSpec((None, n_s, ADA_COLS), lambda l, j: (l, 0, j)),
        ],
        out_shape=[
            jax.ShapeDtypeStruct((depth, MOD_ROWS, n), F32),
            jax.ShapeDtypeStruct((depth, n_s, n), F32),
        ],
        compiler_params=pltpu.CompilerParams(
            dimension_semantics=("arbitrary", "arbitrary"), vmem_limit_bytes=VMEM_LIMIT_BYTES),
        name="adaln",
    )(c_all, ada_w, ada_b.reshape(depth, 1, n))


def _mod_rows(ref, per_row, batch):
    return ref[...] if per_row else ref[pl.ds(batch, 1), :]


def _ffn_kernel(x_ref, sh_ref, sc_ref, gt_ref, win_ref, wout_ref, lng_ref, lnb_ref, o_ref, act_ref,
                *, per_row, tiles_per_batch, alpha):
    batch = pl.program_id(0) // tiles_per_batch
    x = x_ref[...]
    shift = _mod_rows(sh_ref, per_row, batch)
    scale = _mod_rows(sc_ref, per_row, batch)
    gate = _mod_rows(gt_ref, per_row, batch)
    h = (x * (1.0 + scale) + shift).astype(BF16)
    for c in range(D_FF // FF_CHUNK):
        lo = c * FF_CHUNK
        a = _dot(h, win_ref[:, lo:lo + FF_CHUNK])
        g = _dot(h, win_ref[:, D_FF + lo:D_FF + lo + FF_CHUNK])
        act_ref[:, lo:lo + FF_CHUNK] = (_silu(g) * a).astype(BF16)
    y = _dot(act_ref[...], wout_ref[...])
    z = alpha * x + (1.0 + gate) * (0.5 * y)
    o_ref[...] = _layer_norm(z, lng_ref[...], lnb_ref[...])


def _ffn(x, mods, layer, sub, w_in, w_out, ln_g, ln_b, *, per_row, rows_per_batch, alpha):
    m, d = x.shape
    tm = min(FFN_ROWS, rows_per_batch)
    assert m % tm == 0 and rows_per_batch % tm == 0
    mod_rows = mods.shape[1]
    if per_row:
        mod_spec = lambda j: pl.BlockSpec((None, tm, d), lambda i: (layer, i, j))
    else:
        mod_spec = lambda j: pl.BlockSpec((None, mod_rows, d), lambda i: (layer, 0, j))
    const = dict(pipeline_mode=pl.Buffered(1))
    return pl.pallas_call(
        functools.partial(_ffn_kernel, per_row=per_row, tiles_per_batch=rows_per_batch // tm, alpha=alpha),
        grid=(m // tm,),
        in_specs=[
            pl.BlockSpec((tm, d), lambda i: (i, 0)),
            mod_spec(3 * sub), mod_spec(3 * sub + 1), mod_spec(3 * sub + 2),
            pl.BlockSpec((None, d, 2 * D_FF), lambda i: (layer, 0, 0), **const),
            pl.BlockSpec((None, D_FF, d), lambda i: (layer, 0, 0), **const),
            pl.BlockSpec((None, 1, d), lambda i: (layer * 3 + sub, 0, 0)),
            pl.BlockSpec((None, 1, d), lambda i: (layer * 3 + sub, 0, 0)),
        ],
        out_specs=pl.BlockSpec((tm, d), lambda i: (i, 0)),
        out_shape=jax.ShapeDtypeStruct((m, d), F32),
        scratch_shapes=[pltpu.VMEM((tm, D_FF), BF16)],
        compiler_params=pltpu.CompilerParams(
            dimension_semantics=("arbitrary",), vmem_limit_bytes=VMEM_LIMIT_BYTES),
        name="ffn_rows" if per_row else "ffn_batch",
    )(x, mods, mods, mods, w_in, w_out, ln_g, ln_b)


def _attention_block(q_blk, keys, vals, mask, sink_ref, layer):
    outs = []
    rowgrp = lax.broadcasted_iota(jnp.int32, (G_A * WINDOW, 1), 0) >> 7
    for j in range(KV_A):
        kj = keys[:, j * HEAD_DIM:(j + 1) * HEAD_DIM].astype(BF16)
        vj = vals[:, j * HEAD_DIM:(j + 1) * HEAD_DIM].astype(BF16)
        qj = jnp.concatenate(
            [q_blk[:, (j * G_A + g) * HEAD_DIM:(j * G_A + g + 1) * HEAD_DIM] for g in range(G_A)],
            axis=0).astype(BF16)
        s = jnp.where(mask, _dot_t(qj, kj), NEG_BIG)
        sink = jnp.zeros((G_A * WINDOW, 1), F32)
        for g in range(G_A):
            sink = jnp.where(rowgrp == g, sink_ref[layer, j * G_A + g], sink)
        m = jnp.maximum(jnp.max(s, axis=-1, keepdims=True), sink)
        p = jnp.exp(s - m)
        den = jnp.sum(p, axis=-1, keepdims=True) + jnp.exp(sink - m)
        o = _dot((p * (1.0 / den)).astype(BF16), vj)
        outs.extend(o[g * WINDOW:(g + 1) * WINDOW] for g in range(G_A))
    return jnp.concatenate(outs, axis=-1)


def _mixer_prompt_kernel(
        x_ref, sh_ref, sc_ref, gt_ref, win_ref, wout_ref, cos_ref, sl_ref, sh2_ref, sink_ref,
        gg_ref, gb_ref, ws_ref, bs_ref, lb_ref, ng_ref, lng_ref, lnb_ref,
        y_ref, kwin_ref, vwin_ref, st_out_ref,
        kprev_ref, vprev_ref, st_ref, e_ref,
        *, layer, tq, alpha):
    b_idx = pl.program_id(0)
    t_idx = pl.program_id(1)
    n_t = pl.num_programs(1)

    @pl.when(t_idx == 0)
    def _():
        kprev_ref[...] = jnp.zeros_like(kprev_ref)
        vprev_ref[...] = jnp.zeros_like(vprev_ref)
        st_ref[...] = jnp.zeros_like(st_ref)

    x = x_ref[...]
    shift = sh_ref[pl.ds(b_idx, 1), :]
    scale = sc_ref[pl.ds(b_idx, 1), :]
    gate = gt_ref[pl.ds(b_idx, 1), :]
    h = (x * (1.0 + scale) + shift).astype(BF16)
    proj = _dot(h, win_ref[...])

    cos, s_lo, s_hi = cos_ref[...], sl_ref[...], sh2_ref[...]
    q = _rope(proj[:, OFF_Q:OFF_K], cos, s_lo, s_hi) * (HEAD_DIM ** -0.5)
    k = _rope(proj[:, OFF_K:OFF_V], cos, s_lo, s_hi)
    v = proj[:, OFF_V:OFF_U]

    qi = lax.broadcasted_iota(jnp.int32, (G_A * WINDOW, 2 * WINDOW), 0) & (WINDOW - 1)
    kj = lax.broadcasted_iota(jnp.int32, (G_A * WINDOW, 2 * WINDOW), 1)
    dist = kj - qi
    band = (dist - 1).astype(jnp.uint32) < jnp.uint32(WINDOW)
    o_a = []
    for blk in range(tq // WINDOW):
        r0 = blk * WINDOW
        if blk == 0:
            k_prev, v_prev = kprev_ref[...], vprev_ref[...]
            lower = jnp.where(t_idx == 0, WINDOW, 0)
            mask = jnp.logical_and(band, kj >= lower)
        else:
            k_prev, v_prev = k[r0 - WINDOW:r0], v[r0 - WINDOW:r0]
            mask = band
        keys = jnp.concatenate([k_prev, k[r0:r0 + WINDOW]], axis=0)
        vals = jnp.concatenate([v_prev, v[r0:r0 + WINDOW]], axis=0)
        o_a.append(_attention_block(q[r0:r0 + WINDOW], keys, vals, mask, sink_ref, layer))
    o_a = jnp.concatenate(o_a, axis=0)
    kprev_ref[...] = k[tq - WINDOW:]
    vprev_ref[...] = v[tq - WINDOW:]
    kwin_ref[...] = k[tq - WINDOW:]
    vwin_ref[...] = v[tq - WINDOW:]

    vn = _layer_norm(proj[:, OFF_GV:OFF_CQ], gg_ref[...], gb_ref[...])
    u = proj[:, OFF_U:OFF_GV]
    tr = lax.broadcasted_iota(jnp.int32, (CHUNK_B, CHUNK_B), 0)
    tc = lax.broadcasted_iota(jnp.int32, (CHUNK_B, CHUNK_B), 1)
    wm = [jnp.where(tr >= tc, ws_ref[g], 0.0).astype(BF16) for g in range(G_B)]
    o_b = []
    for c in range(tq // CHUNK_B):
        vc = vn[c * CHUNK_B:(c + 1) * CHUNK_B].astype(BF16)
        mixed = jnp.concatenate(
            [_dot(wm[g], vc[:, g * HEAD_DIM:(g + 1) * HEAD_DIM]) for g in range(G_B)], axis=-1)
        o_b.append(u[c * CHUNK_B:(c + 1) * CHUNK_B] * (mixed + bs_ref[...]))
    o_b = jnp.concatenate(o_b, axis=0)

    n_ch = tq // HG_CHUNK
    half = HG_CHUNK // 2
    cq = proj[:, OFF_CQ:OFF_CF]
    ci = proj[:, OFF_CI:OFF_CG]
    lb = _hgrn_lower_bound(lb_ref, layer)
    f = lb + (1.0 - lb) * jax.nn.sigmoid(proj[:, OFF_CF:OFF_CI])
    lf = jnp.log(f)
    kc = 1.0 - f

    rt = lax.broadcasted_iota(jnp.int32, (tq, tq), 0)
    ct = lax.broadcasted_iota(jnp.int32, (tq, tq), 1)
    ltri = jnp.where(ct <= rt, jnp.where(ct >= (rt & -HG_CHUNK), 1.0, 0.0), 0.0).astype(BF16)
    lf_hi = lf.astype(BF16)
    lf_r = lf - lf_hi.astype(F32)
    lf_mid = lf_r.astype(BF16)
    lf_lo = (lf_r - lf_mid.astype(F32)).astype(BF16)
    bsum = _dot(ltri, lf_hi) + _dot(ltri, lf_mid) + _dot(ltri, lf_lo)

    def chunks(a):
        return a.reshape(n_ch, HG_CHUNK, D_C)

    b3, cq3, kc3, ci3 = chunks(bsum), chunks(cq), chunks(kc), chunks(ci)
    b_end = b3[:, HG_CHUNK - 1:HG_CHUNK, :]
    eb = jnp.exp(bsum)
    qb = cq * eb
    kb = kc * jnp.exp(b_end - b3).reshape(tq, D_C)

    def pair_products(s, lo):
        e = cq3[:, lo:] * kc3[:, s:s + 1, :] * jnp.exp(b3[:, lo:] - b3[:, s:s + 1, :])
        pos = lax.broadcasted_iota(jnp.int32, (n_ch, HG_CHUNK - lo, D_C), 1) + lo
        return jnp.where(pos >= s, e, 0.0)

    for s in range(half):
        e_ref[s * tq:(s + 1) * tq, :] = pair_products(s, 0).reshape(tq, D_C).astype(BF16)
    for j in range(half // 2):
        s = half + 2 * j
        slab = jnp.concatenate([pair_products(s, half), pair_products(s + 1, half)], axis=1)
        e_ref[(half + j) * tq:(half + j + 1) * tq, :] = slab.reshape(tq, D_C).astype(BF16)
    ones_bd = _head_block_ones()
    att = _dot(e_ref[...], ones_bd)
    o_all = jnp.zeros((n_ch, HG_CHUNK, D_C), F32)
    for s in range(half):
        o_all = o_all + chunks(att[s * tq:(s + 1) * tq]) * ci3[:, s:s + 1, :]
    o_up = jnp.zeros((n_ch, half, D_C), F32)
    for j in range(half // 2):
        s = half + 2 * j
        a3 = chunks(att[(half + j) * tq:(half + j + 1) * tq])
        o_up = o_up + a3[:, :half] * ci3[:, s:s + 1, :] + a3[:, half:] * ci3[:, s + 1:s + 2, :]
    o_c = (o_all + jnp.concatenate([jnp.zeros((n_ch, half, D_C), F32), o_up], axis=1)).reshape(tq, D_C)

    lane_head = lax.broadcasted_iota(jnp.int32, (tq, D_C), 1) >> 6
    im_h = [jnp.where(lane_head == hh, ci, 0.0).astype(BF16) for hh in range(H_C)]
    qb_h = [qb[:, hh * DK_C:(hh + 1) * DK_C].astype(BF16) for hh in range(H_C)]
    kb_h = [kb[:, hh * DK_C:(hh + 1) * DK_C].astype(BF16) for hh in range(H_C)]
    eb_t = eb.T
    lane_head_c = lax.broadcasted_iota(jnp.int32, (HG_CHUNK, D_C), 1) >> 6
    st = st_ref[...]
    o_inter = []
    for n in range(n_ch):
        r0, r1 = n * HG_CHUNK, (n + 1) * HG_CHUNK
        read = _dot(jnp.concatenate([a[r0:r1] for a in qb_h], axis=0), st.astype(BF16))
        o_n = jnp.zeros((HG_CHUNK, D_C), F32)
        for hh in range(H_C):
            o_n = o_n + jnp.where(lane_head_c == hh, read[hh * HG_CHUNK:(hh + 1) * HG_CHUNK], 0.0)
        o_inter.append(o_n)
        upd = _tdot(jnp.concatenate([a[r0:r1] for a in kb_h], axis=0),
                    jnp.concatenate([a[r0:r1] for a in im_h], axis=0))
        col = eb_t[:, r1 - 1:r1]
        decay = jnp.concatenate(
            [jnp.broadcast_to(col[hh * DK_C:(hh + 1) * DK_C], (DK_C, DV_C)) for hh in range(H_C)], axis=1)
        st = decay * st + upd
    st_ref[...] = st
    o_c = o_c + jnp.concatenate(o_inter, axis=0)

    @pl.when(t_idx == n_t - 1)
    def _():
        for hh in range(H_C):
            st_out_ref[hh * DK_C:(hh + 1) * DK_C, :] = st[:, hh * DV_C:(hh + 1) * DV_C]

    ms = _head_sum(o_c * o_c, ones_bd) * (1.0 / DV_C)
    o_c = o_c * lax.rsqrt(ms + RMS_EPS) * ng_ref[...] * _silu(proj[:, OFF_CG:])

    mix = _dot(jnp.concatenate([o_a, o_b, o_c], axis=-1).astype(BF16), wout_ref[...])
    z = alpha * x + (1.0 + gate) * mix
    y_ref[...] = _layer_norm(z, lng_ref[...], lnb_ref[...])


def _mixer_prompt(x, mods, layer, w_in, w_out, tables, sinks, gln_g, gln_b, ws, bs_exp, lb, ng_exp,
                  ln_g, ln_b, *, batch, seq, alpha):
    m, d = x.shape
    tq = min(MIX_ROWS, seq)
    assert seq % tq == 0 and tq % WINDOW == 0
    n_t = seq // tq
    depth = w_in.shape[0]
    cos, s_lo, s_hi = tables
    mod_spec = lambda j: pl.BlockSpec((None, MOD_ROWS, d), lambda b, t: (layer, 0, j))
    tab_spec = pl.BlockSpec((tq, LANES), lambda b, t: (t, 0))
    lay3 = lambda shape: pl.BlockSpec((None,) + shape, lambda b, t: (layer,) + (0,) * len(shape))
    return pl.pallas_call(
        functools.partial(_mixer_prompt_kernel, layer=layer, tq=tq, alpha=alpha),
        grid=(batch, n_t),
        in_specs=[
            pl.BlockSpec((tq, d), lambda b, t: (b * n_t + t, 0)),
            mod_spec(3), mod_spec(4), mod_spec(5),
            lay3((d, D_IN)), lay3((d, d)),
            tab_spec, tab_spec, tab_spec,
            pl.BlockSpec(memory_space=pltpu.SMEM),
            lay3((1, D_B)), lay3((1, D_B)),
            lay3((G_B, CHUNK_B, CHUNK_B)), lay3((CHUNK_B, D_B)),
            pl.BlockSpec((depth, D_C), lambda b, t: (0, 0)),
            lay3((1, D_C)),
            pl.BlockSpec((None, 1, d), lambda b, t: (layer * 3 + 1, 0, 0)),
            pl.BlockSpec((None, 1, d), lambda b, t: (layer * 3 + 1, 0, 0)),
        ],
        out_specs=[
            pl.BlockSpec((tq, d), lambda b, t: (b * n_t + t, 0)),
            pl.BlockSpec((None, WINDOW, D_KV), lambda b, t: (b, 0, 0)),
            pl.BlockSpec((None, WINDOW, D_KV), lambda b, t: (b, 0, 0)),
            pl.BlockSpec((None, D_C, DV_C), lambda b, t: (b, 0, 0)),
        ],
        out_shape=[
            jax.ShapeDtypeStruct((m, d), F32),
            jax.ShapeDtypeStruct((batch, WINDOW, D_KV), F32),
            jax.ShapeDtypeStruct((batch, WINDOW, D_KV), F32),
            jax.ShapeDtypeStruct((batch, D_C, DV_C), F32),
        ],
        scratch_shapes=[
            pltpu.VMEM((WINDOW, D_KV), F32),
            pltpu.VMEM((WINDOW, D_KV), F32),
            pltpu.VMEM((DK_C, D_C), F32),
            pltpu.VMEM((HG_PAIR_SLABS * tq, D_C), BF16),
        ],
        compiler_params=pltpu.CompilerParams(
            dimension_semantics=("arbitrary", "arbitrary"), vmem_limit_bytes=VMEM_LIMIT_BYTES),
        name="mixer_prompt",
    )(x, mods, mods, mods, w_in, w_out, cos, s_lo, s_hi, sinks, gln_g, gln_b, ws, bs_exp, lb, ng_exp,
      ln_g, ln_b)


def _mixer_sample_kernel(
        x_ref, sh_ref, sc_ref, gt_ref, win_ref, wout_ref, cos_ref, sl_ref, sh2_ref, sink_ref,
        gg_ref, gb_ref, w0_ref, bs_ref, lb_ref, ng_ref, lng_ref, lnb_ref, ck_ref, cv_ref, s0_ref,
        y_ref, knew_ref, vnew_ref, gv_ref, s1_ref,
        qt_ref, kn_ref, vn_ref, sn_ref, sc_scr, ot_ref, ft_ref, kt_ref, qc_ref, it_ref, ob_ref, og_ref, oc_ref,
        *, layer, n_s, nb, alpha):
    step = pl.program_id(0)
    rows_all = n_s * H_A

    @pl.when(step == 0)
    def _():
        x = x_ref[...]
        h = (x * (1.0 + sc_ref[...]) + sh_ref[...]).astype(BF16)
        proj = _dot(h, win_ref[...])
        cos, s_lo, s_hi = cos_ref[...], sl_ref[...], sh2_ref[...]
        q = _rope(proj[:, OFF_Q:OFF_K], cos, s_lo, s_hi) * (HEAD_DIM ** -0.5)
        k = _rope(proj[:, OFF_K:OFF_V], cos, s_lo, s_hi)
        v = proj[:, OFF_V:OFF_U]
        knew_ref[...] = k
        vnew_ref[...] = v
        zeros64 = jnp.zeros((n_s, HEAD_DIM), F32)
        for hd in range(H_A):
            qh = q[:, hd * HEAD_DIM:(hd + 1) * HEAD_DIM]
            row = jnp.concatenate([qh, zeros64] if hd < G_A else [zeros64, qh], axis=-1)
            qt_ref[pl.ds(hd, n_s, stride=H_A), :] = row
            kn_ref[pl.ds(hd, n_s, stride=H_A), :] = k
            vn_ref[pl.ds(hd, n_s, stride=H_A), :] = v
        s_new = jnp.sum(qt_ref[...] * kn_ref[...], axis=-1, keepdims=True)
        sn_ref[...] = jnp.broadcast_to(s_new, (rows_all, LANES))

        vn = _layer_norm(proj[:, OFF_GV:OFF_CQ], gg_ref[...], gb_ref[...])
        gv_ref[...] = vn
        ob_ref[...] = proj[:, OFF_U:OFF_GV] * (vn * w0_ref[...] + bs_ref[0:1, :])

        lb = _hgrn_lower_bound(lb_ref, layer)
        f = lb + (1.0 - lb) * jax.nn.sigmoid(proj[:, OFF_CF:OFF_CI])
        ft_ref[...] = f.T
        kt_ref[...] = (1.0 - f).T
        qc_ref[...] = proj[:, OFF_CQ:OFF_CF].T
        it_ref[...] = proj[:, OFF_CI:OFF_CG].T
        og_ref[...] = ng_ref[...] * _silu(proj[:, OFF_CG:])

    base = pl.multiple_of(step * (nb * H_A), nb * H_A)
    for b in range(nb):
        qrows = qt_ref[pl.ds(base + b * H_A, H_A), :].astype(BF16)
        sc_scr[b * H_A:(b + 1) * H_A, :] = _dot(qrows, ck_ref[b].astype(BF16))
    lane = lax.broadcasted_iota(jnp.int32, (nb * H_A, WINDOW), 1)
    s = jnp.where(lane == 0, NEG_BIG, sc_scr[...])
    s_new = sn_ref[pl.ds(base, nb * H_A), 0:1]
    hrow = lax.broadcasted_iota(jnp.int32, (nb * H_A, 1), 0) & (H_A - 1)
    sink = jnp.zeros((nb * H_A, 1), F32)
    for hd in range(H_A):
        sink = jnp.where(hrow == hd, sink_ref[layer, hd], sink)
    mx = jnp.maximum(jnp.maximum(jnp.max(s, axis=-1, keepdims=True), s_new), sink)
    p = jnp.exp(s - mx)
    p_new = jnp.exp(s_new - mx)
    den = jnp.sum(p, axis=-1, keepdims=True) + p_new + jnp.exp(sink - mx)
    sc_scr[...] = p
    for b in range(nb):
        prow = sc_scr[b * H_A:(b + 1) * H_A, :].astype(BF16)
        ot_ref[pl.ds(base + b * H_A, H_A), :] = _dot_t(prow, cv_ref[b].astype(BF16))
    rows = pl.ds(base, nb * H_A)
    ot_ref[rows, :] = (ot_ref[rows, :] + p_new * vn_ref[rows, :]) * (1.0 / den)

    hrow0 = pl.multiple_of(step * DK_C, DK_C)
    i_h = it_ref[pl.ds(hrow0, DV_C), :]
    acc = jnp.zeros((DV_C, n_s), F32)
    for kk in range(DK_C):
        f_k = ft_ref[pl.ds(hrow0 + kk, 1), :]
        k_k = kt_ref[pl.ds(hrow0 + kk, 1), :]
        q_k = qc_ref[pl.ds(hrow0 + kk, 1), :]
        s1 = f_k * s0_ref[kk] + k_k * i_h
        s1_ref[kk] = s1
        acc = acc + q_k * s1
    oc_ref[pl.ds(hrow0, DV_C), :] = acc

    @pl.when(step == pl.num_programs(0) - 1)
    def _():
        o_a = []
        for hd in range(H_A):
            r = ot_ref[pl.ds(hd, n_s, stride=H_A), :]
            o_a.append(r[:, :HEAD_DIM] if hd < G_A else r[:, HEAD_DIM:])
        o_c = oc_ref[...].T
        ms = _head_sum(o_c * o_c, _head_block_ones()) * (1.0 / DV_C)
        o_c = o_c * lax.rsqrt(ms + RMS_EPS) * og_ref[...]
        mix = _dot(jnp.concatenate(o_a + [ob_ref[...], o_c], axis=-1).astype(BF16), wout_ref[...])
        z = alpha * x_ref[...] + (1.0 + gt_ref[...]) * mix
        y_ref[...] = _layer_norm(z, lng_ref[...], lnb_ref[...])


def _mixer_sample(x, mods, layer, w_in, w_out, tables, sinks, gln_g, gln_b, w0_exp, bs_exp, lb, ng_exp,
                  ln_g, ln_b, cache_k, cache_v, state, *, alpha):
    n_s, d = x.shape
    assert n_s % (H_C * SUBLANES) == 0 and n_s % LANES == 0
    nb = n_s // H_C
    depth = w_in.shape[0]
    cos, s_lo, s_hi = tables
    mod_spec = lambda j: pl.BlockSpec((None, n_s, d), lambda i: (layer, 0, j))
    tab_spec = pl.BlockSpec((1, LANES), lambda i: (0, 0))
    lay3 = lambda shape: pl.BlockSpec((None,) + shape, lambda i: (layer,) + (0,) * len(shape))
    rows = pl.BlockSpec((n_s, d), lambda i: (0, 0))
    per_head = pltpu.VMEM((D_C, n_s), F32)
    per_row = pltpu.VMEM((n_s * H_A, LANES), F32)
    return pl.pallas_call(
        functools.partial(_mixer_sample_kernel, layer=layer, n_s=n_s, nb=nb, alpha=alpha),
        grid=(H_C,),
        in_specs=[
            rows,
            mod_spec(3), mod_spec(4), mod_spec(5),
            lay3((d, D_IN)), lay3((d, d)),
            tab_spec, tab_spec, tab_spec,
            pl.BlockSpec(memory_space=pltpu.SMEM),
            lay3((1, D_B)), lay3((1, D_B)), lay3((1, D_B)), lay3((CHUNK_B, D_B)),
            pl.BlockSpec((depth, D_C), lambda i: (0, 0)),
            lay3((1, D_C)),
            pl.BlockSpec((None, 1, d), lambda i: (layer * 3 + 1, 0, 0)),
            pl.BlockSpec((None, 1, d), lambda i: (layer * 3 + 1, 0, 0)),
            pl.BlockSpec((None, nb, D_KV, WINDOW), lambda i: (layer, i, 0, 0)),
            pl.BlockSpec((None, nb, D_KV, WINDOW), lambda i: (layer, i, 0, 0)),
            pl.BlockSpec((None, None, DK_C, DV_C, n_s), lambda i: (layer, i, 0, 0, 0)),
        ],
        out_specs=[
            rows,
            pl.BlockSpec((n_s, D_KV), lambda i: (0, 0)),
            pl.BlockSpec((n_s, D_KV), lambda i: (0, 0)),
            pl.BlockSpec((n_s, D_B), lambda i: (0, 0)),
            pl.BlockSpec((None, DK_C, DV_C, n_s), lambda i: (i, 0, 0, 0)),
        ],
        out_shape=[
            jax.ShapeDtypeStruct((n_s, d), F32),
            jax.ShapeDtypeStruct((n_s, D_KV), F32),
            jax.ShapeDtypeStruct((n_s, D_KV), F32),
            jax.ShapeDtypeStruct((n_s, D_B), F32),
            jax.ShapeDtypeStruct((H_C, DK_C, DV_C, n_s), F32),
        ],
        scratch_shapes=[
            per_row,
            per_row,
            per_row,
            per_row,
            pltpu.VMEM((nb * H_A, WINDOW), F32),
            per_row,
            per_head, per_head, per_head, per_head,
            pltpu.VMEM((n_s, D_B), F32),
            pltpu.VMEM((n_s, D_C), F32),
            per_head,
        ],
        compiler_params=pltpu.CompilerParams(
            dimension_semantics=("arbitrary",), vmem_limit_bytes=VMEM_LIMIT_BYTES),
        name="mixer_sample",
    )(x, mods, mods, mods, w_in, w_out, cos, s_lo, s_hi, sinks, gln_g, gln_b, w0_exp, bs_exp, lb, ng_exp,
      ln_g, ln_b, cache_k, cache_v, state)


def _rope_tables(positions):
    half = ROT_DIM // 2
    f32 = np.float32
    inv = (ROPE_THETA ** (-np.arange(half, dtype=f32) * 2.0 / ROT_DIM)).astype(f32)
    ang = (positions.astype(f32)[:, None] * inv[None, :]).astype(f32)
    cos, sin = np.cos(ang).astype(f32), np.sin(ang).astype(f32)
    n = positions.shape[0]
    rest = HEAD_DIM - ROT_DIM
    cos_h = np.concatenate([cos, cos, np.ones((n, rest), f32)], axis=-1)
    lo_h = np.concatenate([-sin, np.zeros((n, half + rest), f32)], axis=-1)
    hi_h = np.concatenate([np.zeros((n, half), f32), sin, np.zeros((n, rest), f32)], axis=-1)
    two = lambda a: jnp.asarray(np.concatenate([a, a], axis=-1))
    return two(cos_h), two(lo_h), two(hi_h)


def kernel(x_prompt, x_sample, cache_k, cache_v, state_hgrn, c_prompt, c_sample, w_in, w_out, attn_sinks,
           gmlp_ln_g, gmlp_ln_b, gmlp_ws, gmlp_bs, hgrn_lb, hgrn_norm_g, ffn1_in, ffn1_out, ffn2_in,
           ffn2_out, ada_w, ada_b, ln_g, ln_b):
    batch, seq, d = x_prompt.shape
    n_s, dec_seq, _ = x_sample.shape
    depth = w_in.shape[0]
    assert d == D_MODEL and dec_seq == 1 and batch <= MOD_ROWS
    assert cache_k.shape[2] == WINDOW
    alpha = (2 * depth) ** 0.25

    w_in_b, w_out_b = w_in.astype(BF16), w_out.astype(BF16)
    f1i, f1o, f2i, f2o = (a.astype(BF16) for a in (ffn1_in, ffn1_out, ffn2_in, ffn2_out))
    ln_g3 = ln_g.reshape(depth * 3, 1, d)
    ln_b3 = ln_b.reshape(depth * 3, 1, d)
    gln_g = gmlp_ln_g.reshape(depth, 1, D_B)
    gln_b = gmlp_ln_b.reshape(depth, 1, D_B)
    bs_exp = jnp.repeat(jnp.swapaxes(gmlp_bs, 1, 2), HEAD_DIM, axis=2)
    w0_exp = jnp.repeat(gmlp_ws[:, :, 0, 0], HEAD_DIM, axis=1).reshape(depth, 1, D_B)
    ng_exp = jnp.tile(hgrn_norm_g, (1, H_C)).reshape(depth, 1, D_C)
    tab_p = _rope_tables(np.arange(seq))
    tab_s = _rope_tables(PAST_LEN + np.arange(dec_seq))

    c_all = jnp.concatenate([c_prompt, jnp.zeros((MOD_ROWS - batch, d), F32), c_sample], axis=0)
    mod_p, mod_s = _adaln(c_all, ada_w, ada_b)

    ck = jnp.transpose(cache_k, (0, 1, 3, 4, 2)).reshape(depth, n_s, D_KV, WINDOW)
    cv = jnp.transpose(cache_v, (0, 1, 3, 4, 2)).reshape(depth, n_s, D_KV, WINDOW)
    s0 = jnp.transpose(state_hgrn, (0, 2, 3, 4, 1))

    xp = x_prompt.reshape(batch * seq, d)
    xs = x_sample.reshape(n_s, d)
    ffn_p = functools.partial(_ffn, per_row=False, rows_per_batch=seq, alpha=alpha)
    ffn_s = functools.partial(_ffn, per_row=True, rows_per_batch=n_s, alpha=alpha)
    kw, vw, st_p, kn, vnw, gvs, st_s = [], [], [], [], [], [], []
    for l in range(depth):
        xp = ffn_p(xp, mod_p, l, 0, f1i, f1o, ln_g3, ln_b3)
        xp, k_l, v_l, s_l = _mixer_prompt(
            xp, mod_p, l, w_in_b, w_out_b, tab_p, attn_sinks, gln_g, gln_b, gmlp_ws, bs_exp, hgrn_lb,
            ng_exp, ln_g3, ln_b3, batch=batch, seq=seq, alpha=alpha)
        xp = ffn_p(xp, mod_p, l, 2, f2i, f2o, ln_g3, ln_b3)
        kw.append(k_l), vw.append(v_l), st_p.append(s_l)

        xs = ffn_s(xs, mod_s, l, 0, f1i, f1o, ln_g3, ln_b3)
        xs, k_l, v_l, g_l, s_l = _mixer_sample(
            xs, mod_s, l, w_in_b, w_out_b, tab_s, attn_sinks, gln_g, gln_b, w0_exp, bs_exp, hgrn_lb,
            ng_exp, ln_g3, ln_b3, ck, cv, s0, alpha=alpha)
        xs = ffn_s(xs, mod_s, l, 2, f2i, f2o, ln_g3, ln_b3)
        kn.append(k_l), vnw.append(v_l), gvs.append(g_l), st_s.append(s_l)

    stack = lambda parts, shape: jnp.stack(parts, axis=0).reshape((depth,) + shape)
    return (
        xp.reshape(batch, seq, d),
        xs.reshape(n_s, dec_seq, d),
        stack(kw, (batch, WINDOW, KV_A, HEAD_DIM)),
        stack(vw, (batch, WINDOW, KV_A, HEAD_DIM)),
        stack(st_p, (batch, H_C, DK_C, DV_C)),
        stack(kn, (n_s, dec_seq, KV_A, HEAD_DIM)),
        stack(vnw, (n_s, dec_seq, KV_A, HEAD_DIM)),
        stack(gvs, (n_s, dec_seq, D_B)),
        jnp.transpose(jnp.stack(st_s, axis=0), (0, 4, 1, 2, 3)),
    )
```

```python
import functools

import numpy as np
import jax
import jax.numpy as jnp
from jax import lax
from jax.experimental import pallas as pl
from jax.experimental.pallas import tpu as pltpu

F32 = jnp.float32
BF16 = jnp.bfloat16

D_MODEL = 1024
HEAD_DIM = 64
H_A = 8
KV_A = 2
G_A = H_A // KV_A
D_A = H_A * HEAD_DIM
D_KV = KV_A * HEAD_DIM
WINDOW = 128
ROT_DIM = HEAD_DIM // 4
ROPE_THETA = 500000.0
G_B = 4
D_B = G_B * HEAD_DIM
CHUNK_B = 128
H_C = 4
DK_C = 64
DV_C = 64
D_C = H_C * DV_C
D_IN = 2304
D_FF = 2816
N_MOD = 9
LN_EPS = 1e-5
RMS_EPS = 1e-6
NEG_BIG = -1e30
LOG2_E = 1.4426950408889634
PAST_LEN = 16384

OFF_Q = 0
OFF_K = OFF_Q + D_A
OFF_V = OFF_K + D_KV
OFF_U = OFF_V + D_KV
OFF_GV = OFF_U + D_B
OFF_CQ = OFF_GV + D_B
OFF_CF = OFF_CQ + D_C
OFF_CI = OFF_CF + D_C
OFF_CG = OFF_CI + D_C

SUBLANES = 8
LANES = 128
MXU_DIM = 256
VMEM_LIMIT_BYTES = 56 * 1024 * 1024

FFN_ROWS = 1024
FF_CHUNK = MXU_DIM
ADA_COLS = 1024
MIX_ROWS = 256
HG_CHUNK = 16
MOD_ROWS = SUBLANES


def _dot(a, b):
    return jnp.dot(a, b, preferred_element_type=F32)


def _dot_t(a, b):
    return lax.dot_general(a, b, (((1,), (1,)), ((), ())), preferred_element_type=F32)


def _tdot(a, b):
    return lax.dot_general(a, b, (((0,), (0,)), ((), ())), preferred_element_type=F32)


def _layer_norm(z, g, b):
    mu = jnp.mean(z, axis=-1, keepdims=True)
    zc = z - mu
    var = jnp.mean(zc * zc, axis=-1, keepdims=True)
    return zc * lax.rsqrt(var + LN_EPS) * g + b


def _silu(x):
    return x * jax.nn.sigmoid(x)


def _head_block_ones():
    r = lax.broadcasted_iota(jnp.int32, (D_C, D_C), 0) >> 6
    c = lax.broadcasted_iota(jnp.int32, (D_C, D_C), 1) >> 6
    return jnp.where(r == c, 1.0, 0.0).astype(BF16)


def _head_sum(x, ones_bd):
    hi = x.astype(BF16)
    lo = (x - hi.astype(F32)).astype(BF16)
    return _dot(hi, ones_bd) + _dot(lo, ones_bd)


def _rope(x, cos, sin_lo, sin_hi):
    half = ROT_DIM // 2
    out = []
    for c in range(x.shape[-1] // LANES):
        xc = x[:, c * LANES:(c + 1) * LANES]
        out.append(xc * cos + pltpu.roll(xc, LANES - half, 1) * sin_lo + pltpu.roll(xc, half, 1) * sin_hi)
    return out[0] if len(out) == 1 else jnp.concatenate(out, axis=-1)


def _hgrn_lower_bound(lb_ref, layer):
    w = lb_ref[...]
    e = jnp.exp(w - jnp.max(w, axis=0, keepdims=True))
    p = e / jnp.sum(e, axis=0, keepdims=True)
    lb = jnp.zeros((1, D_C), F32)
    for j in range(1, layer + 1):
        lb = lb + p[j:j + 1, :]
    return lb


def _adaln_kernel(c_ref, w_ref, b_ref, op_ref, os_ref):
    s = _silu(c_ref[...]).astype(BF16)
    r = _dot(s, w_ref[...].astype(BF16)) + b_ref[...]
    op_ref[...] = r[:MOD_ROWS]
    os_ref[...] = r[MOD_ROWS:]


def _adaln(c_all, ada_w, ada_b):
    depth, d, n = ada_w.shape
    rows = c_all.shape[0]
    n_s = rows - MOD_ROWS
    return pl.pallas_call(
        _adaln_kernel,
        grid=(depth, n // ADA_COLS),
        in_specs=[
            pl.BlockSpec((rows, d), lambda l, j: (0, 0)),
            pl.BlockSpec((None, d, ADA_COLS), lambda l, j: (l, 0, j)),
            pl.BlockSpec((None, 1, ADA_COLS), lambda l, j: (l, 0, j)),
        ],
        out_specs=[
            pl.BlockSpec((None, MOD_ROWS, ADA_COLS), lambda l, j: (l, 0, j)),
            pl.BlockSpec((None, n_s, ADA_COLS), lambda l, j: (l, 0, j)),
        ],
        out_shape=[
            jax.ShapeDtypeStruct((depth, MOD_ROWS, n), F32),
            jax.ShapeDtypeStruct((depth, n_s, n), F32),
        ],
        compiler_params=pltpu.CompilerParams(
            dimension_semantics=("arbitrary", "arbitrary"), vmem_limit_bytes=VMEM_LIMIT_BYTES),
        name="adaln",
    )(c_all, ada_w, ada_b.reshape(depth, 1, n))


def _mod_rows(ref, per_row, batch):
    return ref[...] if per_row else ref[pl.ds(batch, 1), :]


def _ffn_kernel(x_ref, sh_ref, sc_ref, gt_ref, win_ref, wout_ref, lng_ref, lnb_ref, o_ref, act_ref,
                *, per_row, tiles_per_batch, alpha):
    batch = pl.program_id(0) // tiles_per_batch
    x = x_ref[...]
    shift = _mod_rows(sh_ref, per_row, batch)
    scale = _mod_rows(sc_ref, per_row, batch)
    gate = _mod_rows(gt_ref, per_row, batch)
    h = (x * (1.0 + scale) + shift).astype(BF16)
    for c in range(D_FF // FF_CHUNK):
        lo = c * FF_CHUNK
        a = _dot(h, win_ref[:, lo:lo + FF_CHUNK])
        g = _dot(h, win_ref[:, D_FF + lo:D_FF + lo + FF_CHUNK])
        act_ref[:, lo:lo + FF_CHUNK] = (_silu(g) * a).astype(BF16)
    y = _dot(act_ref[...], wout_ref[...])
    z = alpha * x + (1.0 + gate) * (0.5 * y)
    o_ref[...] = _layer_norm(z, lng_ref[...], lnb_ref[...])


def _ffn(x, mods, layer, sub, w_in, w_out, ln_g, ln_b, *, per_row, rows_per_batch, alpha):
    m, d = x.shape
    tm = min(FFN_ROWS, rows_per_batch)
    assert m % tm == 0 and rows_per_batch % tm == 0
    mod_rows = mods.shape[1]
    if per_row:
        mod_spec = lambda j: pl.BlockSpec((None, tm, d), lambda i: (layer, i, j))
    else:
        mod_spec = lambda j: pl.BlockSpec((None, mod_rows, d), lambda i: (layer, 0, j))
    const = dict(pipeline_mode=pl.Buffered(1))
    return pl.pallas_call(
        functools.partial(_ffn_kernel, per_row=per_row, tiles_per_batch=rows_per_batch // tm, alpha=alpha),
        grid=(m // tm,),
        in_specs=[
            pl.BlockSpec((tm, d), lambda i: (i, 0)),
            mod_spec(3 * sub), mod_spec(3 * sub + 1), mod_spec(3 * sub + 2),
            pl.BlockSpec((None, d, 2 * D_FF), lambda i: (layer, 0, 0), **const),
            pl.BlockSpec((None, D_FF, d), lambda i: (layer, 0, 0), **const),
            pl.BlockSpec((None, 1, d), lambda i: (layer * 3 + sub, 0, 0)),
            pl.BlockSpec((None, 1, d), lambda i: (layer * 3 + sub, 0, 0)),
        ],
        out_specs=pl.BlockSpec((tm, d), lambda i: (i, 0)),
        out_shape=jax.ShapeDtypeStruct((m, d), F32),
        scratch_shapes=[pltpu.VMEM((tm, D_FF), BF16)],
        compiler_params=pltpu.CompilerParams(
            dimension_semantics=("arbitrary",), vmem_limit_bytes=VMEM_LIMIT_BYTES),
        name="ffn_rows" if per_row else "ffn_batch",
    )(x, mods, mods, mods, w_in, w_out, ln_g, ln_b)


def _attention_block(q_blk, keys, vals, mask, fill):
    outs = []
    for j in range(KV_A):
        kj = keys[:, j * HEAD_DIM:(j + 1) * HEAD_DIM].astype(BF16)
        vj = vals[:, j * HEAD_DIM:(j + 1) * HEAD_DIM].astype(BF16)
        qj = jnp.concatenate(
            [q_blk[:, (j * G_A + g) * HEAD_DIM:(j * G_A + g + 1) * HEAD_DIM] for g in range(G_A)],
            axis=0).astype(BF16)
        s = jnp.where(mask, _dot_t(qj, kj), fill[j])
        p = jnp.exp2(s - jnp.max(s, axis=-1, keepdims=True))
        den = jnp.sum(p, axis=-1, keepdims=True)
        o = _dot(p.astype(BF16), vj) * (1.0 / den)
        outs.extend(o[g * WINDOW:(g + 1) * WINDOW] for g in range(G_A))
    return jnp.concatenate(outs, axis=-1)


def _mixer_prompt_kernel(
        x_ref, sh_ref, sc_ref, gt_ref, win_ref, wout_ref, cos_ref, sl_ref, sh2_ref, sink_ref,
        gg_ref, gb_ref, ws_ref, bs_ref, lb_ref, ng_ref, lng_ref, lnb_ref,
        y_ref, kwin_ref, vwin_ref, st_out_ref,
        kprev_ref, vprev_ref, st_ref,
        *, layer, tq, alpha):
    b_idx = pl.program_id(0)
    t_idx = pl.program_id(1)
    n_t = pl.num_programs(1)

    @pl.when(t_idx == 0)
    def _():
        kprev_ref[...] = jnp.zeros_like(kprev_ref)
        vprev_ref[...] = jnp.zeros_like(vprev_ref)
        st_ref[...] = jnp.zeros_like(st_ref)

    x = x_ref[...]
    shift = sh_ref[pl.ds(b_idx, 1), :]
    scale = sc_ref[pl.ds(b_idx, 1), :]
    gate = gt_ref[pl.ds(b_idx, 1), :]
    h = (x * (1.0 + scale) + shift).astype(BF16)
    proj = _dot(h, win_ref[...])

    cos, s_lo, s_hi = cos_ref[...], sl_ref[...], sh2_ref[...]
    q = _rope(proj[:, OFF_Q:OFF_K], cos, s_lo, s_hi) * (HEAD_DIM ** -0.5 * LOG2_E)
    k = _rope(proj[:, OFF_K:OFF_V], cos, s_lo, s_hi)
    v = proj[:, OFF_V:OFF_U]

    qrow = lax.broadcasted_iota(jnp.int32, (G_A * WINDOW, 2 * WINDOW), 0)
    qi = qrow & (WINDOW - 1)
    kj = lax.broadcasted_iota(jnp.int32, (G_A * WINDOW, 2 * WINDOW), 1)
    dist = kj - qi
    band = (dist - 1).astype(jnp.uint32) < jnp.uint32(WINDOW)
    fill = []
    for j in range(KV_A):
        sink = jnp.zeros((G_A * WINDOW, 2 * WINDOW), F32)
        for g in range(G_A):
            sink = jnp.where((qrow >> 7) == g, sink_ref[layer, j * G_A + g] * LOG2_E, sink)
        fill.append(jnp.where(kj == 0, sink, NEG_BIG))
    first_row = lax.broadcasted_iota(jnp.int32, (WINDOW, D_KV), 0) == 0
    o_a = []
    for blk in range(tq // WINDOW):
        r0 = blk * WINDOW
        if blk == 0:
            k_prev, v_prev = kprev_ref[...], vprev_ref[...]
            lower = jnp.where(t_idx == 0, WINDOW, 0)
            mask = jnp.logical_and(band, kj >= lower)
        else:
            k_prev, v_prev = k[r0 - WINDOW:r0], v[r0 - WINDOW:r0]
            mask = band
        keys = jnp.concatenate([k_prev, k[r0:r0 + WINDOW]], axis=0)
        vals = jnp.concatenate([jnp.where(first_row, 0.0, v_prev), v[r0:r0 + WINDOW]], axis=0)
        o_a.append(_attention_block(q[r0:r0 + WINDOW], keys, vals, mask, fill))
    o_a = jnp.concatenate(o_a, axis=0)
    kprev_ref[...] = k[tq - WINDOW:]
    vprev_ref[...] = v[tq - WINDOW:]
    kwin_ref[...] = k[tq - WINDOW:]
    vwin_ref[...] = v[tq - WINDOW:]

    vn = _layer_norm(proj[:, OFF_GV:OFF_CQ], gg_ref[...], gb_ref[...])
    u = proj[:, OFF_U:OFF_GV]
    tr = lax.broadcasted_iota(jnp.int32, (CHUNK_B, CHUNK_B), 0)
    tc = lax.broadcasted_iota(jnp.int32, (CHUNK_B, CHUNK_B), 1)
    wm = [jnp.where(tr >= tc, ws_ref[g], 0.0).astype(BF16) for g in range(G_B)]
    o_b = []
    for c in range(tq // CHUNK_B):
        vc = vn[c * CHUNK_B:(c + 1) * CHUNK_B].astype(BF16)
        mixed = jnp.concatenate(
            [_dot(wm[g], vc[:, g * HEAD_DIM:(g + 1) * HEAD_DIM]) for g in range(G_B)], axis=-1)
        o_b.append(u[c * CHUNK_B:(c + 1) * CHUNK_B] * (mixed + bs_ref[...]))
    o_b = jnp.concatenate(o_b, axis=0)

    n_ch = tq // HG_CHUNK
    half = HG_CHUNK // 2
    cq = proj[:, OFF_CQ:OFF_CF]
    ci = proj[:, OFF_CI:OFF_CG]
    lb = _hgrn_lower_bound(lb_ref, layer)
    f = lb + (1.0 - lb) * jax.nn.sigmoid(proj[:, OFF_CF:OFF_CI])
    lf = jnp.log(f)
    kc = 1.0 - f

    rt = lax.broadcasted_iota(jnp.int32, (tq, tq), 0)
    ct = lax.broadcasted_iota(jnp.int32, (tq, tq), 1)
    ltri = jnp.where(ct <= rt, jnp.where(ct >= (rt & -HG_CHUNK), 1.0, 0.0), 0.0).astype(BF16)
    lf_hi = lf.astype(BF16)
    lf_r = lf - lf_hi.astype(F32)
    lf_mid = lf_r.astype(BF16)
    lf_lo = (lf_r - lf_mid.astype(F32)).astype(BF16)
    bsum = (_dot(ltri, lf_hi) + _dot(ltri, lf_mid) + _dot(ltri, lf_lo)) * LOG2_E

    def chunks(a):
        return a.reshape(n_ch, HG_CHUNK, D_C)

    b3, cq3, kc3, ci3 = chunks(bsum), chunks(cq), chunks(kc), chunks(ci)
    b_end = b3[:, HG_CHUNK - 1:HG_CHUNK, :]
    eb = jnp.exp2(bsum)
    qb = cq * eb
    kb = kc * jnp.exp2(b_end - b3).reshape(tq, D_C)

    ones_bd = _head_block_ones()

    def pair_products(s, lo):
        e = cq3[:, lo:] * kc3[:, s:s + 1, :] * jnp.exp2(b3[:, lo:] - b3[:, s:s + 1, :])
        pos = lax.broadcasted_iota(jnp.int32, (n_ch, HG_CHUNK - lo, D_C), 1) + lo
        return jnp.where(pos >= s, e, 0.0)

    def head_sums(slab):
        return chunks(_dot(slab.reshape(tq, D_C).astype(BF16), ones_bd))

    o_all = jnp.zeros((n_ch, HG_CHUNK, D_C), F32)
    for s in range(half):
        o_all = o_all + head_sums(pair_products(s, 0)) * ci3[:, s:s + 1, :]
    o_up = jnp.zeros((n_ch, half, D_C), F32)
    for s in range(half, HG_CHUNK, 2):
        a3 = head_sums(jnp.concatenate([pair_products(s, half), pair_products(s + 1, half)], axis=1))
        o_up = o_up + a3[:, :half] * ci3[:, s:s + 1, :] + a3[:, half:] * ci3[:, s + 1:s + 2, :]
    o_c = (o_all + jnp.concatenate([jnp.zeros((n_ch, half, D_C), F32), o_up], axis=1)).reshape(tq, D_C)

    lane_head = lax.broadcasted_iota(jnp.int32, (tq, D_C), 1) >> 6
    im_h = [jnp.where(lane_head == hh, ci, 0.0).astype(BF16) for hh in range(H_C)]
    qb_h = [qb[:, hh * DK_C:(hh + 1) * DK_C].astype(BF16) for hh in range(H_C)]
    kb_h = [kb[:, hh * DK_C:(hh + 1) * DK_C].astype(BF16) for hh in range(H_C)]
    eb_t = eb.T
    lane_head_c = lax.broadcasted_iota(jnp.int32, (HG_CHUNK, D_C), 1) >> 6
    st = st_ref[...]
    o_inter = []
    for n in range(n_ch):
        r0, r1 = n * HG_CHUNK, (n + 1) * HG_CHUNK
        read = _dot(jnp.concatenate([a[r0:r1] for a in qb_h], axis=0), st.astype(BF16))
        o_n = jnp.zeros((HG_CHUNK, D_C), F32)
        for hh in range(H_C):
            o_n = o_n + jnp.where(lane_head_c == hh, read[hh * HG_CHUNK:(hh + 1) * HG_CHUNK], 0.0)
        o_inter.append(o_n)
        upd = _tdot(jnp.concatenate([a[r0:r1] for a in kb_h], axis=0),
                    jnp.concatenate([a[r0:r1] for a in im_h], axis=0))
        col = eb_t[:, r1 - 1:r1]
        decay = jnp.concatenate(
            [jnp.broadcast_to(col[hh * DK_C:(hh + 1) * DK_C], (DK_C, DV_C)) for hh in range(H_C)], axis=1)
        st = decay * st + upd
    st_ref[...] = st
    o_c = o_c + jnp.concatenate(o_inter, axis=0)

    @pl.when(t_idx == n_t - 1)
    def _():
        for hh in range(H_C):
            st_out_ref[hh * DK_C:(hh + 1) * DK_C, :] = st[:, hh * DV_C:(hh + 1) * DV_C]

    ms = _head_sum(o_c * o_c, ones_bd) * (1.0 / DV_C)
    o_c = o_c * lax.rsqrt(ms + RMS_EPS) * ng_ref[...] * _silu(proj[:, OFF_CG:])

    mix = _dot(jnp.concatenate([o_a, o_b, o_c], axis=-1).astype(BF16), wout_ref[...])
    z = alpha * x + (1.0 + gate) * mix
    y_ref[...] = _layer_norm(z, lng_ref[...], lnb_ref[...])


def _mixer_prompt(x, mods, layer, w_in, w_out, tables, sinks, gln_g, gln_b, ws, bs_exp, lb, ng_exp,
                  ln_g, ln_b, *, batch, seq, alpha):
    m, d = x.shape
    tq = min(MIX_ROWS, seq)
    assert seq % tq == 0 and tq % WINDOW == 0
    n_t = seq // tq
    depth = w_in.shape[0]
    cos, s_lo, s_hi = tables
    mod_spec = lambda j: pl.BlockSpec((None, MOD_ROWS, d), lambda b, t: (layer, 0, j))
    tab_spec = pl.BlockSpec((tq, LANES), lambda b, t: (t, 0))
    lay3 = lambda shape: pl.BlockSpec((None,) + shape, lambda b, t: (layer,) + (0,) * len(shape))
    return pl.pallas_call(
        functools.partial(_mixer_prompt_kernel, layer=layer, tq=tq, alpha=alpha),
        grid=(batch, n_t),
        in_specs=[
            pl.BlockSpec((tq, d), lambda b, t: (b * n_t + t, 0)),
            mod_spec(3), mod_spec(4), mod_spec(5),
            lay3((d, D_IN)), lay3((d, d)),
            tab_spec, tab_spec, tab_spec,
            pl.BlockSpec(memory_space=pltpu.SMEM),
            lay3((1, D_B)), lay3((1, D_B)),
            lay3((G_B, CHUNK_B, CHUNK_B)), lay3((CHUNK_B, D_B)),
            pl.BlockSpec((depth, D_C), lambda b, t: (0, 0)),
            lay3((1, D_C)),
            pl.BlockSpec((None, 1, d), lambda b, t: (layer * 3 + 1, 0, 0)),
            pl.BlockSpec((None, 1, d), lambda b, t: (layer * 3 + 1, 0, 0)),
        ],
        out_specs=[
            pl.BlockSpec((tq, d), lambda b, t: (b * n_t + t, 0)),
            pl.BlockSpec((None, WINDOW, D_KV), lambda b, t: (b, 0, 0)),
            pl.BlockSpec((None, WINDOW, D_KV), lambda b, t: (b, 0, 0)),
            pl.BlockSpec((None, D_C, DV_C), lambda b, t: (b, 0, 0)),
        ],
        out_shape=[
            jax.ShapeDtypeStruct((m, d), F32),
            jax.ShapeDtypeStruct((batch, WINDOW, D_KV), F32),
            jax.ShapeDtypeStruct((batch, WINDOW, D_KV), F32),
            jax.ShapeDtypeStruct((batch, D_C, DV_C), F32),
        ],
        scratch_shapes=[
            pltpu.VMEM((WINDOW, D_KV), F32),
            pltpu.VMEM((WINDOW, D_KV), F32),
            pltpu.VMEM((DK_C, D_C), F32),
        ],
        compiler_params=pltpu.CompilerParams(
            dimension_semantics=("arbitrary", "arbitrary"), vmem_limit_bytes=VMEM_LIMIT_BYTES),
        name="mixer_prompt",
    )(x, mods, mods, mods, w_in, w_out, cos, s_lo, s_hi, sinks, gln_g, gln_b, ws, bs_exp, lb, ng_exp,
      ln_g, ln_b)


def _mixer_sample_kernel(
        x_ref, sh_ref, sc_ref, gt_ref, win_ref, wout_ref, cos_ref, sl_ref, sh2_ref, sink_ref,
        gg_ref, gb_ref, w0_ref, bs_ref, lb_ref, ng_ref, lng_ref, lnb_ref, ck_ref, cv_ref, s0_ref,
        y_ref, knew_ref, vnew_ref, gv_ref, s1_ref,
        qt_ref, kn_ref, vn_ref, sn_ref, sc_scr, ot_ref, ft_ref, kt_ref, qc_ref, it_ref, ob_ref, og_ref, oc_ref,
        *, layer, n_s, nb, alpha):
    step = pl.program_id(0)
    rows_all = n_s * H_A

    @pl.when(step == 0)
    def _():
        x = x_ref[...]
        h = (x * (1.0 + sc_ref[...]) + sh_ref[...]).astype(BF16)
        proj = _dot(h, win_ref[...])
        cos, s_lo, s_hi = cos_ref[...], sl_ref[...], sh2_ref[...]
        q = _rope(proj[:, OFF_Q:OFF_K], cos, s_lo, s_hi) * (HEAD_DIM ** -0.5)
        k = _rope(proj[:, OFF_K:OFF_V], cos, s_lo, s_hi)
        v = proj[:, OFF_V:OFF_U]
        knew_ref[...] = k
        vnew_ref[...] = v
        zeros64 = jnp.zeros((n_s, HEAD_DIM), F32)
        for hd in range(H_A):
            qh = q[:, hd * HEAD_DIM:(hd + 1) * HEAD_DIM]
            row = jnp.concatenate([qh, zeros64] if hd < G_A else [zeros64, qh], axis=-1)
            qt_ref[pl.ds(hd, n_s, stride=H_A), :] = row
            kn_ref[pl.ds(hd, n_s, stride=H_A), :] = k
            vn_ref[pl.ds(hd, n_s, stride=H_A), :] = v
        s_new = jnp.sum(qt_ref[...] * kn_ref[...], axis=-1, keepdims=True)
        sn_ref[...] = jnp.broadcast_to(s_new, (rows_all, LANES))

        vn = _layer_norm(proj[:, OFF_GV:OFF_CQ], gg_ref[...], gb_ref[...])
        gv_ref[...] = vn
        ob_ref[...] = proj[:, OFF_U:OFF_GV] * (vn * w0_ref[...] + bs_ref[0:1, :])

        lb = _hgrn_lower_bound(lb_ref, layer)
        f = lb + (1.0 - lb) * jax.nn.sigmoid(proj[:, OFF_CF:OFF_CI])
        ft_ref[...] = f.T
        kt_ref[...] = (1.0 - f).T
        qc_ref[...] = proj[:, OFF_CQ:OFF_CF].T
        it_ref[...] = proj[:, OFF_CI:OFF_CG].T
        og_ref[...] = ng_ref[...] * _silu(proj[:, OFF_CG:])

    base = pl.multiple_of(step * (nb * H_A), nb * H_A)
    for b in range(nb):
        qrows = qt_ref[pl.ds(base + b * H_A, H_A), :].astype(BF16)
        sc_scr[b * H_A:(b + 1) * H_A, :] = _dot(qrows, ck_ref[b].astype(BF16))
    lane = lax.broadcasted_iota(jnp.int32, (nb * H_A, WINDOW), 1)
    s = jnp.where(lane == 0, NEG_BIG, sc_scr[...])
    s_new = sn_ref[pl.ds(base, nb * H_A), 0:1]
    hrow = lax.broadcasted_iota(jnp.int32, (nb * H_A, 1), 0) & (H_A - 1)
    sink = jnp.zeros((nb * H_A, 1), F32)
    for hd in range(H_A):
        sink = jnp.where(hrow == hd, sink_ref[layer, hd], sink)
    mx = jnp.maximum(jnp.maximum(jnp.max(s, axis=-1, keepdims=True), s_new), sink)
    p = jnp.exp(s - mx)
    p_new = jnp.exp(s_new - mx)
    den = jnp.sum(p, axis=-1, keepdims=True) + p_new + jnp.exp(sink - mx)
    sc_scr[...] = p
    for b in range(nb):
        prow = sc_scr[b * H_A:(b + 1) * H_A, :].astype(BF16)
        ot_ref[pl.ds(base + b * H_A, H_A), :] = _dot_t(prow, cv_ref[b].astype(BF16))
    rows = pl.ds(base, nb * H_A)
    ot_ref[rows, :] = (ot_ref[rows, :] + p_new * vn_ref[rows, :]) * (1.0 / den)

    hrow0 = pl.multiple_of(step * DK_C, DK_C)
    i_h = it_ref[pl.ds(hrow0, DV_C), :]
    acc = jnp.zeros((DV_C, n_s), F32)
    for kk in range(DK_C):
        f_k = ft_ref[pl.ds(hrow0 + kk, 1), :]
        k_k = kt_ref[pl.ds(hrow0 + kk, 1), :]
        q_k = qc_ref[pl.ds(hrow0 + kk, 1), :]
        s1 = f_k * s0_ref[kk] + k_k * i_h
        s1_ref[kk] = s1
        acc = acc + q_k * s1
    oc_ref[pl.ds(hrow0, DV_C), :] = acc

    @pl.when(step == pl.num_programs(0) - 1)
    def _():
        o_a = []
        for hd in range(H_A):
            r = ot_ref[pl.ds(hd, n_s, stride=H_A), :]
            o_a.append(r[:, :HEAD_DIM] if hd < G_A else r[:, HEAD_DIM:])
        o_c = oc_ref[...].T
        ms = _head_sum(o_c * o_c, _head_block_ones()) * (1.0 / DV_C)
        o_c = o_c * lax.rsqrt(ms + RMS_EPS) * og_ref[...]
        mix = _dot(jnp.concatenate(o_a + [ob_ref[...], o_c], axis=-1).astype(BF16), wout_ref[...])
        z = alpha * x_ref[...] + (1.0 + gt_ref[...]) * mix
        y_ref[...] = _layer_norm(z, lng_ref[...], lnb_ref[...])


def _mixer_sample(x, mods, layer, w_in, w_out, tables, sinks, gln_g, gln_b, w0_exp, bs_exp, lb, ng_exp,
                  ln_g, ln_b, cache_k, cache_v, state, *, alpha):
    n_s, d = x.shape
    assert n_s % (H_C * SUBLANES) == 0 and n_s % LANES == 0
    nb = n_s // H_C
    depth = w_in.shape[0]
    cos, s_lo, s_hi = tables
    mod_spec = lambda j: pl.BlockSpec((None, n_s, d), lambda i: (layer, 0, j))
    tab_spec = pl.BlockSpec((1, LANES), lambda i: (0, 0))
    lay3 = lambda shape: pl.BlockSpec((None,) + shape, lambda i: (layer,) + (0,) * len(shape))
    rows = pl.BlockSpec((n_s, d), lambda i: (0, 0))
    per_head = pltpu.VMEM((D_C, n_s), F32)
    per_row = pltpu.VMEM((n_s * H_A, LANES), F32)
    return pl.pallas_call(
        functools.partial(_mixer_sample_kernel, layer=layer, n_s=n_s, nb=nb, alpha=alpha),
        grid=(H_C,),
        in_specs=[
            rows,
            mod_spec(3), mod_spec(4), mod_spec(5),
            lay3((d, D_IN)), lay3((d, d)),
            tab_spec, tab_spec, tab_spec,
            pl.BlockSpec(memory_space=pltpu.SMEM),
            lay3((1, D_B)), lay3((1, D_B)), lay3((1, D_B)), lay3((CHUNK_B, D_B)),
            pl.BlockSpec((depth, D_C), lambda i: (0, 0)),
            lay3((1, D_C)),
            pl.BlockSpec((None, 1, d), lambda i: (layer * 3 + 1, 0, 0)),
            pl.BlockSpec((None, 1, d), lambda i: (layer * 3 + 1, 0, 0)),
            pl.BlockSpec((None, nb, D_KV, WINDOW), lambda i: (layer, i, 0, 0)),
            pl.BlockSpec((None, nb, D_KV, WINDOW), lambda i: (layer, i, 0, 0)),
            pl.BlockSpec((None, None, DK_C, DV_C, n_s), lambda i: (layer, i, 0, 0, 0)),
        ],
        out_specs=[
            rows,
            pl.BlockSpec((n_s, D_KV), lambda i: (0, 0)),
            pl.BlockSpec((n_s, D_KV), lambda i: (0, 0)),
            pl.BlockSpec((n_s, D_B), lambda i: (0, 0)),
            pl.BlockSpec((None, DK_C, DV_C, n_s), lambda i: (i, 0, 0, 0)),
        ],
        out_shape=[
            jax.ShapeDtypeStruct((n_s, d), F32),
            jax.ShapeDtypeStruct((n_s, D_KV), F32),
            jax.ShapeDtypeStruct((n_s, D_KV), F32),
            jax.ShapeDtypeStruct((n_s, D_B), F32),
            jax.ShapeDtypeStruct((H_C, DK_C, DV_C, n_s), F32),
        ],
        scratch_shapes=[
            per_row,
            per_row,
            per_row,
            per_row,
            pltpu.VMEM((nb * H_A, WINDOW), F32),
            per_row,
            per_head, per_head, per_head, per_head,
            pltpu.VMEM((n_s, D_B), F32),
            pltpu.VMEM((n_s, D_C), F32),
            per_head,
        ],
        compiler_params=pltpu.CompilerParams(
            dimension_semantics=("arbitrary",), vmem_limit_bytes=VMEM_LIMIT_BYTES),
        name="mixer_sample",
    )(x, mods, mods, mods, w_in, w_out, cos, s_lo, s_hi, sinks, gln_g, gln_b, w0_exp, bs_exp, lb, ng_exp,
      ln_g, ln_b, cache_k, cache_v, state)


def _rope_tables(positions):
    half = ROT_DIM // 2
    f32 = np.float32
    inv = (ROPE_THETA ** (-np.arange(half, dtype=f32) * 2.0 / ROT_DIM)).astype(f32)
    ang = (positions.astype(f32)[:, None] * inv[None, :]).astype(f32)
    cos, sin = np.cos(ang).astype(f32), np.sin(ang).astype(f32)
    n = positions.shape[0]
    rest = HEAD_DIM - ROT_DIM
    cos_h = np.concatenate([cos, cos, np.ones((n, rest), f32)], axis=-1)
    lo_h = np.concatenate([-sin, np.zeros((n, half + rest), f32)], axis=-1)
    hi_h = np.concatenate([np.zeros((n, half), f32), sin, np.zeros((n, rest), f32)], axis=-1)
    two = lambda a: jnp.asarray(np.concatenate([a, a], axis=-1))
    return two(cos_h), two(lo_h), two(hi_h)


def kernel(x_prompt, x_sample, cache_k, cache_v, state_hgrn, c_prompt, c_sample, w_in, w_out, attn_sinks,
           gmlp_ln_g, gmlp_ln_b, gmlp_ws, gmlp_bs, hgrn_lb, hgrn_norm_g, ffn1_in, ffn1_out, ffn2_in,
           ffn2_out, ada_w, ada_b, ln_g, ln_b):
    batch, seq, d = x_prompt.shape
    n_s, dec_seq, _ = x_sample.shape
    depth = w_in.shape[0]
    assert d == D_MODEL and dec_seq == 1 and batch <= MOD_ROWS
    assert cache_k.shape[2] == WINDOW
    alpha = (2 * depth) ** 0.25

    w_in_b, w_out_b = w_in.astype(BF16), w_out.astype(BF16)
    f1i, f1o, f2i, f2o = (a.astype(BF16) for a in (ffn1_in, ffn1_out, ffn2_in, ffn2_out))
    ln_g3 = ln_g.reshape(depth * 3, 1, d)
    ln_b3 = ln_b.reshape(depth * 3, 1, d)
    gln_g = gmlp_ln_g.reshape(depth, 1, D_B)
    gln_b = gmlp_ln_b.reshape(depth, 1, D_B)
    bs_exp = jnp.repeat(jnp.swapaxes(gmlp_bs, 1, 2), HEAD_DIM, axis=2)
    w0_exp = jnp.repeat(gmlp_ws[:, :, 0, 0], HEAD_DIM, axis=1).reshape(depth, 1, D_B)
    ng_exp = jnp.tile(hgrn_norm_g, (1, H_C)).reshape(depth, 1, D_C)
    tab_p = _rope_tables(np.arange(seq))
    tab_s = _rope_tables(PAST_LEN + np.arange(dec_seq))

    c_all = jnp.concatenate([c_prompt, jnp.zeros((MOD_ROWS - batch, d), F32), c_sample], axis=0)
    mod_p, mod_s = _adaln(c_all, ada_w, ada_b)

    ck = jnp.transpose(cache_k, (0, 1, 3, 4, 2)).reshape(depth, n_s, D_KV, WINDOW)
    cv = jnp.transpose(cache_v, (0, 1, 3, 4, 2)).reshape(depth, n_s, D_KV, WINDOW)
    s0 = jnp.transpose(state_hgrn, (0, 2, 3, 4, 1))

    xp = x_prompt.reshape(batch * seq, d)
    xs = x_sample.reshape(n_s, d)
    ffn_p = functools.partial(_ffn, per_row=False, rows_per_batch=seq, alpha=alpha)
    ffn_s = functools.partial(_ffn, per_row=True, rows_per_batch=n_s, alpha=alpha)
    kw, vw, st_p, kn, vnw, gvs, st_s = [], [], [], [], [], [], []
    for l in range(depth):
        xp = ffn_p(xp, mod_p, l, 0, f1i, f1o, ln_g3, ln_b3)
        xp, k_l, v_l, s_l = _mixer_prompt(
            xp, mod_p, l, w_in_b, w_out_b, tab_p, attn_sinks, gln_g, gln_b, gmlp_ws, bs_exp, hgrn_lb,
            ng_exp, ln_g3, ln_b3, batch=batch, seq=seq, alpha=alpha)
        xp = ffn_p(xp, mod_p, l, 2, f2i, f2o, ln_g3, ln_b3)
        kw.append(k_l), vw.append(v_l), st_p.append(s_l)

        xs = ffn_s(xs, mod_s, l, 0, f1i, f1o, ln_g3, ln_b3)
        xs, k_l, v_l, g_l, s_l = _mixer_sample(
            xs, mod_s, l, w_in_b, w_out_b, tab_s, attn_sinks, gln_g, gln_b, w0_exp, bs_exp, hgrn_lb,
            ng_exp, ln_g3, ln_b3, ck, cv, s0, alpha=alpha)
        xs = ffn_s(xs, mod_s, l, 2, f2i, f2o, ln_g3, ln_b3)
        kn.append(k_l), vnw.append(v_l), gvs.append(g_l), st_s.append(s_l)

    stack = lambda parts, shape: jnp.stack(parts, axis=0).reshape((depth,) + shape)
    return (
        xp.reshape(batch, seq, d),
        xs.reshape(n_s, dec_seq, d),
        stack(kw, (batch, WINDOW, KV_A, HEAD_DIM)),
        stack(vw, (batch, WINDOW, KV_A, HEAD_DIM)),
        stack(st_p, (batch, H_C, DK_C, DV_C)),
        stack(kn, (n_s, dec_seq, KV_A, HEAD_DIM)),
        stack(vnw, (n_s, dec_seq, KV_A, HEAD_DIM)),
        stack(gvs, (n_s, dec_seq, D_B)),
        jnp.transpose(jnp.stack(st_s, axis=0), (0, 4, 1, 2, 3)),
    )
```

```python
import functools

import numpy as np
import jax
import jax.numpy as jnp
from jax import lax
from jax.experimental import pallas as pl
from jax.experimental.pallas import tpu as pltpu

F32 = jnp.float32
BF16 = jnp.bfloat16

D_MODEL = 1024
HEAD_DIM = 64
H_A = 8
KV_A = 2
G_A = H_A // KV_A
D_A = H_A * HEAD_DIM
D_KV = KV_A * HEAD_DIM
WINDOW = 128
ROT_DIM = HEAD_DIM // 4
ROPE_THETA = 500000.0
G_B = 4
D_B = G_B * HEAD_DIM
CHUNK_B = 128
H_C = 4
DK_C = 64
DV_C = 64
D_C = H_C * DV_C
D_IN = 2304
D_FF = 2816
N_MOD = 9
LN_EPS = 1e-5
RMS_EPS = 1e-6
NEG_BIG = -1e30
LOG2_E = 1.4426950408889634
PAST_LEN = 16384

OFF_Q = 0
OFF_K = OFF_Q + D_A
OFF_V = OFF_K + D_KV
OFF_U = OFF_V + D_KV
OFF_GV = OFF_U + D_B
OFF_CQ = OFF_GV + D_B
OFF_CF = OFF_CQ + D_C
OFF_CI = OFF_CF + D_C
OFF_CG = OFF_CI + D_C

SUBLANES = 8
LANES = 128
MXU_DIM = 256
VMEM_LIMIT_BYTES = 56 * 1024 * 1024

FFN_ROWS = 1024
FF_CHUNK = MXU_DIM
FF_CONV_STEPS = 11
FF_CONV_COLS = 2 * D_FF // FF_CONV_STEPS
FF_CONV_ROWS = D_FF // FF_CONV_STEPS
ADA_COLS = 1024
MIX_ROWS = 256
HG_CHUNK = 16
MOD_ROWS = SUBLANES


def _dot(a, b):
    return jnp.dot(a, b, preferred_element_type=F32)


def _dot_t(a, b):
    return lax.dot_general(a, b, (((1,), (1,)), ((), ())), preferred_element_type=F32)


def _tdot(a, b):
    return lax.dot_general(a, b, (((0,), (0,)), ((), ())), preferred_element_type=F32)


def _layer_norm(z, g, b):
    mu = jnp.mean(z, axis=-1, keepdims=True)
    zc = z - mu
    var = jnp.mean(zc * zc, axis=-1, keepdims=True)
    return zc * lax.rsqrt(var + LN_EPS) * g + b


def _silu(x):
    return x * jax.nn.sigmoid(x)


def _head_block_ones():
    r = lax.broadcasted_iota(jnp.int32, (D_C, D_C), 0) >> 6
    c = lax.broadcasted_iota(jnp.int32, (D_C, D_C), 1) >> 6
    return jnp.where(r == c, 1.0, 0.0).astype(BF16)


def _head_sum(x, ones_bd):
    hi = x.astype(BF16)
    lo = (x - hi.astype(F32)).astype(BF16)
    return _dot(hi, ones_bd) + _dot(lo, ones_bd)


def _rope(x, cos, sin_lo, sin_hi):
    half = ROT_DIM // 2
    out = []
    for c in range(x.shape[-1] // LANES):
        xc = x[:, c * LANES:(c + 1) * LANES]
        out.append(xc * cos + pltpu.roll(xc, LANES - half, 1) * sin_lo + pltpu.roll(xc, half, 1) * sin_hi)
    return out[0] if len(out) == 1 else jnp.concatenate(out, axis=-1)


def _hgrn_lower_bound(lb_ref, layer):
    w = lb_ref[...]
    e = jnp.exp(w - jnp.max(w, axis=0, keepdims=True))
    p = e / jnp.sum(e, axis=0, keepdims=True)
    lb = jnp.zeros((1, D_C), F32)
    for j in range(1, layer + 1):
        lb = lb + p[j:j + 1, :]
    return lb


def _adaln_kernel(c_ref, w_ref, b_ref, op_ref, os_ref):
    s = _silu(c_ref[...]).astype(BF16)
    r = _dot(s, w_ref[...].astype(BF16)) + b_ref[...]
    op_ref[...] = r[:MOD_ROWS]
    os_ref[...] = r[MOD_ROWS:]


def _adaln(c_all, ada_w, ada_b):
    depth, d, n = ada_w.shape
    rows = c_all.shape[0]
    n_s = rows - MOD_ROWS
    return pl.pallas_call(
        _adaln_kernel,
        grid=(depth, n // ADA_COLS),
        in_specs=[
            pl.BlockSpec((rows, d), lambda l, j: (0, 0)),
            pl.BlockSpec((None, d, ADA_COLS), lambda l, j: (l, 0, j)),
            pl.BlockSpec((None, 1, ADA_COLS), lambda l, j: (l, 0, j)),
        ],
        out_specs=[
            pl.BlockSpec((None, MOD_ROWS, ADA_COLS), lambda l, j: (l, 0, j)),
            pl.BlockSpec((None, n_s, ADA_COLS), lambda l, j: (l, 0, j)),
        ],
        out_shape=[
            jax.ShapeDtypeStruct((depth, MOD_ROWS, n), F32),
            jax.ShapeDtypeStruct((depth, n_s, n), F32),
        ],
        compiler_params=pltpu.CompilerParams(
            dimension_semantics=("arbitrary", "arbitrary"), vmem_limit_bytes=VMEM_LIMIT_BYTES),
        name="adaln",
    )(c_all, ada_w, ada_b.reshape(depth, 1, n))


def _ffn_rows(x, shift, scale, gate, wi_ref, wo_ref, act_ref, lng, lnb, alpha):
    rows = x.shape[0]
    h = (x * (1.0 + scale) + shift).astype(BF16)
    for c in range(D_FF // FF_CHUNK):
        lo = c * FF_CHUNK
        ga = D_FF + lo
        a = _dot(h, wi_ref[lo // FF_CONV_COLS, :, lo % FF_CONV_COLS:lo % FF_CONV_COLS + FF_CHUNK])
        g = _dot(h, wi_ref[ga // FF_CONV_COLS, :, ga % FF_CONV_COLS:ga % FF_CONV_COLS + FF_CHUNK])
        act_ref[0:rows, lo:lo + FF_CHUNK] = (_silu(g) * a).astype(BF16)
    y = _dot(act_ref[0:rows, :], wo_ref[...])
    z = alpha * x + (1.0 + gate) * (0.5 * y)
    return _layer_norm(z, lng, lnb)


def _ffn_kernel(xp_ref, xs_ref, shp_ref, scp_ref, gtp_ref, shs_ref, scs_ref, gts_ref, win_ref, wout_ref,
                lng_ref, lnb_ref, op_ref, os_ref, wi_ref, wo_ref, act_ref, *, n_tiles, tiles_per_batch, alpha):
    step = pl.program_id(0)

    @pl.when(step < FF_CONV_STEPS)
    def _():
        wi_ref[step] = win_ref[...].astype(BF16)
        r0 = pl.multiple_of(step * FF_CONV_ROWS, FF_CONV_ROWS)
        wo_ref[pl.ds(r0, FF_CONV_ROWS), :] = wout_ref[...].astype(BF16)

    @pl.when(jnp.logical_and(step >= FF_CONV_STEPS, step < FF_CONV_STEPS + n_tiles))
    def _():
        batch = (step - FF_CONV_STEPS) // tiles_per_batch
        op_ref[...] = _ffn_rows(
            xp_ref[...], shp_ref[pl.ds(batch, 1), :], scp_ref[pl.ds(batch, 1), :], gtp_ref[pl.ds(batch, 1), :],
            wi_ref, wo_ref, act_ref, lng_ref[...], lnb_ref[...], alpha)

    @pl.when(step == FF_CONV_STEPS + n_tiles)
    def _():
        os_ref[...] = _ffn_rows(
            xs_ref[...], shs_ref[...], scs_ref[...], gts_ref[...],
            wi_ref, wo_ref, act_ref, lng_ref[...], lnb_ref[...], alpha)


def _ffn(xp, xs, mod_p, mod_s, layer, sub, w_in, w_out, ln_g, ln_b, *, rows_per_batch, alpha):
    m, d = xp.shape
    n_s = xs.shape[0]
    tm = min(FFN_ROWS, rows_per_batch)
    assert m % tm == 0 and rows_per_batch % tm == 0 and n_s <= tm
    n_tiles = m // tm
    last_conv = FF_CONV_STEPS - 1
    tile = lambda i: jnp.clip(i - FF_CONV_STEPS, 0, n_tiles - 1)
    mod_p_spec = lambda j: pl.BlockSpec((None, MOD_ROWS, d), lambda i: (layer, 0, j))
    mod_s_spec = lambda j: pl.BlockSpec((None, n_s, d), lambda i: (layer, 0, j))
    return pl.pallas_call(
        functools.partial(_ffn_kernel, n_tiles=n_tiles, tiles_per_batch=rows_per_batch // tm, alpha=alpha),
        grid=(FF_CONV_STEPS + n_tiles + 1,),
        in_specs=[
            pl.BlockSpec((tm, d), lambda i: (tile(i), 0)),
            pl.BlockSpec((n_s, d), lambda i: (0, 0)),
            mod_p_spec(3 * sub), mod_p_spec(3 * sub + 1), mod_p_spec(3 * sub + 2),
            mod_s_spec(3 * sub), mod_s_spec(3 * sub + 1), mod_s_spec(3 * sub + 2),
            pl.BlockSpec((None, d, FF_CONV_COLS), lambda i: (layer, 0, jnp.minimum(i, last_conv))),
            pl.BlockSpec((None, FF_CONV_ROWS, d), lambda i: (layer, jnp.minimum(i, last_conv), 0)),
            pl.BlockSpec((None, 1, d), lambda i: (layer * 3 + sub, 0, 0)),
            pl.BlockSpec((None, 1, d), lambda i: (layer * 3 + sub, 0, 0)),
        ],
        out_specs=[
            pl.BlockSpec((tm, d), lambda i: (tile(i), 0)),
            pl.BlockSpec((n_s, d), lambda i: (0, 0)),
        ],
        out_shape=[jax.ShapeDtypeStruct((m, d), F32), jax.ShapeDtypeStruct((n_s, d), F32)],
        scratch_shapes=[
            pltpu.VMEM((FF_CONV_STEPS, d, FF_CONV_COLS), BF16),
            pltpu.VMEM((D_FF, d), BF16),
            pltpu.VMEM((tm, D_FF), BF16),
        ],
        compiler_params=pltpu.CompilerParams(
            dimension_semantics=("arbitrary",), vmem_limit_bytes=VMEM_LIMIT_BYTES),
        name="ffn",
    )(xp, xs, mod_p, mod_p, mod_p, mod_s, mod_s, mod_s, w_in, w_out, ln_g, ln_b)


def _attention_block(q_blk, keys, vals, mask, fill):
    outs = []
    for j in range(KV_A):
        kj = keys[:, j * HEAD_DIM:(j + 1) * HEAD_DIM].astype(BF16)
        vj = vals[:, j * HEAD_DIM:(j + 1) * HEAD_DIM].astype(BF16)
        qj = jnp.concatenate(
            [q_blk[:, (j * G_A + g) * HEAD_DIM:(j * G_A + g + 1) * HEAD_DIM] for g in range(G_A)],
            axis=0).astype(BF16)
        s = _dot_t(qj, kj)
        s = jnp.concatenate(
            [jnp.where(mask, s[g * WINDOW:(g + 1) * WINDOW], fill[j][g]) for g in range(G_A)], axis=0)
        p = jnp.exp2(s - jnp.max(s, axis=-1, keepdims=True))
        den = jnp.sum(p, axis=-1, keepdims=True)
        o = _dot(p.astype(BF16), vj) * (1.0 / den)
        outs.extend(o[g * WINDOW:(g + 1) * WINDOW] for g in range(G_A))
    return jnp.concatenate(outs, axis=-1)


def _mixer_prompt_kernel(
        x_ref, sh_ref, sc_ref, gt_ref, win_ref, wout_ref, cos_ref, sl_ref, sh2_ref, sink_ref,
        gg_ref, gb_ref, ws_ref, bs_ref, lb_ref, ng_ref, lng_ref, lnb_ref,
        y_ref, kwin_ref, vwin_ref, st_out_ref,
        kprev_ref, vprev_ref, st_ref,
        *, layer, tq, alpha):
    b_idx = pl.program_id(0)
    t_idx = pl.program_id(1)
    n_t = pl.num_programs(1)

    @pl.when(t_idx == 0)
    def _():
        kprev_ref[...] = jnp.zeros_like(kprev_ref)
        vprev_ref[...] = jnp.zeros_like(vprev_ref)
        st_ref[...] = jnp.zeros_like(st_ref)

    x = x_ref[...]
    shift = sh_ref[pl.ds(b_idx, 1), :]
    scale = sc_ref[pl.ds(b_idx, 1), :]
    gate = gt_ref[pl.ds(b_idx, 1), :]
    h = (x * (1.0 + scale) + shift).astype(BF16)
    proj = _dot(h, win_ref[...])

    cos, s_lo, s_hi = cos_ref[...], sl_ref[...], sh2_ref[...]
    q = _rope(proj[:, OFF_Q:OFF_K], cos, s_lo, s_hi) * (HEAD_DIM ** -0.5 * LOG2_E)
    k = _rope(proj[:, OFF_K:OFF_V], cos, s_lo, s_hi)
    v = proj[:, OFF_V:OFF_U]

    qi = lax.broadcasted_iota(jnp.int32, (WINDOW, 2 * WINDOW), 0)
    kj = lax.broadcasted_iota(jnp.int32, (WINDOW, 2 * WINDOW), 1)
    dist = kj - qi
    band = (dist - 1).astype(jnp.uint32) < jnp.uint32(WINDOW)
    col0 = lax.broadcasted_iota(jnp.int32, (1, 2 * WINDOW), 1) == 0
    fill = [[jnp.where(col0, sink_ref[layer, j * G_A + g] * LOG2_E, NEG_BIG) for g in range(G_A)]
            for j in range(KV_A)]
    first_row = lax.broadcasted_iota(jnp.int32, (WINDOW, D_KV), 0) == 0
    o_a = []
    for blk in range(tq // WINDOW):
        r0 = blk * WINDOW
        if blk == 0:
            k_prev, v_prev = kprev_ref[...], vprev_ref[...]
            lower = jnp.where(t_idx == 0, WINDOW, 0)
            mask = jnp.logical_and(band, kj >= lower)
        else:
            k_prev, v_prev = k[r0 - WINDOW:r0], v[r0 - WINDOW:r0]
            mask = band
        keys = jnp.concatenate([k_prev, k[r0:r0 + WINDOW]], axis=0)
        vals = jnp.concatenate([jnp.where(first_row, 0.0, v_prev), v[r0:r0 + WINDOW]], axis=0)
        o_a.append(_attention_block(q[r0:r0 + WINDOW], keys, vals, mask, fill))
    o_a = jnp.concatenate(o_a, axis=0)
    kprev_ref[...] = k[tq - WINDOW:]
    vprev_ref[...] = v[tq - WINDOW:]
    kwin_ref[...] = k[tq - WINDOW:]
    vwin_ref[...] = v[tq - WINDOW:]

    vn = _layer_norm(proj[:, OFF_GV:OFF_CQ], gg_ref[...], gb_ref[...])
    u = proj[:, OFF_U:OFF_GV]
    tr = lax.broadcasted_iota(jnp.int32, (CHUNK_B, CHUNK_B), 0)
    tc = lax.broadcasted_iota(jnp.int32, (CHUNK_B, CHUNK_B), 1)
    wm = [jnp.where(tr >= tc, ws_ref[g], 0.0).astype(BF16) for g in range(G_B)]
    o_b = []
    for c in range(tq // CHUNK_B):
        vc = vn[c * CHUNK_B:(c + 1) * CHUNK_B].astype(BF16)
        mixed = jnp.concatenate(
            [_dot(wm[g], vc[:, g * HEAD_DIM:(g + 1) * HEAD_DIM]) for g in range(G_B)], axis=-1)
        o_b.append(u[c * CHUNK_B:(c + 1) * CHUNK_B] * (mixed + bs_ref[...]))
    o_b = jnp.concatenate(o_b, axis=0)

    n_ch = tq // HG_CHUNK
    half = HG_CHUNK // 2
    cq = proj[:, OFF_CQ:OFF_CF]
    ci = proj[:, OFF_CI:OFF_CG]
    lb = _hgrn_lower_bound(lb_ref, layer)
    f = lb + (1.0 - lb) * jax.nn.sigmoid(proj[:, OFF_CF:OFF_CI])
    lf = jnp.log(f)
    kc = 1.0 - f

    rt = lax.broadcasted_iota(jnp.int32, (tq, tq), 0)
    ct = lax.broadcasted_iota(jnp.int32, (tq, tq), 1)
    ltri = jnp.where(ct <= rt, jnp.where(ct >= (rt & -HG_CHUNK), 1.0, 0.0), 0.0).astype(BF16)
    lf_hi = lf.astype(BF16)
    lf_r = lf - lf_hi.astype(F32)
    lf_mid = lf_r.astype(BF16)
    lf_lo = (lf_r - lf_mid.astype(F32)).astype(BF16)
    bsum = (_dot(ltri, lf_hi) + _dot(ltri, lf_mid) + _dot(ltri, lf_lo)) * LOG2_E

    def chunks(a):
        return a.reshape(n_ch, HG_CHUNK, D_C)

    b3, cq3, kc3, ci3 = chunks(bsum), chunks(cq), chunks(kc), chunks(ci)
    b_end = b3[:, HG_CHUNK - 1:HG_CHUNK, :]
    eb = jnp.exp2(bsum)
    qb = cq * eb
    kb = kc * jnp.exp2(b_end - b3).reshape(tq, D_C)

    ones_bd = _head_block_ones()

    def pair_products(s, lo):
        e = cq3[:, lo:] * kc3[:, s:s + 1, :] * jnp.exp2(b3[:, lo:] - b3[:, s:s + 1, :])
        pos = lax.broadcasted_iota(jnp.int32, (n_ch, HG_CHUNK - lo, D_C), 1) + lo
        return jnp.where(pos >= s, e, 0.0)

    def head_sums(slab):
        return chunks(_dot(slab.reshape(tq, D_C).astype(BF16), ones_bd))

    o_all = jnp.zeros((n_ch, HG_CHUNK, D_C), F32)
    for s in range(half):
        o_all = o_all + head_sums(pair_products(s, 0)) * ci3[:, s:s + 1, :]
    o_up = jnp.zeros((n_ch, half, D_C), F32)
    for s in range(half, HG_CHUNK, 2):
        a3 = head_sums(jnp.concatenate([pair_products(s, half), pair_products(s + 1, half)], axis=1))
        o_up = o_up + a3[:, :half] * ci3[:, s:s + 1, :] + a3[:, half:] * ci3[:, s + 1:s + 2, :]
    o_c = (o_all + jnp.concatenate([jnp.zeros((n_ch, half, D_C), F32), o_up], axis=1)).reshape(tq, D_C)

    lane_head = lax.broadcasted_iota(jnp.int32, (tq, D_C), 1) >> 6
    im_h = [jnp.where(lane_head == hh, ci, 0.0).astype(BF16) for hh in range(H_C)]
    qb_h = [qb[:, hh * DK_C:(hh + 1) * DK_C].astype(BF16) for hh in range(H_C)]
    kb_h = [kb[:, hh * DK_C:(hh + 1) * DK_C].astype(BF16) for hh in range(H_C)]
    eb_t = eb.T
    lane_head_c = lax.broadcasted_iota(jnp.int32, (HG_CHUNK, D_C), 1) >> 6
    st = st_ref[...]
    o_inter = []
    for n in range(n_ch):
        r0, r1 = n * HG_CHUNK, (n + 1) * HG_CHUNK
        read = _dot(jnp.concatenate([a[r0:r1] for a in qb_h], axis=0), st.astype(BF16))
        o_n = jnp.zeros((HG_CHUNK, D_C), F32)
        for hh in range(H_C):
            o_n = o_n + jnp.where(lane_head_c == hh, read[hh * HG_CHUNK:(hh + 1) * HG_CHUNK], 0.0)
        o_inter.append(o_n)
        upd = _tdot(jnp.concatenate([a[r0:r1] for a in kb_h], axis=0),
                    jnp.concatenate([a[r0:r1] for a in im_h], axis=0))
        col = eb_t[:, r1 - 1:r1]
        decay = jnp.concatenate(
            [jnp.broadcast_to(col[hh * DK_C:(hh + 1) * DK_C], (DK_C, DV_C)) for hh in range(H_C)], axis=1)
        st = decay * st + upd
    st_ref[...] = st
    o_c = o_c + jnp.concatenate(o_inter, axis=0)

    @pl.when(t_idx == n_t - 1)
    def _():
        for hh in range(H_C):
            st_out_ref[hh * DK_C:(hh + 1) * DK_C, :] = st[:, hh * DV_C:(hh + 1) * DV_C]

    ms = _head_sum(o_c * o_c, ones_bd) * (1.0 / DV_C)
    o_c = o_c * lax.rsqrt(ms + RMS_EPS) * ng_ref[...] * _silu(proj[:, OFF_CG:])

    mix = _dot(jnp.concatenate([o_a, o_b, o_c], axis=-1).astype(BF16), wout_ref[...])
    z = alpha * x + (1.0 + gate) * mix
    y_ref[...] = _layer_norm(z, lng_ref[...], lnb_ref[...])


def _mixer_prompt(x, mods, layer, w_in, w_out, tables, sinks, gln_g, gln_b, ws, bs_exp, lb, ng_exp,
                  ln_g, ln_b, *, batch, seq, alpha):
    m, d = x.shape
    tq = min(MIX_ROWS, seq)
    assert seq % tq == 0 and tq % WINDOW == 0
    n_t = seq // tq
    depth = w_in.shape[0]
    cos, s_lo, s_hi = tables
    mod_spec = lambda j: pl.BlockSpec((None, MOD_ROWS, d), lambda b, t: (layer, 0, j))
    tab_spec = pl.BlockSpec((tq, LANES), lambda b, t: (t, 0))
    lay3 = lambda shape: pl.BlockSpec((None,) + shape, lambda b, t: (layer,) + (0,) * len(shape))
    return pl.pallas_call(
        functools.partial(_mixer_prompt_kernel, layer=layer, tq=tq, alpha=alpha),
        grid=(batch, n_t),
        in_specs=[
            pl.BlockSpec((tq, d), lambda b, t: (b * n_t + t, 0)),
            mod_spec(3), mod_spec(4), mod_spec(5),
            lay3((d, D_IN)), lay3((d, d)),
            tab_spec, tab_spec, tab_spec,
            pl.BlockSpec(memory_space=pltpu.SMEM),
            lay3((1, D_B)), lay3((1, D_B)),
            lay3((G_B, CHUNK_B, CHUNK_B)), lay3((CHUNK_B, D_B)),
            pl.BlockSpec((depth, D_C), lambda b, t: (0, 0)),
            lay3((1, D_C)),
            pl.BlockSpec((None, 1, d), lambda b, t: (layer * 3 + 1, 0, 0)),
            pl.BlockSpec((None, 1, d), lambda b, t: (layer * 3 + 1, 0, 0)),
        ],
        out_specs=[
            pl.BlockSpec((tq, d), lambda b, t: (b * n_t + t, 0)),
            pl.BlockSpec((None, WINDOW, D_KV), lambda b, t: (b, 0, 0)),
            pl.BlockSpec((None, WINDOW, D_KV), lambda b, t: (b, 0, 0)),
            pl.BlockSpec((None, D_C, DV_C), lambda b, t: (b, 0, 0)),
        ],
        out_shape=[
            jax.ShapeDtypeStruct((m, d), F32),
            jax.ShapeDtypeStruct((batch, WINDOW, D_KV), F32),
            jax.ShapeDtypeStruct((batch, WINDOW, D_KV), F32),
            jax.ShapeDtypeStruct((batch, D_C, DV_C), F32),
        ],
        scratch_shapes=[
            pltpu.VMEM((WINDOW, D_KV), F32),
            pltpu.VMEM((WINDOW, D_KV), F32),
            pltpu.VMEM((DK_C, D_C), F32),
        ],
        compiler_params=pltpu.CompilerParams(
            dimension_semantics=("arbitrary", "arbitrary"), vmem_limit_bytes=VMEM_LIMIT_BYTES),
        name="mixer_prompt",
    )(x, mods, mods, mods, w_in, w_out, cos, s_lo, s_hi, sinks, gln_g, gln_b, ws, bs_exp, lb, ng_exp,
      ln_g, ln_b)


def _mixer_sample_kernel(
        x_ref, sh_ref, sc_ref, gt_ref, win_ref, wout_ref, cos_ref, sl_ref, sh2_ref, sink_ref,
        gg_ref, gb_ref, w0_ref, bs_ref, lb_ref, ng_ref, lng_ref, lnb_ref, ck_ref, cv_ref, s0_ref,
        y_ref, knew_ref, vnew_ref, gv_ref, s1_ref,
        qt_ref, kn_ref, vn_ref, sn_ref, sc_scr, ot_ref, ft_ref, kt_ref, qc_ref, it_ref, ob_ref, og_ref, oc_ref,
        *, layer, n_s, nb, alpha):
    step = pl.program_id(0)
    rows_all = n_s * H_A

    @pl.when(step == 0)
    def _():
        x = x_ref[...]
        h = (x * (1.0 + sc_ref[...]) + sh_ref[...]).astype(BF16)
        proj = _dot(h, win_ref[...])
        cos, s_lo, s_hi = cos_ref[...], sl_ref[...], sh2_ref[...]
        q = _rope(proj[:, OFF_Q:OFF_K], cos, s_lo, s_hi) * (HEAD_DIM ** -0.5)
        k = _rope(proj[:, OFF_K:OFF_V], cos, s_lo, s_hi)
        v = proj[:, OFF_V:OFF_U]
        knew_ref[...] = k
        vnew_ref[...] = v
        zeros64 = jnp.zeros((n_s, HEAD_DIM), F32)
        for hd in range(H_A):
            qh = q[:, hd * HEAD_DIM:(hd + 1) * HEAD_DIM]
            row = jnp.concatenate([qh, zeros64] if hd < G_A else [zeros64, qh], axis=-1)
            qt_ref[pl.ds(hd, n_s, stride=H_A), :] = row
            kn_ref[pl.ds(hd, n_s, stride=H_A), :] = k
            vn_ref[pl.ds(hd, n_s, stride=H_A), :] = v
        s_new = jnp.sum(qt_ref[...] * kn_ref[...], axis=-1, keepdims=True)
        sn_ref[...] = jnp.broadcast_to(s_new, (rows_all, LANES))

        vn = _layer_norm(proj[:, OFF_GV:OFF_CQ], gg_ref[...], gb_ref[...])
        gv_ref[...] = vn
        ob_ref[...] = proj[:, OFF_U:OFF_GV] * (vn * w0_ref[...] + bs_ref[0:1, :])

        lb = _hgrn_lower_bound(lb_ref, layer)
        f = lb + (1.0 - lb) * jax.nn.sigmoid(proj[:, OFF_CF:OFF_CI])
        ft_ref[...] = f.T
        kt_ref[...] = (1.0 - f).T
        qc_ref[...] = proj[:, OFF_CQ:OFF_CF].T
        it_ref[...] = proj[:, OFF_CI:OFF_CG].T
        og_ref[...] = ng_ref[...] * _silu(proj[:, OFF_CG:])

    base = pl.multiple_of(step * (nb * H_A), nb * H_A)
    for b in range(nb):
        qrows = qt_ref[pl.ds(base + b * H_A, H_A), :].astype(BF16)
        sc_scr[b * H_A:(b + 1) * H_A, :] = _dot(qrows, ck_ref[b].astype(BF16))
    lane = lax.broadcasted_iota(jnp.int32, (nb * H_A, WINDOW), 1)
    s = jnp.where(lane == 0, NEG_BIG, sc_scr[...])
    s_new = sn_ref[pl.ds(base, nb * H_A), 0:1]
    hrow = lax.broadcasted_iota(jnp.int32, (nb * H_A, 1), 0) & (H_A - 1)
    sink = jnp.zeros((nb * H_A, 1), F32)
    for hd in range(H_A):
        sink = jnp.where(hrow == hd, sink_ref[layer, hd], sink)
    mx = jnp.maximum(jnp.maximum(jnp.max(s, axis=-1, keepdims=True), s_new), sink)
    p = jnp.exp(s - mx)
    p_new = jnp.exp(s_new - mx)
    den = jnp.sum(p, axis=-1, keepdims=True) + p_new + jnp.exp(sink - mx)
    sc_scr[...] = p
    for b in range(nb):
        prow = sc_scr[b * H_A:(b + 1) * H_A, :].astype(BF16)
        ot_ref[pl.ds(base + b * H_A, H_A), :] = _dot_t(prow, cv_ref[b].astype(BF16))
    rows = pl.ds(base, nb * H_A)
    ot_ref[rows, :] = (ot_ref[rows, :] + p_new * vn_ref[rows, :]) * (1.0 / den)

    hrow0 = pl.multiple_of(step * DK_C, DK_C)
    i_h = it_ref[pl.ds(hrow0, DV_C), :]
    acc = jnp.zeros((DV_C, n_s), F32)
    for kk in range(DK_C):
        f_k = ft_ref[pl.ds(hrow0 + kk, 1), :]
        k_k = kt_ref[pl.ds(hrow0 + kk, 1), :]
        q_k = qc_ref[pl.ds(hrow0 + kk, 1), :]
        s1 = f_k * s0_ref[kk] + k_k * i_h
        s1_ref[kk] = s1
        acc = acc + q_k * s1
    oc_ref[pl.ds(hrow0, DV_C), :] = acc

    @pl.when(step == pl.num_programs(0) - 1)
    def _():
        o_a = []
        for hd in range(H_A):
            r = ot_ref[pl.ds(hd, n_s, stride=H_A), :]
            o_a.append(r[:, :HEAD_DIM] if hd < G_A else r[:, HEAD_DIM:])
        o_c = oc_ref[...].T
        ms = _head_sum(o_c * o_c, _head_block_ones()) * (1.0 / DV_C)
        o_c = o_c * lax.rsqrt(ms + RMS_EPS) * og_ref[...]
        mix = _dot(jnp.concatenate(o_a + [ob_ref[...], o_c], axis=-1).astype(BF16), wout_ref[...])
        z = alpha * x_ref[...] + (1.0 + gt_ref[...]) * mix
        y_ref[...] = _layer_norm(z, lng_ref[...], lnb_ref[...])


def _mixer_sample(x, mods, layer, w_in, w_out, tables, sinks, gln_g, gln_b, w0_exp, bs_exp, lb, ng_exp,
                  ln_g, ln_b, cache_k, cache_v, state, *, alpha):
    n_s, d = x.shape
    assert n_s % (H_C * SUBLANES) == 0 and n_s % LANES == 0
    nb = n_s // H_C
    depth = w_in.shape[0]
    cos, s_lo, s_hi = tables
    mod_spec = lambda j: pl.BlockSpec((None, n_s, d), lambda i: (layer, 0, j))
    tab_spec = pl.BlockSpec((1, LANES), lambda i: (0, 0))
    lay3 = lambda shape: pl.BlockSpec((None,) + shape, lambda i: (layer,) + (0,) * len(shape))
    rows = pl.BlockSpec((n_s, d), lambda i: (0, 0))
    per_head = pltpu.VMEM((D_C, n_s), F32)
    per_row = pltpu.VMEM((n_s * H_A, LANES), F32)
    return pl.pallas_call(
        functools.partial(_mixer_sample_kernel, layer=layer, n_s=n_s, nb=nb, alpha=alpha),
        grid=(H_C,),
        in_specs=[
            rows,
            mod_spec(3), mod_spec(4), mod_spec(5),
            lay3((d, D_IN)), lay3((d, d)),
            tab_spec, tab_spec, tab_spec,
            pl.BlockSpec(memory_space=pltpu.SMEM),
            lay3((1, D_B)), lay3((1, D_B)), lay3((1, D_B)), lay3((CHUNK_B, D_B)),
            pl.BlockSpec((depth, D_C), lambda i: (0, 0)),
            lay3((1, D_C)),
            pl.BlockSpec((None, 1, d), lambda i: (layer * 3 + 1, 0, 0)),
            pl.BlockSpec((None, 1, d), lambda i: (layer * 3 + 1, 0, 0)),
            pl.BlockSpec((None, nb, D_KV, WINDOW), lambda i: (layer, i, 0, 0)),
            pl.BlockSpec((None, nb, D_KV, WINDOW), lambda i: (layer, i, 0, 0)),
            pl.BlockSpec((None, None, DK_C, DV_C, n_s), lambda i: (layer, i, 0, 0, 0)),
        ],
        out_specs=[
            rows,
            pl.BlockSpec((n_s, D_KV), lambda i: (0, 0)),
            pl.BlockSpec((n_s, D_KV), lambda i: (0, 0)),
            pl.BlockSpec((n_s, D_B), lambda i: (0, 0)),
            pl.BlockSpec((None, DK_C, DV_C, n_s), lambda i: (i, 0, 0, 0)),
        ],
        out_shape=[
            jax.ShapeDtypeStruct((n_s, d), F32),
            jax.ShapeDtypeStruct((n_s, D_KV), F32),
            jax.ShapeDtypeStruct((n_s, D_KV), F32),
            jax.ShapeDtypeStruct((n_s, D_B), F32),
            jax.ShapeDtypeStruct((H_C, DK_C, DV_C, n_s), F32),
        ],
        scratch_shapes=[
            per_row,
            per_row,
            per_row,
            per_row,
            pltpu.VMEM((nb * H_A, WINDOW), F32),
            per_row,
            per_head, per_head, per_head, per_head,
            pltpu.VMEM((n_s, D_B), F32),
            pltpu.VMEM((n_s, D_C), F32),
            per_head,
        ],
        compiler_params=pltpu.CompilerParams(
            dimension_semantics=("arbitrary",), vmem_limit_bytes=VMEM_LIMIT_BYTES),
        name="mixer_sample",
    )(x, mods, mods, mods, w_in, w_out, cos, s_lo, s_hi, sinks, gln_g, gln_b, w0_exp, bs_exp, lb, ng_exp,
      ln_g, ln_b, cache_k, cache_v, state)


def _rope_tables(positions):
    half = ROT_DIM // 2
    f32 = np.float32
    inv = (ROPE_THETA ** (-np.arange(half, dtype=f32) * 2.0 / ROT_DIM)).astype(f32)
    ang = (positions.astype(f32)[:, None] * inv[None, :]).astype(f32)
    cos, sin = np.cos(ang).astype(f32), np.sin(ang).astype(f32)
    n = positions.shape[0]
    rest = HEAD_DIM - ROT_DIM
    cos_h = np.concatenate([cos, cos, np.ones((n, rest), f32)], axis=-1)
    lo_h = np.concatenate([-sin, np.zeros((n, half + rest), f32)], axis=-1)
    hi_h = np.concatenate([np.zeros((n, half), f32), sin, np.zeros((n, rest), f32)], axis=-1)
    two = lambda a: jnp.asarray(np.concatenate([a, a], axis=-1))
    return two(cos_h), two(lo_h), two(hi_h)


def kernel(x_prompt, x_sample, cache_k, cache_v, state_hgrn, c_prompt, c_sample, w_in, w_out, attn_sinks,
           gmlp_ln_g, gmlp_ln_b, gmlp_ws, gmlp_bs, hgrn_lb, hgrn_norm_g, ffn1_in, ffn1_out, ffn2_in,
           ffn2_out, ada_w, ada_b, ln_g, ln_b):
    batch, seq, d = x_prompt.shape
    n_s, dec_seq, _ = x_sample.shape
    depth = w_in.shape[0]
    assert d == D_MODEL and dec_seq == 1 and batch <= MOD_ROWS
    assert cache_k.shape[2] == WINDOW
    alpha = (2 * depth) ** 0.25

    w_in_b, w_out_b = w_in.astype(BF16), w_out.astype(BF16)
    ln_g3 = ln_g.reshape(depth * 3, 1, d)
    ln_b3 = ln_b.reshape(depth * 3, 1, d)
    gln_g = gmlp_ln_g.reshape(depth, 1, D_B)
    gln_b = gmlp_ln_b.reshape(depth, 1, D_B)
    bs_exp = jnp.repeat(jnp.swapaxes(gmlp_bs, 1, 2), HEAD_DIM, axis=2)
    w0_exp = jnp.repeat(gmlp_ws[:, :, 0, 0], HEAD_DIM, axis=1).reshape(depth, 1, D_B)
    ng_exp = jnp.tile(hgrn_norm_g, (1, H_C)).reshape(depth, 1, D_C)
    tab_p = _rope_tables(np.arange(seq))
    tab_s = _rope_tables(PAST_LEN + np.arange(dec_seq))

    c_all = jnp.concatenate([c_prompt, jnp.zeros((MOD_ROWS - batch, d), F32), c_sample], axis=0)
    mod_p, mod_s = _adaln(c_all, ada_w, ada_b)

    ck = jnp.transpose(cache_k, (0, 1, 3, 4, 2)).reshape(depth, n_s, D_KV, WINDOW)
    cv = jnp.transpose(cache_v, (0, 1, 3, 4, 2)).reshape(depth, n_s, D_KV, WINDOW)
    s0 = jnp.transpose(state_hgrn, (0, 2, 3, 4, 1))

    xp = x_prompt.reshape(batch * seq, d)
    xs = x_sample.reshape(n_s, d)
    ffn = functools.partial(_ffn, rows_per_batch=seq, alpha=alpha)
    kw, vw, st_p, kn, vnw, gvs, st_s = [], [], [], [], [], [], []
    for l in range(depth):
        xp, xs = ffn(xp, xs, mod_p, mod_s, l, 0, ffn1_in, ffn1_out, ln_g3, ln_b3)
        xp, k_l, v_l, s_l = _mixer_prompt(
            xp, mod_p, l, w_in_b, w_out_b, tab_p, attn_sinks, gln_g, gln_b, gmlp_ws, bs_exp, hgrn_lb,
            ng_exp, ln_g3, ln_b3, batch=batch, seq=seq, alpha=alpha)
        kw.append(k_l), vw.append(v_l), st_p.append(s_l)
        xs, k_l, v_l, g_l, s_l = _mixer_sample(
            xs, mod_s, l, w_in_b, w_out_b, tab_s, attn_sinks, gln_g, gln_b, w0_exp, bs_exp, hgrn_lb,
            ng_exp, ln_g3, ln_b3, ck, cv, s0, alpha=alpha)
        kn.append(k_l), vnw.append(v_l), gvs.append(g_l), st_s.append(s_l)
        xp, xs = ffn(xp, xs, mod_p, mod_s, l, 2, ffn2_in, ffn2_out, ln_g3, ln_b3)

    stack = lambda parts, shape: jnp.stack(parts, axis=0).reshape((depth,) + shape)
    return (
        xp.reshape(batch, seq, d),
        xs.reshape(n_s, dec_seq, d),
        stack(kw, (batch, WINDOW, KV_A, HEAD_DIM)),
        stack(vw, (batch, WINDOW, KV_A, HEAD_DIM)),
        stack(st_p, (batch, H_C, DK_C, DV_C)),
        stack(kn, (n_s, dec_seq, KV_A, HEAD_DIM)),
        stack(vnw, (n_s, dec_seq, KV_A, HEAD_DIM)),
        stack(gvs, (n_s, dec_seq, D_B)),
        jnp.transpose(jnp.stack(st_s, axis=0), (0, 4, 1, 2, 3)),
    )
```

```python
import functools

import numpy as np
import jax
import jax.numpy as jnp
from jax import lax
from jax.experimental import pallas as pl
from jax.experimental.pallas import tpu as pltpu

F32 = jnp.float32
BF16 = jnp.bfloat16

D_MODEL = 1024
HEAD_DIM = 64
H_A = 8
KV_A = 2
G_A = H_A // KV_A
D_A = H_A * HEAD_DIM
D_KV = KV_A * HEAD_DIM
WINDOW = 128
ROT_DIM = HEAD_DIM // 4
ROPE_THETA = 500000.0
G_B = 4
D_B = G_B * HEAD_DIM
CHUNK_B = 128
H_C = 4
DK_C = 64
DV_C = 64
D_C = H_C * DV_C
D_IN = 2304
D_FF = 2816
N_MOD = 9
LN_EPS = 1e-5
RMS_EPS = 1e-6
NEG_BIG = -1e30
LOG2_E = 1.4426950408889634
PAST_LEN = 16384

OFF_Q = 0
OFF_K = OFF_Q + D_A
OFF_V = OFF_K + D_KV
OFF_U = OFF_V + D_KV
OFF_GV = OFF_U + D_B
OFF_CQ = OFF_GV + D_B
OFF_CF = OFF_CQ + D_C
OFF_CI = OFF_CF + D_C
OFF_CG = OFF_CI + D_C

SUBLANES = 8
LANES = 128
MXU_DIM = 256
VMEM_LIMIT_BYTES = 56 * 1024 * 1024

FFN_ROWS = 1024
FF_CHUNK = MXU_DIM
FF_CONV_STEPS = 11
FF_CONV_COLS = 2 * D_FF // FF_CONV_STEPS
FF_CONV_ROWS = D_FF // FF_CONV_STEPS
ADA_COLS = 1024
MIX_ROWS = 512
HG_CHUNK = 16
MOD_ROWS = SUBLANES


def _dot(a, b):
    return jnp.dot(a, b, preferred_element_type=F32)


def _dot_t(a, b):
    return lax.dot_general(a, b, (((1,), (1,)), ((), ())), preferred_element_type=F32)


def _tdot(a, b):
    return lax.dot_general(a, b, (((0,), (0,)), ((), ())), preferred_element_type=F32)


def _layer_norm(z, g, b):
    mu = jnp.mean(z, axis=-1, keepdims=True)
    zc = z - mu
    var = jnp.mean(zc * zc, axis=-1, keepdims=True)
    return zc * lax.rsqrt(var + LN_EPS) * g + b


def _silu(x):
    return x * jax.nn.sigmoid(x)


def _head_block_ones():
    r = lax.broadcasted_iota(jnp.int32, (D_C, D_C), 0) >> 6
    c = lax.broadcasted_iota(jnp.int32, (D_C, D_C), 1) >> 6
    return jnp.where(r == c, 1.0, 0.0).astype(BF16)


def _head_sum(x, ones_bd):
    hi = x.astype(BF16)
    lo = (x - hi.astype(F32)).astype(BF16)
    return _dot(hi, ones_bd) + _dot(lo, ones_bd)


def _rope(x, cos, sin_lo, sin_hi):
    half = ROT_DIM // 2
    out = []
    for c in range(x.shape[-1] // LANES):
        xc = x[:, c * LANES:(c + 1) * LANES]
        out.append(xc * cos + pltpu.roll(xc, LANES - half, 1) * sin_lo + pltpu.roll(xc, half, 1) * sin_hi)
    return out[0] if len(out) == 1 else jnp.concatenate(out, axis=-1)


def _hgrn_lower_bound(lb_ref, layer):
    w = lb_ref[...]
    e = jnp.exp(w - jnp.max(w, axis=0, keepdims=True))
    p = e / jnp.sum(e, axis=0, keepdims=True)
    lb = jnp.zeros((1, D_C), F32)
    for j in range(1, layer + 1):
        lb = lb + p[j:j + 1, :]
    return lb


def _adaln_kernel(c_ref, w_ref, b_ref, op_ref, os_ref):
    s = _silu(c_ref[...]).astype(BF16)
    r = _dot(s, w_ref[...].astype(BF16)) + b_ref[...]
    op_ref[...] = r[:MOD_ROWS]
    os_ref[...] = r[MOD_ROWS:]


def _adaln(c_all, ada_w, ada_b):
    depth, d, n = ada_w.shape
    rows = c_all.shape[0]
    n_s = rows - MOD_ROWS
    return pl.pallas_call(
        _adaln_kernel,
        grid=(depth, n // ADA_COLS),
        in_specs=[
            pl.BlockSpec((rows, d), lambda l, j: (0, 0)),
            pl.BlockSpec((None, d, ADA_COLS), lambda l, j: (l, 0, j)),
            pl.BlockSpec((None, 1, ADA_COLS), lambda l, j: (l, 0, j)),
        ],
        out_specs=[
            pl.BlockSpec((None, MOD_ROWS, ADA_COLS), lambda l, j: (l, 0, j)),
            pl.BlockSpec((None, n_s, ADA_COLS), lambda l, j: (l, 0, j)),
        ],
        out_shape=[
            jax.ShapeDtypeStruct((depth, MOD_ROWS, n), F32),
            jax.ShapeDtypeStruct((depth, n_s, n), F32),
        ],
        compiler_params=pltpu.CompilerParams(
            dimension_semantics=("arbitrary", "arbitrary"), vmem_limit_bytes=VMEM_LIMIT_BYTES),
        name="adaln",
    )(c_all, ada_w, ada_b.reshape(depth, 1, n))


def _ffn_rows(x, shift, scale, gate, wi_ref, wo_ref, act_ref, lng, lnb, alpha):
    rows = x.shape[0]
    h = (x * (1.0 + scale) + shift).astype(BF16)
    for c in range(D_FF // FF_CHUNK):
        lo = c * FF_CHUNK
        ga = D_FF + lo
        a = _dot(h, wi_ref[lo // FF_CONV_COLS, :, lo % FF_CONV_COLS:lo % FF_CONV_COLS + FF_CHUNK])
        g = _dot(h, wi_ref[ga // FF_CONV_COLS, :, ga % FF_CONV_COLS:ga % FF_CONV_COLS + FF_CHUNK])
        act_ref[0:rows, lo:lo + FF_CHUNK] = (_silu(g) * a).astype(BF16)
    y = _dot(act_ref[0:rows, :], wo_ref[...])
    z = alpha * x + (1.0 + gate) * (0.5 * y)
    return _layer_norm(z, lng, lnb)


def _ffn_kernel(xp_ref, xs_ref, shp_ref, scp_ref, gtp_ref, shs_ref, scs_ref, gts_ref, win_ref, wout_ref,
                lng_ref, lnb_ref, op_ref, os_ref, wi_ref, wo_ref, act_ref, *, n_tiles, tiles_per_batch, alpha):
    step = pl.program_id(0)

    @pl.when(step < FF_CONV_STEPS)
    def _():
        wi_ref[step] = win_ref[...].astype(BF16)
        r0 = pl.multiple_of(step * FF_CONV_ROWS, FF_CONV_ROWS)
        wo_ref[pl.ds(r0, FF_CONV_ROWS), :] = wout_ref[...].astype(BF16)

    @pl.when(jnp.logical_and(step >= FF_CONV_STEPS, step < FF_CONV_STEPS + n_tiles))
    def _():
        batch = (step - FF_CONV_STEPS) // tiles_per_batch
        op_ref[...] = _ffn_rows(
            xp_ref[...], shp_ref[pl.ds(batch, 1), :], scp_ref[pl.ds(batch, 1), :], gtp_ref[pl.ds(batch, 1), :],
            wi_ref, wo_ref, act_ref, lng_ref[...], lnb_ref[...], alpha)

    @pl.when(step == FF_CONV_STEPS + n_tiles)
    def _():
        os_ref[...] = _ffn_rows(
            xs_ref[...], shs_ref[...], scs_ref[...], gts_ref[...],
            wi_ref, wo_ref, act_ref, lng_ref[...], lnb_ref[...], alpha)


def _ffn(xp, xs, mod_p, mod_s, layer, sub, w_in, w_out, ln_g, ln_b, *, rows_per_batch, alpha):
    m, d = xp.shape
    n_s = xs.shape[0]
    tm = min(FFN_ROWS, rows_per_batch)
    assert m % tm == 0 and rows_per_batch % tm == 0 and n_s <= tm
    n_tiles = m // tm
    last_conv = FF_CONV_STEPS - 1
    tile = lambda i: jnp.clip(i - FF_CONV_STEPS, 0, n_tiles - 1)
    mod_p_spec = lambda j: pl.BlockSpec((None, MOD_ROWS, d), lambda i: (layer, 0, j))
    mod_s_spec = lambda j: pl.BlockSpec((None, n_s, d), lambda i: (layer, 0, j))
    return pl.pallas_call(
        functools.partial(_ffn_kernel, n_tiles=n_tiles, tiles_per_batch=rows_per_batch // tm, alpha=alpha),
        grid=(FF_CONV_STEPS + n_tiles + 1,),
        in_specs=[
            pl.BlockSpec((tm, d), lambda i: (tile(i), 0)),
            pl.BlockSpec((n_s, d), lambda i: (0, 0)),
            mod_p_spec(3 * sub), mod_p_spec(3 * sub + 1), mod_p_spec(3 * sub + 2),
            mod_s_spec(3 * sub), mod_s_spec(3 * sub + 1), mod_s_spec(3 * sub + 2),
            pl.BlockSpec((None, d, FF_CONV_COLS), lambda i: (layer, 0, jnp.minimum(i, last_conv))),
            pl.BlockSpec((None, FF_CONV_ROWS, d), lambda i: (layer, jnp.minimum(i, last_conv), 0)),
            pl.BlockSpec((None, 1, d), lambda i: (layer * 3 + sub, 0, 0)),
            pl.BlockSpec((None, 1, d), lambda i: (layer * 3 + sub, 0, 0)),
        ],
        out_specs=[
            pl.BlockSpec((tm, d), lambda i: (tile(i), 0)),
            pl.BlockSpec((n_s, d), lambda i: (0, 0)),
        ],
        out_shape=[jax.ShapeDtypeStruct((m, d), F32), jax.ShapeDtypeStruct((n_s, d), F32)],
        scratch_shapes=[
            pltpu.VMEM((FF_CONV_STEPS, d, FF_CONV_COLS), BF16),
            pltpu.VMEM((D_FF, d), BF16),
            pltpu.VMEM((tm, D_FF), BF16),
        ],
        compiler_params=pltpu.CompilerParams(
            dimension_semantics=("arbitrary",), vmem_limit_bytes=VMEM_LIMIT_BYTES),
        name="ffn",
    )(xp, xs, mod_p, mod_p, mod_p, mod_s, mod_s, mod_s, w_in, w_out, ln_g, ln_b)


def _attention_block(q_blk, keys, vals, mask, fill):
    outs = []
    for j in range(KV_A):
        kj = keys[:, j * HEAD_DIM:(j + 1) * HEAD_DIM].astype(BF16)
        vj = vals[:, j * HEAD_DIM:(j + 1) * HEAD_DIM].astype(BF16)
        qj = jnp.concatenate(
            [q_blk[:, (j * G_A + g) * HEAD_DIM:(j * G_A + g + 1) * HEAD_DIM] for g in range(G_A)],
            axis=0).astype(BF16)
        s = _dot_t(qj, kj)
        s = jnp.concatenate(
            [jnp.where(mask, s[g * WINDOW:(g + 1) * WINDOW], fill[j][g]) for g in range(G_A)], axis=0)
        p = jnp.exp2(s - jnp.max(s, axis=-1, keepdims=True))
        den = jnp.sum(p, axis=-1, keepdims=True)
        o = _dot(p.astype(BF16), vj) * (1.0 / den)
        outs.extend(o[g * WINDOW:(g + 1) * WINDOW] for g in range(G_A))
    return jnp.concatenate(outs, axis=-1)


def _mixer_prompt_kernel(
        x_ref, sh_ref, sc_ref, gt_ref, win_ref, wout_ref, cos_ref, sl_ref, sh2_ref, sink_ref,
        gg_ref, gb_ref, ws_ref, bs_ref, lb_ref, ng_ref, lng_ref, lnb_ref,
        y_ref, kwin_ref, vwin_ref, st_out_ref,
        kprev_ref, vprev_ref, st_ref,
        *, layer, tq, alpha):
    b_idx = pl.program_id(0)
    t_idx = pl.program_id(1)
    n_t = pl.num_programs(1)

    @pl.when(t_idx == 0)
    def _():
        kprev_ref[...] = jnp.zeros_like(kprev_ref)
        vprev_ref[...] = jnp.zeros_like(vprev_ref)
        st_ref[...] = jnp.zeros_like(st_ref)

    x = x_ref[...]
    shift = sh_ref[pl.ds(b_idx, 1), :]
    scale = sc_ref[pl.ds(b_idx, 1), :]
    gate = gt_ref[pl.ds(b_idx, 1), :]
    h = (x * (1.0 + scale) + shift).astype(BF16)
    proj = _dot(h, win_ref[...])

    cos, s_lo, s_hi = cos_ref[...], sl_ref[...], sh2_ref[...]
    q = _rope(proj[:, OFF_Q:OFF_K], cos, s_lo, s_hi) * (HEAD_DIM ** -0.5 * LOG2_E)
    k = _rope(proj[:, OFF_K:OFF_V], cos, s_lo, s_hi)
    v = proj[:, OFF_V:OFF_U]

    qi = lax.broadcasted_iota(jnp.int32, (WINDOW, 2 * WINDOW), 0)
    kj = lax.broadcasted_iota(jnp.int32, (WINDOW, 2 * WINDOW), 1)
    dist = kj - qi
    band = (dist - 1).astype(jnp.uint32) < jnp.uint32(WINDOW)
    col0 = lax.broadcasted_iota(jnp.int32, (1, 2 * WINDOW), 1) == 0
    fill = [[jnp.where(col0, sink_ref[layer, j * G_A + g] * LOG2_E, NEG_BIG) for g in range(G_A)]
            for j in range(KV_A)]
    first_row = lax.broadcasted_iota(jnp.int32, (WINDOW, D_KV), 0) == 0
    o_a = []
    for blk in range(tq // WINDOW):
        r0 = blk * WINDOW
        if blk == 0:
            k_prev, v_prev = kprev_ref[...], vprev_ref[...]
            lower = jnp.where(t_idx == 0, WINDOW, 0)
            mask = jnp.logical_and(band, kj >= lower)
        else:
            k_prev, v_prev = k[r0 - WINDOW:r0], v[r0 - WINDOW:r0]
            mask = band
        keys = jnp.concatenate([k_prev, k[r0:r0 + WINDOW]], axis=0)
        vals = jnp.concatenate([jnp.where(first_row, 0.0, v_prev), v[r0:r0 + WINDOW]], axis=0)
        o_a.append(_attention_block(q[r0:r0 + WINDOW], keys, vals, mask, fill))
    o_a = jnp.concatenate(o_a, axis=0)
    kprev_ref[...] = k[tq - WINDOW:]
    vprev_ref[...] = v[tq - WINDOW:]
    kwin_ref[...] = k[tq - WINDOW:]
    vwin_ref[...] = v[tq - WINDOW:]

    vn = _layer_norm(proj[:, OFF_GV:OFF_CQ], gg_ref[...], gb_ref[...])
    u = proj[:, OFF_U:OFF_GV]
    tr = lax.broadcasted_iota(jnp.int32, (CHUNK_B, CHUNK_B), 0)
    tc = lax.broadcasted_iota(jnp.int32, (CHUNK_B, CHUNK_B), 1)
    wm = [jnp.where(tr >= tc, ws_ref[g], 0.0).astype(BF16) for g in range(G_B)]
    o_b = []
    for c in range(tq // CHUNK_B):
        vc = vn[c * CHUNK_B:(c + 1) * CHUNK_B].astype(BF16)
        mixed = jnp.concatenate(
            [_dot(wm[g], vc[:, g * HEAD_DIM:(g + 1) * HEAD_DIM]) for g in range(G_B)], axis=-1)
        o_b.append(u[c * CHUNK_B:(c + 1) * CHUNK_B] * (mixed + bs_ref[...]))
    o_b = jnp.concatenate(o_b, axis=0)

    n_ch = tq // HG_CHUNK
    half = HG_CHUNK // 2
    cq = proj[:, OFF_CQ:OFF_CF]
    ci = proj[:, OFF_CI:OFF_CG]
    lb = _hgrn_lower_bound(lb_ref, layer)
    f = lb + (1.0 - lb) * jax.nn.sigmoid(proj[:, OFF_CF:OFF_CI])
    lf = jnp.log(f)
    kc = 1.0 - f

    rt = lax.broadcasted_iota(jnp.int32, (tq, tq), 0)
    ct = lax.broadcasted_iota(jnp.int32, (tq, tq), 1)
    ltri = jnp.where(ct <= rt, jnp.where(ct >= (rt & -HG_CHUNK), 1.0, 0.0), 0.0).astype(BF16)
    lf_hi = lf.astype(BF16)
    lf_r = lf - lf_hi.astype(F32)
    lf_mid = lf_r.astype(BF16)
    lf_lo = (lf_r - lf_mid.astype(F32)).astype(BF16)
    bsum = (_dot(ltri, lf_hi) + _dot(ltri, lf_mid) + _dot(ltri, lf_lo)) * LOG2_E

    def chunks(a):
        return a.reshape(n_ch, HG_CHUNK, D_C)

    b3, cq3, kc3, ci3 = chunks(bsum), chunks(cq), chunks(kc), chunks(ci)
    b_end = b3[:, HG_CHUNK - 1:HG_CHUNK, :]
    eb = jnp.exp2(bsum)
    qb = cq * eb
    kb = kc * jnp.exp2(b_end - b3).reshape(tq, D_C)

    ones_bd = _head_block_ones()

    def pair_products(s, lo):
        e = cq3[:, lo:] * kc3[:, s:s + 1, :] * jnp.exp2(b3[:, lo:] - b3[:, s:s + 1, :])
        pos = lax.broadcasted_iota(jnp.int32, (n_ch, HG_CHUNK - lo, D_C), 1) + lo
        return jnp.where(pos >= s, e, 0.0)

    def head_sums(slab):
        return chunks(_dot(slab.reshape(tq, D_C).astype(BF16), ones_bd))

    o_all = jnp.zeros((n_ch, HG_CHUNK, D_C), F32)
    for s in range(half):
        o_all = o_all + head_sums(pair_products(s, 0)) * ci3[:, s:s + 1, :]
    o_up = jnp.zeros((n_ch, half, D_C), F32)
    for s in range(half, HG_CHUNK, 2):
        a3 = head_sums(jnp.concatenate([pair_products(s, half), pair_products(s + 1, half)], axis=1))
        o_up = o_up + a3[:, :half] * ci3[:, s:s + 1, :] + a3[:, half:] * ci3[:, s + 1:s + 2, :]
    o_c = (o_all + jnp.concatenate([jnp.zeros((n_ch, half, D_C), F32), o_up], axis=1)).reshape(tq, D_C)

    lane_head = lax.broadcasted_iota(jnp.int32, (tq, D_C), 1) >> 6
    im_h = [jnp.where(lane_head == hh, ci, 0.0).astype(BF16) for hh in range(H_C)]
    qb_h = [qb[:, hh * DK_C:(hh + 1) * DK_C].astype(BF16) for hh in range(H_C)]
    kb_h = [kb[:, hh * DK_C:(hh + 1) * DK_C].astype(BF16) for hh in range(H_C)]
    eb_t = eb.T
    lane_head_c = lax.broadcasted_iota(jnp.int32, (HG_CHUNK, D_C), 1) >> 6
    st = st_ref[...]
    o_inter = []
    for n in range(n_ch):
        r0, r1 = n * HG_CHUNK, (n + 1) * HG_CHUNK
        read = _dot(jnp.concatenate([a[r0:r1] for a in qb_h], axis=0), st.astype(BF16))
        o_n = jnp.zeros((HG_CHUNK, D_C), F32)
        for hh in range(H_C):
            o_n = o_n + jnp.where(lane_head_c == hh, read[hh * HG_CHUNK:(hh + 1) * HG_CHUNK], 0.0)
        o_inter.append(o_n)
        upd = _tdot(jnp.concatenate([a[r0:r1] for a in kb_h], axis=0),
                    jnp.concatenate([a[r0:r1] for a in im_h], axis=0))
        col = eb_t[:, r1 - 1:r1]
        decay = jnp.concatenate(
            [jnp.broadcast_to(col[hh * DK_C:(hh + 1) * DK_C], (DK_C, DV_C)) for hh in range(H_C)], axis=1)
        st = decay * st + upd
    st_ref[...] = st
    o_c = o_c + jnp.concatenate(o_inter, axis=0)

    @pl.when(t_idx == n_t - 1)
    def _():
        for hh in range(H_C):
            st_out_ref[hh * DK_C:(hh + 1) * DK_C, :] = st[:, hh * DV_C:(hh + 1) * DV_C]

    ms = _head_sum(o_c * o_c, ones_bd) * (1.0 / DV_C)
    o_c = o_c * lax.rsqrt(ms + RMS_EPS) * ng_ref[...] * _silu(proj[:, OFF_CG:])

    mix = _dot(jnp.concatenate([o_a, o_b, o_c], axis=-1).astype(BF16), wout_ref[...])
    z = alpha * x + (1.0 + gate) * mix
    y_ref[...] = _layer_norm(z, lng_ref[...], lnb_ref[...])


def _mixer_prompt(x, mods, layer, w_in, w_out, tables, sinks, gln_g, gln_b, ws, bs_exp, lb, ng_exp,
                  ln_g, ln_b, *, batch, seq, alpha):
    m, d = x.shape
    tq = min(MIX_ROWS, seq)
    assert seq % tq == 0 and tq % WINDOW == 0
    n_t = seq // tq
    depth = w_in.shape[0]
    cos, s_lo, s_hi = tables
    mod_spec = lambda j: pl.BlockSpec((None, MOD_ROWS, d), lambda b, t: (layer, 0, j))
    tab_spec = pl.BlockSpec((tq, LANES), lambda b, t: (t, 0))
    lay3 = lambda shape: pl.BlockSpec((None,) + shape, lambda b, t: (layer,) + (0,) * len(shape))
    return pl.pallas_call(
        functools.partial(_mixer_prompt_kernel, layer=layer, tq=tq, alpha=alpha),
        grid=(batch, n_t),
        in_specs=[
            pl.BlockSpec((tq, d), lambda b, t: (b * n_t + t, 0)),
            mod_spec(3), mod_spec(4), mod_spec(5),
            lay3((d, D_IN)), lay3((d, d)),
            tab_spec, tab_spec, tab_spec,
            pl.BlockSpec(memory_space=pltpu.SMEM),
            lay3((1, D_B)), lay3((1, D_B)),
            lay3((G_B, CHUNK_B, CHUNK_B)), lay3((CHUNK_B, D_B)),
            pl.BlockSpec((depth, D_C), lambda b, t: (0, 0)),
            lay3((1, D_C)),
            pl.BlockSpec((None, 1, d), lambda b, t: (layer * 3 + 1, 0, 0)),
            pl.BlockSpec((None, 1, d), lambda b, t: (layer * 3 + 1, 0, 0)),
        ],
        out_specs=[
            pl.BlockSpec((tq, d), lambda b, t: (b * n_t + t, 0)),
            pl.BlockSpec((None, WINDOW, D_KV), lambda b, t: (b, 0, 0)),
            pl.BlockSpec((None, WINDOW, D_KV), lambda b, t: (b, 0, 0)),
            pl.BlockSpec((None, D_C, DV_C), lambda b, t: (b, 0, 0)),
        ],
        out_shape=[
            jax.ShapeDtypeStruct((m, d), F32),
            jax.ShapeDtypeStruct((batch, WINDOW, D_KV), F32),
            jax.ShapeDtypeStruct((batch, WINDOW, D_KV), F32),
            jax.ShapeDtypeStruct((batch, D_C, DV_C), F32),
        ],
        scratch_shapes=[
            pltpu.VMEM((WINDOW, D_KV), F32),
            pltpu.VMEM((WINDOW, D_KV), F32),
            pltpu.VMEM((DK_C, D_C), F32),
        ],
        compiler_params=pltpu.CompilerParams(
            dimension_semantics=("arbitrary", "arbitrary"), vmem_limit_bytes=VMEM_LIMIT_BYTES),
        name="mixer_prompt",
    )(x, mods, mods, mods, w_in, w_out, cos, s_lo, s_hi, sinks, gln_g, gln_b, ws, bs_exp, lb, ng_exp,
      ln_g, ln_b)


def _mixer_sample_kernel(
        x_ref, sh_ref, sc_ref, gt_ref, win_ref, wout_ref, cos_ref, sl_ref, sh2_ref, sink_ref,
        gg_ref, gb_ref, w0_ref, bs_ref, lb_ref, ng_ref, lng_ref, lnb_ref, ck_ref, cv_ref, s0_ref,
        y_ref, knew_ref, vnew_ref, gv_ref, s1_ref,
        qt_ref, kn_ref, vn_ref, sn_ref, sc_scr, ot_ref, ft_ref, kt_ref, qc_ref, it_ref, ob_ref, og_ref, oc_ref,
        *, layer, n_s, nb, alpha):
    step = pl.program_id(0)
    rows_all = n_s * H_A

    @pl.when(step == 0)
    def _():
        x = x_ref[...]
        h = (x * (1.0 + sc_ref[...]) + sh_ref[...]).astype(BF16)
        proj = _dot(h, win_ref[...])
        cos, s_lo, s_hi = cos_ref[...], sl_ref[...], sh2_ref[...]
        q = _rope(proj[:, OFF_Q:OFF_K], cos, s_lo, s_hi) * (HEAD_DIM ** -0.5)
        k = _rope(proj[:, OFF_K:OFF_V], cos, s_lo, s_hi)
        v = proj[:, OFF_V:OFF_U]
        knew_ref[...] = k
        vnew_ref[...] = v
        zeros64 = jnp.zeros((n_s, HEAD_DIM), F32)
        for hd in range(H_A):
            qh = q[:, hd * HEAD_DIM:(hd + 1) * HEAD_DIM]
            row = jnp.concatenate([qh, zeros64] if hd < G_A else [zeros64, qh], axis=-1)
            qt_ref[pl.ds(hd, n_s, stride=H_A), :] = row
            kn_ref[pl.ds(hd, n_s, stride=H_A), :] = k
            vn_ref[pl.ds(hd, n_s, stride=H_A), :] = v
        s_new = jnp.sum(qt_ref[...] * kn_ref[...], axis=-1, keepdims=True)
        sn_ref[...] = jnp.broadcast_to(s_new, (rows_all, LANES))

        vn = _layer_norm(proj[:, OFF_GV:OFF_CQ], gg_ref[...], gb_ref[...])
        gv_ref[...] = vn
        ob_ref[...] = proj[:, OFF_U:OFF_GV] * (vn * w0_ref[...] + bs_ref[0:1, :])

        lb = _hgrn_lower_bound(lb_ref, layer)
        f = lb + (1.0 - lb) * jax.nn.sigmoid(proj[:, OFF_CF:OFF_CI])
        ft_ref[...] = f.T
        kt_ref[...] = (1.0 - f).T
        qc_ref[...] = proj[:, OFF_CQ:OFF_CF].T
        it_ref[...] = proj[:, OFF_CI:OFF_CG].T
        og_ref[...] = ng_ref[...] * _silu(proj[:, OFF_CG:])

    base = pl.multiple_of(step * (nb * H_A), nb * H_A)
    for b in range(nb):
        qrows = qt_ref[pl.ds(base + b * H_A, H_A), :].astype(BF16)
        sc_scr[b * H_A:(b + 1) * H_A, :] = _dot(qrows, ck_ref[b].astype(BF16))
    lane = lax.broadcasted_iota(jnp.int32, (nb * H_A, WINDOW), 1)
    s = jnp.where(lane == 0, NEG_BIG, sc_scr[...])
    s_new = sn_ref[pl.ds(base, nb * H_A), 0:1]
    hrow = lax.broadcasted_iota(jnp.int32, (nb * H_A, 1), 0) & (H_A - 1)
    sink = jnp.zeros((nb * H_A, 1), F32)
    for hd in range(H_A):
        sink = jnp.where(hrow == hd, sink_ref[layer, hd], sink)
    mx = jnp.maximum(jnp.maximum(jnp.max(s, axis=-1, keepdims=True), s_new), sink)
    p = jnp.exp(s - mx)
    p_new = jnp.exp(s_new - mx)
    den = jnp.sum(p, axis=-1, keepdims=True) + p_new + jnp.exp(sink - mx)
    sc_scr[...] = p
    for b in range(nb):
        prow = sc_scr[b * H_A:(b + 1) * H_A, :].astype(BF16)
        ot_ref[pl.ds(base + b * H_A, H_A), :] = _dot_t(prow, cv_ref[b].astype(BF16))
    rows = pl.ds(base, nb * H_A)
    ot_ref[rows, :] = (ot_ref[rows, :] + p_new * vn_ref[rows, :]) * (1.0 / den)

    hrow0 = pl.multiple_of(step * DK_C, DK_C)
    i_h = it_ref[pl.ds(hrow0, DV_C), :]
    acc = jnp.zeros((DV_C, n_s), F32)
    for kk in range(DK_C):
        f_k = ft_ref[pl.ds(hrow0 + kk, 1), :]
        k_k = kt_ref[pl.ds(hrow0 + kk, 1), :]
        q_k = qc_ref[pl.ds(hrow0 + kk, 1), :]
        s1 = f_k * s0_ref[kk] + k_k * i_h
        s1_ref[kk] = s1
        acc = acc + q_k * s1
    oc_ref[pl.ds(hrow0, DV_C), :] = acc

    @pl.when(step == pl.num_programs(0) - 1)
    def _():
        o_a = []
        for hd in range(H_A):
            r = ot_ref[pl.ds(hd, n_s, stride=H_A), :]
            o_a.append(r[:, :HEAD_DIM] if hd < G_A else r[:, HEAD_DIM:])
        o_c = oc_ref[...].T
        ms = _head_sum(o_c * o_c, _head_block_ones()) * (1.0 / DV_C)
        o_c = o_c * lax.rsqrt(ms + RMS_EPS) * og_ref[...]
        mix = _dot(jnp.concatenate(o_a + [ob_ref[...], o_c], axis=-1).astype(BF16), wout_ref[...])
        z = alpha * x_ref[...] + (1.0 + gt_ref[...]) * mix
        y_ref[...] = _layer_norm(z, lng_ref[...], lnb_ref[...])


def _mixer_sample(x, mods, layer, w_in, w_out, tables, sinks, gln_g, gln_b, w0_exp, bs_exp, lb, ng_exp,
                  ln_g, ln_b, cache_k, cache_v, state, *, alpha):
    n_s, d = x.shape
    assert n_s % (H_C * SUBLANES) == 0 and n_s % LANES == 0
    nb = n_s // H_C
    depth = w_in.shape[0]
    cos, s_lo, s_hi = tables
    mod_spec = lambda j: pl.BlockSpec((None, n_s, d), lambda i: (layer, 0, j))
    tab_spec = pl.BlockSpec((1, LANES), lambda i: (0, 0))
    lay3 = lambda shape: pl.BlockSpec((None,) + shape, lambda i: (layer,) + (0,) * len(shape))
    rows = pl.BlockSpec((n_s, d), lambda i: (0, 0))
    per_head = pltpu.VMEM((D_C, n_s), F32)
    per_row = pltpu.VMEM((n_s * H_A, LANES), F32)
    return pl.pallas_call(
        functools.partial(_mixer_sample_kernel, layer=layer, n_s=n_s, nb=nb, alpha=alpha),
        grid=(H_C,),
        in_specs=[
            rows,
            mod_spec(3), mod_spec(4), mod_spec(5),
            lay3((d, D_IN)), lay3((d, d)),
            tab_spec, tab_spec, tab_spec,
            pl.BlockSpec(memory_space=pltpu.SMEM),
            lay3((1, D_B)), lay3((1, D_B)), lay3((1, D_B)), lay3((CHUNK_B, D_B)),
            pl.BlockSpec((depth, D_C), lambda i: (0, 0)),
            lay3((1, D_C)),
            pl.BlockSpec((None, 1, d), lambda i: (layer * 3 + 1, 0, 0)),
            pl.BlockSpec((None, 1, d), lambda i: (layer * 3 + 1, 0, 0)),
            pl.BlockSpec((None, nb, D_KV, WINDOW), lambda i: (layer, i, 0, 0)),
            pl.BlockSpec((None, nb, D_KV, WINDOW), lambda i: (layer, i, 0, 0)),
            pl.BlockSpec((None, None, DK_C, DV_C, n_s), lambda i: (layer, i, 0, 0, 0)),
        ],
        out_specs=[
            rows,
            pl.BlockSpec((n_s, D_KV), lambda i: (0, 0)),
            pl.BlockSpec((n_s, D_KV), lambda i: (0, 0)),
            pl.BlockSpec((n_s, D_B), lambda i: (0, 0)),
            pl.BlockSpec((None, DK_C, DV_C, n_s), lambda i: (i, 0, 0, 0)),
        ],
        out_shape=[
            jax.ShapeDtypeStruct((n_s, d), F32),
            jax.ShapeDtypeStruct((n_s, D_KV), F32),
            jax.ShapeDtypeStruct((n_s, D_KV), F32),
            jax.ShapeDtypeStruct((n_s, D_B), F32),
            jax.ShapeDtypeStruct((H_C, DK_C, DV_C, n_s), F32),
        ],
        scratch_shapes=[
            per_row,
            per_row,
            per_row,
            per_row,
            pltpu.VMEM((nb * H_A, WINDOW), F32),
            per_row,
            per_head, per_head, per_head, per_head,
            pltpu.VMEM((n_s, D_B), F32),
            pltpu.VMEM((n_s, D_C), F32),
            per_head,
        ],
        compiler_params=pltpu.CompilerParams(
            dimension_semantics=("arbitrary",), vmem_limit_bytes=VMEM_LIMIT_BYTES),
        name="mixer_sample",
    )(x, mods, mods, mods, w_in, w_out, cos, s_lo, s_hi, sinks, gln_g, gln_b, w0_exp, bs_exp, lb, ng_exp,
      ln_g, ln_b, cache_k, cache_v, state)


def _rope_tables(positions):
    half = ROT_DIM // 2
    f32 = np.float32
    inv = (ROPE_THETA ** (-np.arange(half, dtype=f32) * 2.0 / ROT_DIM)).astype(f32)
    ang = (positions.astype(f32)[:, None] * inv[None, :]).astype(f32)
    cos, sin = np.cos(ang).astype(f32), np.sin(ang).astype(f32)
    n = positions.shape[0]
    rest = HEAD_DIM - ROT_DIM
    cos_h = np.concatenate([cos, cos, np.ones((n, rest), f32)], axis=-1)
    lo_h = np.concatenate([-sin, np.zeros((n, half + rest), f32)], axis=-1)
    hi_h = np.concatenate([np.zeros((n, half), f32), sin, np.zeros((n, rest), f32)], axis=-1)
    two = lambda a: jnp.asarray(np.concatenate([a, a], axis=-1))
    return two(cos_h), two(lo_h), two(hi_h)


def kernel(x_prompt, x_sample, cache_k, cache_v, state_hgrn, c_prompt, c_sample, w_in, w_out, attn_sinks,
           gmlp_ln_g, gmlp_ln_b, gmlp_ws, gmlp_bs, hgrn_lb, hgrn_norm_g, ffn1_in, ffn1_out, ffn2_in,
           ffn2_out, ada_w, ada_b, ln_g, ln_b):
    batch, seq, d = x_prompt.shape
    n_s, dec_seq, _ = x_sample.shape
    depth = w_in.shape[0]
    assert d == D_MODEL and dec_seq == 1 and batch <= MOD_ROWS
    assert cache_k.shape[2] == WINDOW
    alpha = (2 * depth) ** 0.25

    w_in_b, w_out_b = w_in.astype(BF16), w_out.astype(BF16)
    ln_g3 = ln_g.reshape(depth * 3, 1, d)
    ln_b3 = ln_b.reshape(depth * 3, 1, d)
    gln_g = gmlp_ln_g.reshape(depth, 1, D_B)
    gln_b = gmlp_ln_b.reshape(depth, 1, D_B)
    bs_exp = jnp.repeat(jnp.swapaxes(gmlp_bs, 1, 2), HEAD_DIM, axis=2)
    w0_exp = jnp.repeat(gmlp_ws[:, :, 0, 0], HEAD_DIM, axis=1).reshape(depth, 1, D_B)
    ng_exp = jnp.tile(hgrn_norm_g, (1, H_C)).reshape(depth, 1, D_C)
    tab_p = _rope_tables(np.arange(seq))
    tab_s = _rope_tables(PAST_LEN + np.arange(dec_seq))

    c_all = jnp.concatenate([c_prompt, jnp.zeros((MOD_ROWS - batch, d), F32), c_sample], axis=0)
    mod_p, mod_s = _adaln(c_all, ada_w, ada_b)

    ck = jnp.transpose(cache_k, (0, 1, 3, 4, 2)).reshape(depth, n_s, D_KV, WINDOW)
    cv = jnp.transpose(cache_v, (0, 1, 3, 4, 2)).reshape(depth, n_s, D_KV, WINDOW)
    s0 = jnp.transpose(state_hgrn, (0, 2, 3, 4, 1))

    xp = x_prompt.reshape(batch * seq, d)
    xs = x_sample.reshape(n_s, d)
    ffn = functools.partial(_ffn, rows_per_batch=seq, alpha=alpha)
    kw, vw, st_p, kn, vnw, gvs, st_s = [], [], [], [], [], [], []
    for l in range(depth):
        xp, xs = ffn(xp, xs, mod_p, mod_s, l, 0, ffn1_in, ffn1_out, ln_g3, ln_b3)
        xp, k_l, v_l, s_l = _mixer_prompt(
            xp, mod_p, l, w_in_b, w_out_b, tab_p, attn_sinks, gln_g, gln_b, gmlp_ws, bs_exp, hgrn_lb,
            ng_exp, ln_g3, ln_b3, batch=batch, seq=seq, alpha=alpha)
        kw.append(k_l), vw.append(v_l), st_p.append(s_l)
        xs, k_l, v_l, g_l, s_l = _mixer_sample(
            xs, mod_s, l, w_in_b, w_out_b, tab_s, attn_sinks, gln_g, gln_b, w0_exp, bs_exp, hgrn_lb,
            ng_exp, ln_g3, ln_b3, ck, cv, s0, alpha=alpha)
        kn.append(k_l), vnw.append(v_l), gvs.append(g_l), st_s.append(s_l)
        xp, xs = ffn(xp, xs, mod_p, mod_s, l, 2, ffn2_in, ffn2_out, ln_g3, ln_b3)

    stack = lambda parts, shape: jnp.stack(parts, axis=0).reshape((depth,) + shape)
    return (
        xp.reshape(batch, seq, d),
        xs.reshape(n_s, dec_seq, d),
        stack(kw, (batch, WINDOW, KV_A, HEAD_DIM)),
        stack(vw, (batch, WINDOW, KV_A, HEAD_DIM)),
        stack(st_p, (batch, H_C, DK_C, DV_C)),
        stack(kn, (n_s, dec_seq, KV_A, HEAD_DIM)),
        stack(vnw, (n_s, dec_seq, KV_A, HEAD_DIM)),
        stack(gvs, (n_s, dec_seq, D_B)),
        jnp.transpose(jnp.stack(st_s, axis=0), (0, 4, 1, 2, 3)),
    )
```

```python
import functools

import numpy as np
import jax
import jax.numpy as jnp
from jax import lax
from jax.experimental import pallas as pl
from jax.experimental.pallas import tpu as pltpu

F32 = jnp.float32
BF16 = jnp.bfloat16

D_MODEL = 1024
HEAD_DIM = 64
H_A = 8
KV_A = 2
G_A = H_A // KV_A
D_A = H_A * HEAD_DIM
D_KV = KV_A * HEAD_DIM
WINDOW = 128
ROT_DIM = HEAD_DIM // 4
ROPE_THETA = 500000.0
G_B = 4
D_B = G_B * HEAD_DIM
CHUNK_B = 128
H_C = 4
DK_C = 64
DV_C = 64
D_C = H_C * DV_C
D_IN = 2304
D_FF = 2816
N_MOD = 9
LN_EPS = 1e-5
RMS_EPS = 1e-6
NEG_BIG = -1e30
LOG2_E = 1.4426950408889634
PAST_LEN = 16384

OFF_Q = 0
OFF_K = OFF_Q + D_A
OFF_V = OFF_K + D_KV
OFF_U = OFF_V + D_KV
OFF_GV = OFF_U + D_B
OFF_CQ = OFF_GV + D_B
OFF_CF = OFF_CQ + D_C
OFF_CI = OFF_CF + D_C
OFF_CG = OFF_CI + D_C

SUBLANES = 8
LANES = 128
MXU_DIM = 256
VMEM_LIMIT_BYTES = 56 * 1024 * 1024

FFN_ROWS = 1024
FF_CHUNK = MXU_DIM
FF_CONV_STEPS = 11
FF_CONV_COLS = 2 * D_FF // FF_CONV_STEPS
FF_CONV_ROWS = D_FF // FF_CONV_STEPS
ADA_COLS = 2304
MIX_ROWS = 512
HG_CHUNK = 16
MOD_ROWS = SUBLANES


def _dot(a, b):
    return jnp.dot(a, b, preferred_element_type=F32)


def _dot_t(a, b):
    return lax.dot_general(a, b, (((1,), (1,)), ((), ())), preferred_element_type=F32)


def _tdot(a, b):
    return lax.dot_general(a, b, (((0,), (0,)), ((), ())), preferred_element_type=F32)


def _layer_norm(z, g, b):
    mu = jnp.mean(z, axis=-1, keepdims=True)
    zc = z - mu
    var = jnp.mean(zc * zc, axis=-1, keepdims=True)
    return zc * lax.rsqrt(var + LN_EPS) * g + b


def _silu(x):
    return x * jax.nn.sigmoid(x)


def _head_block_ones():
    r = lax.broadcasted_iota(jnp.int32, (D_C, D_C), 0) >> 6
    c = lax.broadcasted_iota(jnp.int32, (D_C, D_C), 1) >> 6
    return jnp.where(r == c, 1.0, 0.0).astype(BF16)


def _head_sum(x, ones_bd):
    hi = x.astype(BF16)
    lo = (x - hi.astype(F32)).astype(BF16)
    return _dot(hi, ones_bd) + _dot(lo, ones_bd)


def _rope(x, cos, sin_lo, sin_hi):
    half = ROT_DIM // 2
    out = []
    for c in range(x.shape[-1] // LANES):
        xc = x[:, c * LANES:(c + 1) * LANES]
        out.append(xc * cos + pltpu.roll(xc, LANES - half, 1) * sin_lo + pltpu.roll(xc, half, 1) * sin_hi)
    return out[0] if len(out) == 1 else jnp.concatenate(out, axis=-1)


def _hgrn_lower_bound(lb_ref, layer):
    w = lb_ref[...]
    e = jnp.exp(w - jnp.max(w, axis=0, keepdims=True))
    p = e / jnp.sum(e, axis=0, keepdims=True)
    lb = jnp.zeros((1, D_C), F32)
    for j in range(1, layer + 1):
        lb = lb + p[j:j + 1, :]
    return lb


def _adaln_kernel(c_ref, w_ref, b_ref, op_ref, os_ref):
    s = _silu(c_ref[...]).astype(BF16)
    r = _dot(s, w_ref[...].astype(BF16)) + b_ref[...]
    op_ref[...] = r[:MOD_ROWS]
    os_ref[...] = r[MOD_ROWS:]


def _adaln(c_all, ada_w, ada_b):
    depth, d, n = ada_w.shape
    rows = c_all.shape[0]
    n_s = rows - MOD_ROWS
    return pl.pallas_call(
        _adaln_kernel,
        grid=(depth, n // ADA_COLS),
        in_specs=[
            pl.BlockSpec((rows, d), lambda l, j: (0, 0)),
            pl.BlockSpec((None, d, ADA_COLS), lambda l, j: (l, 0, j)),
            pl.BlockSpec((None, 1, ADA_COLS), lambda l, j: (l, 0, j)),
        ],
        out_specs=[
            pl.BlockSpec((None, MOD_ROWS, ADA_COLS), lambda l, j: (l, 0, j)),
            pl.BlockSpec((None, n_s, ADA_COLS), lambda l, j: (l, 0, j)),
        ],
        out_shape=[
            jax.ShapeDtypeStruct((depth, MOD_ROWS, n), F32),
            jax.ShapeDtypeStruct((depth, n_s, n), F32),
        ],
        compiler_params=pltpu.CompilerParams(
            dimension_semantics=("arbitrary", "arbitrary"), vmem_limit_bytes=VMEM_LIMIT_BYTES),
        name="adaln",
    )(c_all, ada_w, ada_b.reshape(depth, 1, n))


def _ffn_rows(x, shift, scale, gate, wi_ref, wo_ref, act_ref, lng, lnb, alpha):
    rows = x.shape[0]
    h = (x * (1.0 + scale) + shift).astype(BF16)
    for c in range(D_FF // FF_CHUNK):
        lo = c * FF_CHUNK
        ga = D_FF + lo
        a = _dot(h, wi_ref[lo // FF_CONV_COLS, :, lo % FF_CONV_COLS:lo % FF_CONV_COLS + FF_CHUNK])
        g = _dot(h, wi_ref[ga // FF_CONV_COLS, :, ga % FF_CONV_COLS:ga % FF_CONV_COLS + FF_CHUNK])
        act_ref[0:rows, lo:lo + FF_CHUNK] = (_silu(g) * a).astype(BF16)
    y = _dot(act_ref[0:rows, :], wo_ref[...])
    z = alpha * x + (1.0 + gate) * (0.5 * y)
    return _layer_norm(z, lng, lnb)


def _ffn_kernel(xp_ref, xs_ref, shp_ref, scp_ref, gtp_ref, shs_ref, scs_ref, gts_ref, win_ref, wout_ref,
                lng_ref, lnb_ref, op_ref, os_ref, wi_ref, wo_ref, act_ref, *, n_tiles, tiles_per_batch, alpha):
    step = pl.program_id(0)

    @pl.when(step < FF_CONV_STEPS)
    def _():
        wi_ref[step] = win_ref[...].astype(BF16)
        r0 = pl.multiple_of(step * FF_CONV_ROWS, FF_CONV_ROWS)
        wo_ref[pl.ds(r0, FF_CONV_ROWS), :] = wout_ref[...].astype(BF16)

    @pl.when(jnp.logical_and(step >= FF_CONV_STEPS, step < FF_CONV_STEPS + n_tiles))
    def _():
        batch = (step - FF_CONV_STEPS) // tiles_per_batch
        op_ref[...] = _ffn_rows(
            xp_ref[...], shp_ref[pl.ds(batch, 1), :], scp_ref[pl.ds(batch, 1), :], gtp_ref[pl.ds(batch, 1), :],
            wi_ref, wo_ref, act_ref, lng_ref[...], lnb_ref[...], alpha)

    @pl.when(step == FF_CONV_STEPS + n_tiles)
    def _():
        os_ref[...] = _ffn_rows(
            xs_ref[...], shs_ref[...], scs_ref[...], gts_ref[...],
            wi_ref, wo_ref, act_ref, lng_ref[...], lnb_ref[...], alpha)


def _ffn(xp, xs, mod_p, mod_s, layer, sub, w_in, w_out, ln_g, ln_b, *, rows_per_batch, alpha):
    m, d = xp.shape
    n_s = xs.shape[0]
    tm = min(FFN_ROWS, rows_per_batch)
    assert m % tm == 0 and rows_per_batch % tm == 0 and n_s <= tm
    n_tiles = m // tm
    last_conv = FF_CONV_STEPS - 1
    tile = lambda i: jnp.clip(i - FF_CONV_STEPS, 0, n_tiles - 1)
    mod_p_spec = lambda j: pl.BlockSpec((None, MOD_ROWS, d), lambda i: (layer, 0, j))
    mod_s_spec = lambda j: pl.BlockSpec((None, n_s, d), lambda i: (layer, 0, j))
    return pl.pallas_call(
        functools.partial(_ffn_kernel, n_tiles=n_tiles, tiles_per_batch=rows_per_batch // tm, alpha=alpha),
        grid=(FF_CONV_STEPS + n_tiles + 1,),
        in_specs=[
            pl.BlockSpec((tm, d), lambda i: (tile(i), 0)),
            pl.BlockSpec((n_s, d), lambda i: (0, 0)),
            mod_p_spec(3 * sub), mod_p_spec(3 * sub + 1), mod_p_spec(3 * sub + 2),
            mod_s_spec(3 * sub), mod_s_spec(3 * sub + 1), mod_s_spec(3 * sub + 2),
            pl.BlockSpec((None, d, FF_CONV_COLS), lambda i: (layer, 0, jnp.minimum(i, last_conv))),
            pl.BlockSpec((None, FF_CONV_ROWS, d), lambda i: (layer, jnp.minimum(i, last_conv), 0)),
            pl.BlockSpec((None, 1, d), lambda i: (layer * 3 + sub, 0, 0)),
            pl.BlockSpec((None, 1, d), lambda i: (layer * 3 + sub, 0, 0)),
        ],
        out_specs=[
            pl.BlockSpec((tm, d), lambda i: (tile(i), 0)),
            pl.BlockSpec((n_s, d), lambda i: (0, 0)),
        ],
        out_shape=[jax.ShapeDtypeStruct((m, d), F32), jax.ShapeDtypeStruct((n_s, d), F32)],
        scratch_shapes=[
            pltpu.VMEM((FF_CONV_STEPS, d, FF_CONV_COLS), BF16),
            pltpu.VMEM((D_FF, d), BF16),
            pltpu.VMEM((tm, D_FF), BF16),
        ],
        compiler_params=pltpu.CompilerParams(
            dimension_semantics=("arbitrary",), vmem_limit_bytes=VMEM_LIMIT_BYTES),
        name="ffn",
    )(xp, xs, mod_p, mod_p, mod_p, mod_s, mod_s, mod_s, w_in, w_out, ln_g, ln_b)


def _attention_block(q_blk, keys, vals, mask, fill):
    outs = []
    for j in range(KV_A):
        kj = keys[:, j * HEAD_DIM:(j + 1) * HEAD_DIM].astype(BF16)
        vj = vals[:, j * HEAD_DIM:(j + 1) * HEAD_DIM].astype(BF16)
        qj = jnp.concatenate(
            [q_blk[:, (j * G_A + g) * HEAD_DIM:(j * G_A + g + 1) * HEAD_DIM] for g in range(G_A)],
            axis=0).astype(BF16)
        s = _dot_t(qj, kj)
        s = jnp.concatenate(
            [jnp.where(mask, s[g * WINDOW:(g + 1) * WINDOW], fill[j][g]) for g in range(G_A)], axis=0)
        p = jnp.exp2(s - jnp.max(s, axis=-1, keepdims=True))
        den = jnp.sum(p, axis=-1, keepdims=True)
        o = _dot(p.astype(BF16), vj) * (1.0 / den)
        outs.extend(o[g * WINDOW:(g + 1) * WINDOW] for g in range(G_A))
    return jnp.concatenate(outs, axis=-1)


def _mixer_prompt_kernel(
        x_ref, sh_ref, sc_ref, gt_ref, win_ref, wout_ref, cos_ref, sl_ref, sh2_ref, sink_ref,
        gg_ref, gb_ref, ws_ref, bs_ref, lb_ref, ng_ref, lng_ref, lnb_ref,
        y_ref, kwin_ref, vwin_ref, st_out_ref,
        kprev_ref, vprev_ref, st_ref,
        *, layer, tq, alpha):
    b_idx = pl.program_id(0)
    t_idx = pl.program_id(1)
    n_t = pl.num_programs(1)

    @pl.when(t_idx == 0)
    def _():
        kprev_ref[...] = jnp.zeros_like(kprev_ref)
        vprev_ref[...] = jnp.zeros_like(vprev_ref)
        st_ref[...] = jnp.zeros_like(st_ref)

    x = x_ref[...]
    shift = sh_ref[pl.ds(b_idx, 1), :]
    scale = sc_ref[pl.ds(b_idx, 1), :]
    gate = gt_ref[pl.ds(b_idx, 1), :]
    h = (x * (1.0 + scale) + shift).astype(BF16)
    proj = _dot(h, win_ref[...])

    cos, s_lo, s_hi = cos_ref[...], sl_ref[...], sh2_ref[...]
    q = _rope(proj[:, OFF_Q:OFF_K], cos, s_lo, s_hi) * (HEAD_DIM ** -0.5 * LOG2_E)
    k = _rope(proj[:, OFF_K:OFF_V], cos, s_lo, s_hi)
    v = proj[:, OFF_V:OFF_U]

    qi = lax.broadcasted_iota(jnp.int32, (WINDOW, 2 * WINDOW), 0)
    kj = lax.broadcasted_iota(jnp.int32, (WINDOW, 2 * WINDOW), 1)
    dist = kj - qi
    band = (dist - 1).astype(jnp.uint32) < jnp.uint32(WINDOW)
    col0 = lax.broadcasted_iota(jnp.int32, (1, 2 * WINDOW), 1) == 0
    fill = [[jnp.where(col0, sink_ref[layer, j * G_A + g] * LOG2_E, NEG_BIG) for g in range(G_A)]
            for j in range(KV_A)]
    first_row = lax.broadcasted_iota(jnp.int32, (WINDOW, D_KV), 0) == 0
    o_a = []
    for blk in range(tq // WINDOW):
        r0 = blk * WINDOW
        if blk == 0:
            k_prev, v_prev = kprev_ref[...], vprev_ref[...]
            lower = jnp.where(t_idx == 0, WINDOW, 0)
            mask = jnp.logical_and(band, kj >= lower)
        else:
            k_prev, v_prev = k[r0 - WINDOW:r0], v[r0 - WINDOW:r0]
            mask = band
        keys = jnp.concatenate([k_prev, k[r0:r0 + WINDOW]], axis=0)
        vals = jnp.concatenate([jnp.where(first_row, 0.0, v_prev), v[r0:r0 + WINDOW]], axis=0)
        o_a.append(_attention_block(q[r0:r0 + WINDOW], keys, vals, mask, fill))
    o_a = jnp.concatenate(o_a, axis=0)
    kprev_ref[...] = k[tq - WINDOW:]
    vprev_ref[...] = v[tq - WINDOW:]
    kwin_ref[...] = k[tq - WINDOW:]
    vwin_ref[...] = v[tq - WINDOW:]

    vn = _layer_norm(proj[:, OFF_GV:OFF_CQ], gg_ref[...], gb_ref[...])
    u = proj[:, OFF_U:OFF_GV]
    tr = lax.broadcasted_iota(jnp.int32, (CHUNK_B, CHUNK_B), 0)
    tc = lax.broadcasted_iota(jnp.int32, (CHUNK_B, CHUNK_B), 1)
    wm = [jnp.where(tr >= tc, ws_ref[g], 0.0).astype(BF16) for g in range(G_B)]
    o_b = []
    for c in range(tq // CHUNK_B):
        vc = vn[c * CHUNK_B:(c + 1) * CHUNK_B].astype(BF16)
        mixed = jnp.concatenate(
            [_dot(wm[g], vc[:, g * HEAD_DIM:(g + 1) * HEAD_DIM]) for g in range(G_B)], axis=-1)
        o_b.append(u[c * CHUNK_B:(c + 1) * CHUNK_B] * (mixed + bs_ref[...]))
    o_b = jnp.concatenate(o_b, axis=0)

    n_ch = tq // HG_CHUNK
    half = HG_CHUNK // 2
    cq = proj[:, OFF_CQ:OFF_CF]
    ci = proj[:, OFF_CI:OFF_CG]
    lb = _hgrn_lower_bound(lb_ref, layer)
    f = lb + (1.0 - lb) * jax.nn.sigmoid(proj[:, OFF_CF:OFF_CI])
    lf = jnp.log(f)
    kc = 1.0 - f

    sr = min(tq, MXU_DIM)
    rt = lax.broadcasted_iota(jnp.int32, (sr, sr), 0)
    ct = lax.broadcasted_iota(jnp.int32, (sr, sr), 1)
    ltri = jnp.where(ct <= rt, jnp.where(ct >= (rt & -HG_CHUNK), 1.0, 0.0), 0.0).astype(BF16)
    lf_hi = lf.astype(BF16)
    lf_r = lf - lf_hi.astype(F32)
    lf_mid = lf_r.astype(BF16)
    lf_lo = (lf_r - lf_mid.astype(F32)).astype(BF16)
    bsum = jnp.concatenate(
        [_dot(ltri, lf_hi[r:r + sr]) + _dot(ltri, lf_mid[r:r + sr]) + _dot(ltri, lf_lo[r:r + sr])
         for r in range(0, tq, sr)], axis=0) * LOG2_E

    def chunks(a):
        return a.reshape(n_ch, HG_CHUNK, D_C)

    b3, cq3, ci3 = chunks(bsum), chunks(cq), chunks(ci)
    b_end = b3[:, HG_CHUNK - 1:HG_CHUNK, :]
    eb = jnp.exp2(bsum)
    qb = cq * eb
    kb = kc * jnp.exp2(b_end - b3).reshape(tq, D_C)

    ones_bd = _head_block_ones()
    c3 = chunks(bsum - jnp.log2(jnp.maximum(kc, 0.0)))

    def pair_products(s, lo):
        e = cq3[:, lo:] * jnp.exp2(b3[:, lo:] - c3[:, s:s + 1, :])
        pos = lax.broadcasted_iota(jnp.int32, (n_ch, half, D_C), 1) + (s // half) * half
        grp = s // half - lo // half
        parts = [e[:, g * half:(g + 1) * half] for g in range((HG_CHUNK - lo) // half)]
        parts[grp] = jnp.where(pos >= s, parts[grp], 0.0)
        return parts[0] if len(parts) == 1 else jnp.concatenate(parts, axis=1)

    def head_sums(slab):
        return chunks(_dot(slab.reshape(tq, D_C).astype(BF16), ones_bd))

    o_all = jnp.zeros((n_ch, HG_CHUNK, D_C), F32)
    for s in range(half):
        o_all = o_all + head_sums(pair_products(s, 0)) * ci3[:, s:s + 1, :]
    o_up = jnp.zeros((n_ch, half, D_C), F32)
    for s in range(half, HG_CHUNK, 2):
        a3 = head_sums(jnp.concatenate([pair_products(s, half), pair_products(s + 1, half)], axis=1))
        o_up = o_up + a3[:, :half] * ci3[:, s:s + 1, :] + a3[:, half:] * ci3[:, s + 1:s + 2, :]
    o_c = (o_all + jnp.concatenate([jnp.zeros((n_ch, half, D_C), F32), o_up], axis=1)).reshape(tq, D_C)

    lane_head = lax.broadcasted_iota(jnp.int32, (tq, D_C), 1) >> 6
    im_h = [jnp.where(lane_head == hh, ci, 0.0).astype(BF16) for hh in range(H_C)]
    qb_h = [qb[:, hh * DK_C:(hh + 1) * DK_C].astype(BF16) for hh in range(H_C)]
    kb_h = [kb[:, hh * DK_C:(hh + 1) * DK_C].astype(BF16) for hh in range(H_C)]
    eb_t = eb.T
    lane_head_c = lax.broadcasted_iota(jnp.int32, (HG_CHUNK, D_C), 1) >> 6
    st = st_ref[...]
    o_inter = []
    for n in range(n_ch):
        r0, r1 = n * HG_CHUNK, (n + 1) * HG_CHUNK
        read = _dot(jnp.concatenate([a[r0:r1] for a in qb_h], axis=0), st.astype(BF16))
        o_n = jnp.zeros((HG_CHUNK, D_C), F32)
        for hh in range(H_C):
            o_n = o_n + jnp.where(lane_head_c == hh, read[hh * HG_CHUNK:(hh + 1) * HG_CHUNK], 0.0)
        o_inter.append(o_n)
        upd = _tdot(jnp.concatenate([a[r0:r1] for a in kb_h], axis=0),
                    jnp.concatenate([a[r0:r1] for a in im_h], axis=0))
        col = eb_t[:, r1 - 1:r1]
        decay = jnp.concatenate(
            [jnp.broadcast_to(col[hh * DK_C:(hh + 1) * DK_C], (DK_C, DV_C)) for hh in range(H_C)], axis=1)
        st = decay * st + upd
    st_ref[...] = st
    o_c = o_c + jnp.concatenate(o_inter, axis=0)

    @pl.when(t_idx == n_t - 1)
    def _():
        for hh in range(H_C):
            st_out_ref[hh * DK_C:(hh + 1) * DK_C, :] = st[:, hh * DV_C:(hh + 1) * DV_C]

    ms = _head_sum(o_c * o_c, ones_bd) * (1.0 / DV_C)
    o_c = o_c * lax.rsqrt(ms + RMS_EPS) * ng_ref[...] * _silu(proj[:, OFF_CG:])

    mix = _dot(jnp.concatenate([o_a, o_b, o_c], axis=-1).astype(BF16), wout_ref[...])
    z = alpha * x + (1.0 + gate) * mix
    y_ref[...] = _layer_norm(z, lng_ref[...], lnb_ref[...])


def _mixer_prompt(x, mods, layer, w_in, w_out, tables, sinks, gln_g, gln_b, ws, bs_exp, lb, ng_exp,
                  ln_g, ln_b, *, batch, seq, alpha):
    m, d = x.shape
    tq = min(MIX_ROWS, seq)
    assert seq % tq == 0 and tq % WINDOW == 0
    n_t = seq // tq
    depth = w_in.shape[0]
    cos, s_lo, s_hi = tables
    mod_spec = lambda j: pl.BlockSpec((None, MOD_ROWS, d), lambda b, t: (layer, 0, j))
    tab_spec = pl.BlockSpec((tq, LANES), lambda b, t: (t, 0))
    lay3 = lambda shape: pl.BlockSpec((None,) + shape, lambda b, t: (layer,) + (0,) * len(shape))
    return pl.pallas_call(
        functools.partial(_mixer_prompt_kernel, layer=layer, tq=tq, alpha=alpha),
        grid=(batch, n_t),
        in_specs=[
            pl.BlockSpec((tq, d), lambda b, t: (b * n_t + t, 0)),
            mod_spec(3), mod_spec(4), mod_spec(5),
            lay3((d, D_IN)), lay3((d, d)),
            tab_spec, tab_spec, tab_spec,
            pl.BlockSpec(memory_space=pltpu.SMEM),
            lay3((1, D_B)), lay3((1, D_B)),
            lay3((G_B, CHUNK_B, CHUNK_B)), lay3((CHUNK_B, D_B)),
            pl.BlockSpec((depth, D_C), lambda b, t: (0, 0)),
            lay3((1, D_C)),
            pl.BlockSpec((None, 1, d), lambda b, t: (layer * 3 + 1, 0, 0)),
            pl.BlockSpec((None, 1, d), lambda b, t: (layer * 3 + 1, 0, 0)),
        ],
        out_specs=[
            pl.BlockSpec((tq, d), lambda b, t: (b * n_t + t, 0)),
            pl.BlockSpec((None, WINDOW, D_KV), lambda b, t: (b, 0, 0)),
            pl.BlockSpec((None, WINDOW, D_KV), lambda b, t: (b, 0, 0)),
            pl.BlockSpec((None, D_C, DV_C), lambda b, t: (b, 0, 0)),
        ],
        out_shape=[
            jax.ShapeDtypeStruct((m, d), F32),
            jax.ShapeDtypeStruct((batch, WINDOW, D_KV), F32),
            jax.ShapeDtypeStruct((batch, WINDOW, D_KV), F32),
            jax.ShapeDtypeStruct((batch, D_C, DV_C), F32),
        ],
        scratch_shapes=[
            pltpu.VMEM((WINDOW, D_KV), F32),
            pltpu.VMEM((WINDOW, D_KV), F32),
            pltpu.VMEM((DK_C, D_C), F32),
        ],
        compiler_params=pltpu.CompilerParams(
            dimension_semantics=("arbitrary", "arbitrary"), vmem_limit_bytes=VMEM_LIMIT_BYTES),
        name="mixer_prompt",
    )(x, mods, mods, mods, w_in, w_out, cos, s_lo, s_hi, sinks, gln_g, gln_b, ws, bs_exp, lb, ng_exp,
      ln_g, ln_b)


def _mixer_sample_kernel(
        x_ref, sh_ref, sc_ref, gt_ref, win_ref, wout_ref, cos_ref, sl_ref, sh2_ref, sink_ref,
        gg_ref, gb_ref, w0_ref, bs_ref, lb_ref, ng_ref, lng_ref, lnb_ref, ck_ref, cv_ref, s0_ref,
        y_ref, knew_ref, vnew_ref, gv_ref, s1_ref,
        qt_ref, kn_ref, vn_ref, sn_ref, sc_scr, ot_ref, ft_ref, kt_ref, qc_ref, it_ref, ob_ref, og_ref, oc_ref,
        *, layer, n_s, nb, alpha):
    step = pl.program_id(0)
    rows_all = n_s * H_A

    @pl.when(step == 0)
    def _():
        x = x_ref[...]
        h = (x * (1.0 + sc_ref[...]) + sh_ref[...]).astype(BF16)
        proj = _dot(h, win_ref[...])
        cos, s_lo, s_hi = cos_ref[...], sl_ref[...], sh2_ref[...]
        q = _rope(proj[:, OFF_Q:OFF_K], cos, s_lo, s_hi) * (HEAD_DIM ** -0.5)
        k = _rope(proj[:, OFF_K:OFF_V], cos, s_lo, s_hi)
        v = proj[:, OFF_V:OFF_U]
        knew_ref[...] = k
        vnew_ref[...] = v
        zeros64 = jnp.zeros((n_s, HEAD_DIM), F32)
        for hd in range(H_A):
            qh = q[:, hd * HEAD_DIM:(hd + 1) * HEAD_DIM]
            row = jnp.concatenate([qh, zeros64] if hd < G_A else [zeros64, qh], axis=-1)
            qt_ref[pl.ds(hd, n_s, stride=H_A), :] = row
            kn_ref[pl.ds(hd, n_s, stride=H_A), :] = k
            vn_ref[pl.ds(hd, n_s, stride=H_A), :] = v
        s_new = jnp.sum(qt_ref[...] * kn_ref[...], axis=-1, keepdims=True)
        sn_ref[...] = jnp.broadcast_to(s_new, (rows_all, LANES))

        vn = _layer_norm(proj[:, OFF_GV:OFF_CQ], gg_ref[...], gb_ref[...])
        gv_ref[...] = vn
        ob_ref[...] = proj[:, OFF_U:OFF_GV] * (vn * w0_ref[...] + bs_ref[0:1, :])

        lb = _hgrn_lower_bound(lb_ref, layer)
        f = lb + (1.0 - lb) * jax.nn.sigmoid(proj[:, OFF_CF:OFF_CI])
        ft_ref[...] = f.T
        kt_ref[...] = (1.0 - f).T
        qc_ref[...] = proj[:, OFF_CQ:OFF_CF].T
        it_ref[...] = proj[:, OFF_CI:OFF_CG].T
        og_ref[...] = ng_ref[...] * _silu(proj[:, OFF_CG:])

    base = pl.multiple_of(step * (nb * H_A), nb * H_A)
    for b in range(nb):
        qrows = qt_ref[pl.ds(base + b * H_A, H_A), :].astype(BF16)
        sc_scr[b * H_A:(b + 1) * H_A, :] = _dot(qrows, ck_ref[b].astype(BF16))
    lane = lax.broadcasted_iota(jnp.int32, (nb * H_A, WINDOW), 1)
    s = jnp.where(lane == 0, NEG_BIG, sc_scr[...])
    s_new = sn_ref[pl.ds(base, nb * H_A), 0:1]
    hrow = lax.broadcasted_iota(jnp.int32, (nb * H_A, 1), 0) & (H_A - 1)
    sink = jnp.zeros((nb * H_A, 1), F32)
    for hd in range(H_A):
        sink = jnp.where(hrow == hd, sink_ref[layer, hd], sink)
    mx = jnp.maximum(jnp.maximum(jnp.max(s, axis=-1, keepdims=True), s_new), sink)
    p = jnp.exp(s - mx)
    p_new = jnp.exp(s_new - mx)
    den = jnp.sum(p, axis=-1, keepdims=True) + p_new + jnp.exp(sink - mx)
    sc_scr[...] = p
    for b in range(nb):
        prow = sc_scr[b * H_A:(b + 1) * H_A, :].astype(BF16)
        ot_ref[pl.ds(base + b * H_A, H_A), :] = _dot_t(prow, cv_ref[b].astype(BF16))
    rows = pl.ds(base, nb * H_A)
    ot_ref[rows, :] = (ot_ref[rows, :] + p_new * vn_ref[rows, :]) * (1.0 / den)

    hrow0 = pl.multiple_of(step * DK_C, DK_C)
    i_h = it_ref[pl.ds(hrow0, DV_C), :]
    acc = jnp.zeros((DV_C, n_s), F32)
    for kk in range(DK_C):
        f_k = ft_ref[pl.ds(hrow0 + kk, 1), :]
        k_k = kt_ref[pl.ds(hrow0 + kk, 1), :]
        q_k = qc_ref[pl.ds(hrow0 + kk, 1), :]
        s1 = f_k * s0_ref[kk] + k_k * i_h
        s1_ref[kk] = s1
        acc = acc + q_k * s1
    oc_ref[pl.ds(hrow0, DV_C), :] = acc

    @pl.when(step == pl.num_programs(0) - 1)
    def _():
        o_a = []
        for hd in range(H_A):
            r = ot_ref[pl.ds(hd, n_s, stride=H_A), :]
            o_a.append(r[:, :HEAD_DIM] if hd < G_A else r[:, HEAD_DIM:])
        o_c = oc_ref[...].T
        ms = _head_sum(o_c * o_c, _head_block_ones()) * (1.0 / DV_C)
        o_c = o_c * lax.rsqrt(ms + RMS_EPS) * og_ref[...]
        mix = _dot(jnp.concatenate(o_a + [ob_ref[...], o_c], axis=-1).astype(BF16), wout_ref[...])
        z = alpha * x_ref[...] + (1.0 + gt_ref[...]) * mix
        y_ref[...] = _layer_norm(z, lng_ref[...], lnb_ref[...])


def _mixer_sample(x, mods, layer, w_in, w_out, tables, sinks, gln_g, gln_b, w0_exp, bs_exp, lb, ng_exp,
                  ln_g, ln_b, cache_k, cache_v, state, *, alpha):
    n_s, d = x.shape
    assert n_s % (H_C * SUBLANES) == 0 and n_s % LANES == 0
    nb = n_s // H_C
    depth = w_in.shape[0]
    cos, s_lo, s_hi = tables
    mod_spec = lambda j: pl.BlockSpec((None, n_s, d), lambda i: (layer, 0, j))
    tab_spec = pl.BlockSpec((1, LANES), lambda i: (0, 0))
    lay3 = lambda shape: pl.BlockSpec((None,) + shape, lambda i: (layer,) + (0,) * len(shape))
    rows = pl.BlockSpec((n_s, d), lambda i: (0, 0))
    per_head = pltpu.VMEM((D_C, n_s), F32)
    per_row = pltpu.VMEM((n_s * H_A, LANES), F32)
    return pl.pallas_call(
        functools.partial(_mixer_sample_kernel, layer=layer, n_s=n_s, nb=nb, alpha=alpha),
        grid=(H_C,),
        in_specs=[
            rows,
            mod_spec(3), mod_spec(4), mod_spec(5),
            lay3((d, D_IN)), lay3((d, d)),
            tab_spec, tab_spec, tab_spec,
            pl.BlockSpec(memory_space=pltpu.SMEM),
            lay3((1, D_B)), lay3((1, D_B)), lay3((1, D_B)), lay3((CHUNK_B, D_B)),
            pl.BlockSpec((depth, D_C), lambda i: (0, 0)),
            lay3((1, D_C)),
            pl.BlockSpec((None, 1, d), lambda i: (layer * 3 + 1, 0, 0)),
            pl.BlockSpec((None, 1, d), lambda i: (layer * 3 + 1, 0, 0)),
            pl.BlockSpec((None, nb, D_KV, WINDOW), lambda i: (layer, i, 0, 0)),
            pl.BlockSpec((None, nb, D_KV, WINDOW), lambda i: (layer, i, 0, 0)),
            pl.BlockSpec((None, None, DK_C, DV_C, n_s), lambda i: (layer, i, 0, 0, 0)),
        ],
        out_specs=[
            rows,
            pl.BlockSpec((n_s, D_KV), lambda i: (0, 0)),
            pl.BlockSpec((n_s, D_KV), lambda i: (0, 0)),
            pl.BlockSpec((n_s, D_B), lambda i: (0, 0)),
            pl.BlockSpec((None, DK_C, DV_C, n_s), lambda i: (i, 0, 0, 0)),
        ],
        out_shape=[
            jax.ShapeDtypeStruct((n_s, d), F32),
            jax.ShapeDtypeStruct((n_s, D_KV), F32),
            jax.ShapeDtypeStruct((n_s, D_KV), F32),
            jax.ShapeDtypeStruct((n_s, D_B), F32),
            jax.ShapeDtypeStruct((H_C, DK_C, DV_C, n_s), F32),
        ],
        scratch_shapes=[
            per_row,
            per_row,
            per_row,
            per_row,
            pltpu.VMEM((nb * H_A, WINDOW), F32),
            per_row,
            per_head, per_head, per_head, per_head,
            pltpu.VMEM((n_s, D_B), F32),
            pltpu.VMEM((n_s, D_C), F32),
            per_head,
        ],
        compiler_params=pltpu.CompilerParams(
            dimension_semantics=("arbitrary",), vmem_limit_bytes=VMEM_LIMIT_BYTES),
        name="mixer_sample",
    )(x, mods, mods, mods, w_in, w_out, cos, s_lo, s_hi, sinks, gln_g, gln_b, w0_exp, bs_exp, lb, ng_exp,
      ln_g, ln_b, cache_k, cache_v, state)


def _rope_tables(positions):
    half = ROT_DIM // 2
    f32 = np.float32
    inv = (ROPE_THETA ** (-np.arange(half, dtype=f32) * 2.0 / ROT_DIM)).astype(f32)
    ang = (positions.astype(f32)[:, None] * inv[None, :]).astype(f32)
    cos, sin = np.cos(ang).astype(f32), np.sin(ang).astype(f32)
    n = positions.shape[0]
    rest = HEAD_DIM - ROT_DIM
    cos_h = np.concatenate([cos, cos, np.ones((n, rest), f32)], axis=-1)
    lo_h = np.concatenate([-sin, np.zeros((n, half + rest), f32)], axis=-1)
    hi_h = np.concatenate([np.zeros((n, half), f32), sin, np.zeros((n, rest), f32)], axis=-1)
    two = lambda a: jnp.asarray(np.concatenate([a, a], axis=-1))
    return two(cos_h), two(lo_h), two(hi_h)


def kernel(x_prompt, x_sample, cache_k, cache_v, state_hgrn, c_prompt, c_sample, w_in, w_out, attn_sinks,
           gmlp_ln_g, gmlp_ln_b, gmlp_ws, gmlp_bs, hgrn_lb, hgrn_norm_g, ffn1_in, ffn1_out, ffn2_in,
           ffn2_out, ada_w, ada_b, ln_g, ln_b):
    batch, seq, d = x_prompt.shape
    n_s, dec_seq, _ = x_sample.shape
    depth = w_in.shape[0]
    assert d == D_MODEL and dec_seq == 1 and batch <= MOD_ROWS
    assert cache_k.shape[2] == WINDOW
    alpha = (2 * depth) ** 0.25

    w_in_b, w_out_b = w_in.astype(BF16), w_out.astype(BF16)
    ln_g3 = ln_g.reshape(depth * 3, 1, d)
    ln_b3 = ln_b.reshape(depth * 3, 1, d)
    gln_g = gmlp_ln_g.reshape(depth, 1, D_B)
    gln_b = gmlp_ln_b.reshape(depth, 1, D_B)
    bs_exp = jnp.repeat(jnp.swapaxes(gmlp_bs, 1, 2), HEAD_DIM, axis=2)
    w0_exp = jnp.repeat(gmlp_ws[:, :, 0, 0], HEAD_DIM, axis=1).reshape(depth, 1, D_B)
    ng_exp = jnp.tile(hgrn_norm_g, (1, H_C)).reshape(depth, 1, D_C)
    tab_p = _rope_tables(np.arange(seq))
    tab_s = _rope_tables(PAST_LEN + np.arange(dec_seq))

    c_all = jnp.concatenate([c_prompt, jnp.zeros((MOD_ROWS - batch, d), F32), c_sample], axis=0)
    mod_p, mod_s = _adaln(c_all, ada_w, ada_b)

    ck = jnp.transpose(cache_k, (0, 1, 3, 4, 2)).reshape(depth, n_s, D_KV, WINDOW)
    cv = jnp.transpose(cache_v, (0, 1, 3, 4, 2)).reshape(depth, n_s, D_KV, WINDOW)
    s0 = jnp.transpose(state_hgrn, (0, 2, 3, 4, 1))

    xp = x_prompt.reshape(batch * seq, d)
    xs = x_sample.reshape(n_s, d)
    ffn = functools.partial(_ffn, rows_per_batch=seq, alpha=alpha)
    kw, vw, st_p, kn, vnw, gvs, st_s = [], [], [], [], [], [], []
    for l in range(depth):
        xp, xs = ffn(xp, xs, mod_p, mod_s, l, 0, ffn1_in, ffn1_out, ln_g3, ln_b3)
        xp, k_l, v_l, s_l = _mixer_prompt(
            xp, mod_p, l, w_in_b, w_out_b, tab_p, attn_sinks, gln_g, gln_b, gmlp_ws, bs_exp, hgrn_lb,
            ng_exp, ln_g3, ln_b3, batch=batch, seq=seq, alpha=alpha)
        kw.append(k_l), vw.append(v_l), st_p.append(s_l)
        xs, k_l, v_l, g_l, s_l = _mixer_sample(
            xs, mod_s, l, w_in_b, w_out_b, tab_s, attn_sinks, gln_g, gln_b, w0_exp, bs_exp, hgrn_lb,
            ng_exp, ln_g3, ln_b3, ck, cv, s0, alpha=alpha)
        kn.append(k_l), vnw.append(v_l), gvs.append(g_l), st_s.append(s_l)
        xp, xs = ffn(xp, xs, mod_p, mod_s, l, 2, ffn2_in, ffn2_out, ln_g3, ln_b3)

    stack = lambda parts, shape: jnp.stack(parts, axis=0).reshape((depth,) + shape)
    return (
        xp.reshape(batch, seq, d),
        xs.reshape(n_s, dec_seq, d),
        stack(kw, (batch, WINDOW, KV_A, HEAD_DIM)),
        stack(vw, (batch, WINDOW, KV_A, HEAD_DIM)),
        stack(st_p, (batch, H_C, DK_C, DV_C)),
        stack(kn, (n_s, dec_seq, KV_A, HEAD_DIM)),
        stack(vnw, (n_s, dec_seq, KV_A, HEAD_DIM)),
        stack(gvs, (n_s, dec_seq, D_B)),
        jnp.transpose(jnp.stack(st_s, axis=0), (0, 4, 1, 2, 3)),
    )
```

```python
import functools

import numpy as np
import jax
import jax.numpy as jnp
from jax import lax
from jax.experimental import pallas as pl
from jax.experimental.pallas import tpu as pltpu

F32 = jnp.float32
BF16 = jnp.bfloat16

D_MODEL = 1024
HEAD_DIM = 64
H_A = 8
KV_A = 2
G_A = H_A // KV_A
D_A = H_A * HEAD_DIM
D_KV = KV_A * HEAD_DIM
WINDOW = 128
ROT_DIM = HEAD_DIM // 4
ROPE_THETA = 500000.0
G_B = 4
D_B = G_B * HEAD_DIM
CHUNK_B = 128
H_C = 4
DK_C = 64
DV_C = 64
D_C = H_C * DV_C
D_IN = 2304
D_FF = 2816
N_MOD = 9
LN_EPS = 1e-5
RMS_EPS = 1e-6
NEG_BIG = -1e30
LOG2_E = 1.4426950408889634
PAST_LEN = 16384

OFF_Q = 0
OFF_K = OFF_Q + D_A
OFF_V = OFF_K + D_KV
OFF_U = OFF_V + D_KV
OFF_GV = OFF_U + D_B
OFF_CQ = OFF_GV + D_B
OFF_CF = OFF_CQ + D_C
OFF_CI = OFF_CF + D_C
OFF_CG = OFF_CI + D_C

SUBLANES = 8
LANES = 128
MXU_DIM = 256
VMEM_LIMIT_BYTES = 56 * 1024 * 1024

FFN_ROWS = 1024
FFN_OUT_BLOCKS = 4
MIX_OUT_BLOCKS = 2
FF_CHUNK = MXU_DIM
FF_CONV_STEPS = 11
FF_CONV_COLS = 2 * D_FF // FF_CONV_STEPS
FF_CONV_ROWS = D_FF // FF_CONV_STEPS
ADA_COLS = 2304
MIX_ROWS = 512
HG_CHUNK = 16
MOD_ROWS = SUBLANES


def _dot(a, b):
    return jnp.dot(a, b, preferred_element_type=F32)


def _dot_t(a, b):
    return lax.dot_general(a, b, (((1,), (1,)), ((), ())), preferred_element_type=F32)


def _tdot(a, b):
    return lax.dot_general(a, b, (((0,), (0,)), ((), ())), preferred_element_type=F32)


def _layer_norm(z, g, b):
    mu = jnp.mean(z, axis=-1, keepdims=True)
    zc = z - mu
    var = jnp.mean(zc * zc, axis=-1, keepdims=True)
    return zc * lax.rsqrt(var + LN_EPS) * g + b


def _silu(x):
    return x * jax.nn.sigmoid(x)


def _head_block_ones():
    r = lax.broadcasted_iota(jnp.int32, (D_C, D_C), 0) >> 6
    c = lax.broadcasted_iota(jnp.int32, (D_C, D_C), 1) >> 6
    return jnp.where(r == c, 1.0, 0.0).astype(BF16)


def _head_sum(x, ones_bd):
    hi = x.astype(BF16)
    lo = (x - hi.astype(F32)).astype(BF16)
    return _dot(hi, ones_bd) + _dot(lo, ones_bd)


def _rope(x, cos, sin_lo, sin_hi):
    half = ROT_DIM // 2
    out = []
    for c in range(x.shape[-1] // LANES):
        xc = x[:, c * LANES:(c + 1) * LANES]
        out.append(xc * cos + pltpu.roll(xc, LANES - half, 1) * sin_lo + pltpu.roll(xc, half, 1) * sin_hi)
    return out[0] if len(out) == 1 else jnp.concatenate(out, axis=-1)


def _hgrn_lower_bound(lb_ref, layer):
    w = lb_ref[...]
    e = jnp.exp(w - jnp.max(w, axis=0, keepdims=True))
    p = e / jnp.sum(e, axis=0, keepdims=True)
    lb = jnp.zeros((1, D_C), F32)
    for j in range(1, layer + 1):
        lb = lb + p[j:j + 1, :]
    return lb


def _adaln_kernel(c_ref, w_ref, b_ref, op_ref, os_ref):
    s = _silu(c_ref[...]).astype(BF16)
    r = _dot(s, w_ref[...].astype(BF16)) + b_ref[...]
    op_ref[...] = r[:MOD_ROWS]
    os_ref[...] = r[MOD_ROWS:]


def _adaln(c_all, ada_w, ada_b):
    depth, d, n = ada_w.shape
    rows = c_all.shape[0]
    n_s = rows - MOD_ROWS
    return pl.pallas_call(
        _adaln_kernel,
        grid=(depth, n // ADA_COLS),
        in_specs=[
            pl.BlockSpec((rows, d), lambda l, j: (0, 0)),
            pl.BlockSpec((None, d, ADA_COLS), lambda l, j: (l, 0, j)),
            pl.BlockSpec((None, 1, ADA_COLS), lambda l, j: (l, 0, j)),
        ],
        out_specs=[
            pl.BlockSpec((None, MOD_ROWS, ADA_COLS), lambda l, j: (l, 0, j)),
            pl.BlockSpec((None, n_s, ADA_COLS), lambda l, j: (l, 0, j)),
        ],
        out_shape=[
            jax.ShapeDtypeStruct((depth, MOD_ROWS, n), F32),
            jax.ShapeDtypeStruct((depth, n_s, n), F32),
        ],
        compiler_params=pltpu.CompilerParams(
            dimension_semantics=("arbitrary", "arbitrary"), vmem_limit_bytes=VMEM_LIMIT_BYTES),
        name="adaln",
    )(c_all, ada_w, ada_b.reshape(depth, 1, n))


def _ffn_rows(x, shift, scale, gate, wi_ref, wo_ref, act_ref, lng, lnb, alpha):
    rows = x.shape[0]
    h = (x * (1.0 + scale) + shift).astype(BF16)
    for c in range(D_FF // FF_CHUNK):
        lo = c * FF_CHUNK
        ga = D_FF + lo
        a = _dot(h, wi_ref[lo // FF_CONV_COLS, :, lo % FF_CONV_COLS:lo % FF_CONV_COLS + FF_CHUNK])
        g = _dot(h, wi_ref[ga // FF_CONV_COLS, :, ga % FF_CONV_COLS:ga % FF_CONV_COLS + FF_CHUNK])
        act_ref[0:rows, lo:lo + FF_CHUNK] = (_silu(g) * a).astype(BF16)
    br = rows // FFN_OUT_BLOCKS if rows % (FFN_OUT_BLOCKS * MXU_DIM) == 0 else rows
    half_gate = 0.5 * (1.0 + gate)
    out = []
    for r in range(0, rows, br):
        y = _dot(act_ref[r:r + br, :], wo_ref[...])
        hg = half_gate if half_gate.shape[0] == 1 else half_gate[r:r + br]
        out.append(_layer_norm(alpha * x[r:r + br] + hg * y, lng, lnb))
    return out[0] if len(out) == 1 else jnp.concatenate(out, axis=0)


def _ffn_kernel(xp_ref, xs_ref, shp_ref, scp_ref, gtp_ref, shs_ref, scs_ref, gts_ref, win_ref, wout_ref,
                lng_ref, lnb_ref, op_ref, os_ref, wi_ref, wo_ref, act_ref, *, n_tiles, tiles_per_batch, alpha):
    step = pl.program_id(0)

    @pl.when(step < FF_CONV_STEPS)
    def _():
        wi_ref[step] = win_ref[...].astype(BF16)
        r0 = pl.multiple_of(step * FF_CONV_ROWS, FF_CONV_ROWS)
        wo_ref[pl.ds(r0, FF_CONV_ROWS), :] = wout_ref[...].astype(BF16)

    @pl.when(jnp.logical_and(step >= FF_CONV_STEPS, step < FF_CONV_STEPS + n_tiles))
    def _():
        batch = (step - FF_CONV_STEPS) // tiles_per_batch
        op_ref[...] = _ffn_rows(
            xp_ref[...], shp_ref[pl.ds(batch, 1), :], scp_ref[pl.ds(batch, 1), :], gtp_ref[pl.ds(batch, 1), :],
            wi_ref, wo_ref, act_ref, lng_ref[...], lnb_ref[...], alpha)

    @pl.when(step == FF_CONV_STEPS + n_tiles)
    def _():
        os_ref[...] = _ffn_rows(
            xs_ref[...], shs_ref[...], scs_ref[...], gts_ref[...],
            wi_ref, wo_ref, act_ref, lng_ref[...], lnb_ref[...], alpha)


def _ffn(xp, xs, mod_p, mod_s, layer, sub, w_in, w_out, ln_g, ln_b, *, rows_per_batch, alpha):
    m, d = xp.shape
    n_s = xs.shape[0]
    tm = min(FFN_ROWS, rows_per_batch)
    assert m % tm == 0 and rows_per_batch % tm == 0 and n_s <= tm
    n_tiles = m // tm
    last_conv = FF_CONV_STEPS - 1
    tile = lambda i: jnp.clip(i - FF_CONV_STEPS, 0, n_tiles - 1)
    mod_p_spec = lambda j: pl.BlockSpec((None, MOD_ROWS, d), lambda i: (layer, 0, j))
    mod_s_spec = lambda j: pl.BlockSpec((None, n_s, d), lambda i: (layer, 0, j))
    return pl.pallas_call(
        functools.partial(_ffn_kernel, n_tiles=n_tiles, tiles_per_batch=rows_per_batch // tm, alpha=alpha),
        grid=(FF_CONV_STEPS + n_tiles + 1,),
        in_specs=[
            pl.BlockSpec((tm, d), lambda i: (tile(i), 0)),
            pl.BlockSpec((n_s, d), lambda i: (0, 0)),
            mod_p_spec(3 * sub), mod_p_spec(3 * sub + 1), mod_p_spec(3 * sub + 2),
            mod_s_spec(3 * sub), mod_s_spec(3 * sub + 1), mod_s_spec(3 * sub + 2),
            pl.BlockSpec((None, d, FF_CONV_COLS), lambda i: (layer, 0, jnp.minimum(i, last_conv))),
            pl.BlockSpec((None, FF_CONV_ROWS, d), lambda i: (layer, jnp.minimum(i, last_conv), 0)),
            pl.BlockSpec((None, 1, d), lambda i: (layer * 3 + sub, 0, 0)),
            pl.BlockSpec((None, 1, d), lambda i: (layer * 3 + sub, 0, 0)),
        ],
        out_specs=[
            pl.BlockSpec((tm, d), lambda i: (tile(i), 0)),
            pl.BlockSpec((n_s, d), lambda i: (0, 0)),
        ],
        out_shape=[jax.ShapeDtypeStruct((m, d), F32), jax.ShapeDtypeStruct((n_s, d), F32)],
        scratch_shapes=[
            pltpu.VMEM((FF_CONV_STEPS, d, FF_CONV_COLS), BF16),
            pltpu.VMEM((D_FF, d), BF16),
            pltpu.VMEM((tm, D_FF), BF16),
        ],
        compiler_params=pltpu.CompilerParams(
            dimension_semantics=("arbitrary",), vmem_limit_bytes=VMEM_LIMIT_BYTES),
        name="ffn",
    )(xp, xs, mod_p, mod_p, mod_p, mod_s, mod_s, mod_s, w_in, w_out, ln_g, ln_b)


def _attention_block(q_blk, keys, vals, mask, fill):
    outs = []
    for j in range(KV_A):
        kj = keys[:, j * HEAD_DIM:(j + 1) * HEAD_DIM].astype(BF16)
        vj = vals[:, j * HEAD_DIM:(j + 1) * HEAD_DIM].astype(BF16)
        qj = jnp.concatenate(
            [q_blk[:, (j * G_A + g) * HEAD_DIM:(j * G_A + g + 1) * HEAD_DIM] for g in range(G_A)],
            axis=0).astype(BF16)
        s = _dot_t(qj, kj)
        s = jnp.concatenate(
            [jnp.where(mask, s[g * WINDOW:(g + 1) * WINDOW], fill[j][g]) for g in range(G_A)], axis=0)
        p = jnp.exp2(s - jnp.max(s, axis=-1, keepdims=True))
        den = jnp.sum(p, axis=-1, keepdims=True)
        o = _dot(p.astype(BF16), vj) * (1.0 / den)
        outs.extend(o[g * WINDOW:(g + 1) * WINDOW] for g in range(G_A))
    return jnp.concatenate(outs, axis=-1)


def _mixer_prompt_kernel(
        x_ref, sh_ref, sc_ref, gt_ref, win_ref, wout_ref, cos_ref, sl_ref, sh2_ref, sink_ref,
        gg_ref, gb_ref, ws_ref, bs_ref, lb_ref, ng_ref, lng_ref, lnb_ref,
        y_ref, kwin_ref, vwin_ref, st_out_ref,
        kprev_ref, vprev_ref, st_ref,
        *, layer, tq, alpha):
    b_idx = pl.program_id(0)
    t_idx = pl.program_id(1)
    n_t = pl.num_programs(1)

    @pl.when(t_idx == 0)
    def _():
        kprev_ref[...] = jnp.zeros_like(kprev_ref)
        vprev_ref[...] = jnp.zeros_like(vprev_ref)
        st_ref[...] = jnp.zeros_like(st_ref)

    x = x_ref[...]
    shift = sh_ref[pl.ds(b_idx, 1), :]
    scale = sc_ref[pl.ds(b_idx, 1), :]
    gate = gt_ref[pl.ds(b_idx, 1), :]
    h = (x * (1.0 + scale) + shift).astype(BF16)
    proj = _dot(h, win_ref[...])

    cos, s_lo, s_hi = cos_ref[...], sl_ref[...], sh2_ref[...]
    q = _rope(proj[:, OFF_Q:OFF_K], cos, s_lo, s_hi) * (HEAD_DIM ** -0.5 * LOG2_E)
    k = _rope(proj[:, OFF_K:OFF_V], cos, s_lo, s_hi)
    v = proj[:, OFF_V:OFF_U]

    qi = lax.broadcasted_iota(jnp.int32, (WINDOW, 2 * WINDOW), 0)
    kj = lax.broadcasted_iota(jnp.int32, (WINDOW, 2 * WINDOW), 1)
    dist = kj - qi
    band = (dist - 1).astype(jnp.uint32) < jnp.uint32(WINDOW)
    col0 = lax.broadcasted_iota(jnp.int32, (1, 2 * WINDOW), 1) == 0
    fill = [[jnp.where(col0, sink_ref[layer, j * G_A + g] * LOG2_E, NEG_BIG) for g in range(G_A)]
            for j in range(KV_A)]
    first_row = lax.broadcasted_iota(jnp.int32, (WINDOW, D_KV), 0) == 0
    o_a = []
    for blk in range(tq // WINDOW):
        r0 = blk * WINDOW
        if blk == 0:
            k_prev, v_prev = kprev_ref[...], vprev_ref[...]
            lower = jnp.where(t_idx == 0, WINDOW, 0)
            mask = jnp.logical_and(band, kj >= lower)
        else:
            k_prev, v_prev = k[r0 - WINDOW:r0], v[r0 - WINDOW:r0]
            mask = band
        keys = jnp.concatenate([k_prev, k[r0:r0 + WINDOW]], axis=0)
        vals = jnp.concatenate([jnp.where(first_row, 0.0, v_prev), v[r0:r0 + WINDOW]], axis=0)
        o_a.append(_attention_block(q[r0:r0 + WINDOW], keys, vals, mask, fill))
    o_a = jnp.concatenate(o_a, axis=0)
    kprev_ref[...] = k[tq - WINDOW:]
    vprev_ref[...] = v[tq - WINDOW:]
    kwin_ref[...] = k[tq - WINDOW:]
    vwin_ref[...] = v[tq - WINDOW:]

    vn = _layer_norm(proj[:, OFF_GV:OFF_CQ], gg_ref[...], gb_ref[...])
    u = proj[:, OFF_U:OFF_GV]
    tr = lax.broadcasted_iota(jnp.int32, (CHUNK_B, CHUNK_B), 0)
    tc = lax.broadcasted_iota(jnp.int32, (CHUNK_B, CHUNK_B), 1)
    wm = [jnp.where(tr >= tc, ws_ref[g], 0.0).astype(BF16) for g in range(G_B)]
    o_b = []
    for c in range(tq // CHUNK_B):
        vc = vn[c * CHUNK_B:(c + 1) * CHUNK_B].astype(BF16)
        mixed = jnp.concatenate(
            [_dot(wm[g], vc[:, g * HEAD_DIM:(g + 1) * HEAD_DIM]) for g in range(G_B)], axis=-1)
        o_b.append(u[c * CHUNK_B:(c + 1) * CHUNK_B] * (mixed + bs_ref[...]))
    o_b = jnp.concatenate(o_b, axis=0)

    n_ch = tq // HG_CHUNK
    half = HG_CHUNK // 2
    cq = proj[:, OFF_CQ:OFF_CF]
    ci = proj[:, OFF_CI:OFF_CG]
    lb = _hgrn_lower_bound(lb_ref, layer)
    f = lb + (1.0 - lb) * jax.nn.sigmoid(proj[:, OFF_CF:OFF_CI])
    lf = jnp.log(f)
    kc = 1.0 - f

    sr = min(tq, MXU_DIM)
    rt = lax.broadcasted_iota(jnp.int32, (sr, sr), 0)
    ct = lax.broadcasted_iota(jnp.int32, (sr, sr), 1)
    ltri = jnp.where(ct <= rt, jnp.where(ct >= (rt & -HG_CHUNK), 1.0, 0.0), 0.0).astype(BF16)
    lf_hi = lf.astype(BF16)
    lf_r = lf - lf_hi.astype(F32)
    lf_mid = lf_r.astype(BF16)
    lf_lo = (lf_r - lf_mid.astype(F32)).astype(BF16)
    bsum = jnp.concatenate(
        [_dot(ltri, lf_hi[r:r + sr]) + _dot(ltri, lf_mid[r:r + sr]) + _dot(ltri, lf_lo[r:r + sr])
         for r in range(0, tq, sr)], axis=0) * LOG2_E

    def chunks(a):
        return a.reshape(n_ch, HG_CHUNK, D_C)

    b3, cq3, ci3 = chunks(bsum), chunks(cq), chunks(ci)
    b_end = b3[:, HG_CHUNK - 1:HG_CHUNK, :]
    eb = jnp.exp2(bsum)
    qb = cq * eb
    kb = kc * jnp.exp2(b_end - b3).reshape(tq, D_C)

    ones_bd = _head_block_ones()
    c3 = chunks(bsum - jnp.log2(jnp.maximum(kc, 0.0)))

    def pair_products(s, lo):
        e = cq3[:, lo:] * jnp.exp2(b3[:, lo:] - c3[:, s:s + 1, :])
        pos = lax.broadcasted_iota(jnp.int32, (n_ch, half, D_C), 1) + (s // half) * half
        grp = s // half - lo // half
        parts = [e[:, g * half:(g + 1) * half] for g in range((HG_CHUNK - lo) // half)]
        parts[grp] = jnp.where(pos >= s, parts[grp], 0.0)
        return parts[0] if len(parts) == 1 else jnp.concatenate(parts, axis=1)

    def head_sums(slab):
        return chunks(_dot(slab.reshape(tq, D_C).astype(BF16), ones_bd))

    o_all = jnp.zeros((n_ch, HG_CHUNK, D_C), F32)
    for s in range(half):
        o_all = o_all + head_sums(pair_products(s, 0)) * ci3[:, s:s + 1, :]
    o_up = jnp.zeros((n_ch, half, D_C), F32)
    for s in range(half, HG_CHUNK, 2):
        a3 = head_sums(jnp.concatenate([pair_products(s, half), pair_products(s + 1, half)], axis=1))
        o_up = o_up + a3[:, :half] * ci3[:, s:s + 1, :] + a3[:, half:] * ci3[:, s + 1:s + 2, :]
    o_c = (o_all + jnp.concatenate([jnp.zeros((n_ch, half, D_C), F32), o_up], axis=1)).reshape(tq, D_C)

    lane_head = lax.broadcasted_iota(jnp.int32, (tq, D_C), 1) >> 6
    im_h = [jnp.where(lane_head == hh, ci, 0.0).astype(BF16) for hh in range(H_C)]
    qb_h = [qb[:, hh * DK_C:(hh + 1) * DK_C].astype(BF16) for hh in range(H_C)]
    kb_h = [kb[:, hh * DK_C:(hh + 1) * DK_C].astype(BF16) for hh in range(H_C)]
    eb_t = eb.T
    lane_head_c = lax.broadcasted_iota(jnp.int32, (HG_CHUNK, D_C), 1) >> 6
    st = st_ref[...]
    o_inter = []
    for n in range(n_ch):
        r0, r1 = n * HG_CHUNK, (n + 1) * HG_CHUNK
        read = _dot(jnp.concatenate([a[r0:r1] for a in qb_h], axis=0), st.astype(BF16))
        o_n = jnp.zeros((HG_CHUNK, D_C), F32)
        for hh in range(H_C):
            o_n = o_n + jnp.where(lane_head_c == hh, read[hh * HG_CHUNK:(hh + 1) * HG_CHUNK], 0.0)
        o_inter.append(o_n)
        upd = _tdot(jnp.concatenate([a[r0:r1] for a in kb_h], axis=0),
                    jnp.concatenate([a[r0:r1] for a in im_h], axis=0))
        col = eb_t[:, r1 - 1:r1]
        decay = jnp.concatenate(
            [jnp.broadcast_to(col[hh * DK_C:(hh + 1) * DK_C], (DK_C, DV_C)) for hh in range(H_C)], axis=1)
        st = decay * st + upd
    st_ref[...] = st
    o_c = o_c + jnp.concatenate(o_inter, axis=0)

    @pl.when(t_idx == n_t - 1)
    def _():
        for hh in range(H_C):
            st_out_ref[hh * DK_C:(hh + 1) * DK_C, :] = st[:, hh * DV_C:(hh + 1) * DV_C]

    ms = _head_sum(o_c * o_c, ones_bd) * (1.0 / DV_C)
    o_c = o_c * lax.rsqrt(ms + RMS_EPS) * ng_ref[...] * _silu(proj[:, OFF_CG:])

    o_cat = jnp.concatenate([o_a, o_b, o_c], axis=-1).astype(BF16)
    br = tq // MIX_OUT_BLOCKS if tq % (MIX_OUT_BLOCKS * MXU_DIM) == 0 else tq
    for r in range(0, tq, br):
        z = alpha * x[r:r + br] + (1.0 + gate) * _dot(o_cat[r:r + br], wout_ref[...])
        y_ref[r:r + br, :] = _layer_norm(z, lng_ref[...], lnb_ref[...])


def _mixer_prompt(x, mods, layer, w_in, w_out, tables, sinks, gln_g, gln_b, ws, bs_exp, lb, ng_exp,
                  ln_g, ln_b, *, batch, seq, alpha):
    m, d = x.shape
    tq = min(MIX_ROWS, seq)
    assert seq % tq == 0 and tq % WINDOW == 0
    n_t = seq // tq
    depth = w_in.shape[0]
    cos, s_lo, s_hi = tables
    mod_spec = lambda j: pl.BlockSpec((None, MOD_ROWS, d), lambda b, t: (layer, 0, j))
    tab_spec = pl.BlockSpec((tq, LANES), lambda b, t: (t, 0))
    lay3 = lambda shape: pl.BlockSpec((None,) + shape, lambda b, t: (layer,) + (0,) * len(shape))
    return pl.pallas_call(
        functools.partial(_mixer_prompt_kernel, layer=layer, tq=tq, alpha=alpha),
        grid=(batch, n_t),
        in_specs=[
            pl.BlockSpec((tq, d), lambda b, t: (b * n_t + t, 0)),
            mod_spec(3), mod_spec(4), mod_spec(5),
            lay3((d, D_IN)), lay3((d, d)),
            tab_spec, tab_spec, tab_spec,
            pl.BlockSpec(memory_space=pltpu.SMEM),
            lay3((1, D_B)), lay3((1, D_B)),
            lay3((G_B, CHUNK_B, CHUNK_B)), lay3((CHUNK_B, D_B)),
            pl.BlockSpec((depth, D_C), lambda b, t: (0, 0)),
            lay3((1, D_C)),
            pl.BlockSpec((None, 1, d), lambda b, t: (layer * 3 + 1, 0, 0)),
            pl.BlockSpec((None, 1, d), lambda b, t: (layer * 3 + 1, 0, 0)),
        ],
        out_specs=[
            pl.BlockSpec((tq, d), lambda b, t: (b * n_t + t, 0)),
            pl.BlockSpec((None, WINDOW, D_KV), lambda b, t: (b, 0, 0)),
            pl.BlockSpec((None, WINDOW, D_KV), lambda b, t: (b, 0, 0)),
            pl.BlockSpec((None, D_C, DV_C), lambda b, t: (b, 0, 0)),
        ],
        out_shape=[
            jax.ShapeDtypeStruct((m, d), F32),
            jax.ShapeDtypeStruct((batch, WINDOW, D_KV), F32),
            jax.ShapeDtypeStruct((batch, WINDOW, D_KV), F32),
            jax.ShapeDtypeStruct((batch, D_C, DV_C), F32),
        ],
        scratch_shapes=[
            pltpu.VMEM((WINDOW, D_KV), F32),
            pltpu.VMEM((WINDOW, D_KV), F32),
            pltpu.VMEM((DK_C, D_C), F32),
        ],
        compiler_params=pltpu.CompilerParams(
            dimension_semantics=("arbitrary", "arbitrary"), vmem_limit_bytes=VMEM_LIMIT_BYTES),
        name="mixer_prompt",
    )(x, mods, mods, mods, w_in, w_out, cos, s_lo, s_hi, sinks, gln_g, gln_b, ws, bs_exp, lb, ng_exp,
      ln_g, ln_b)


def _mixer_sample_kernel(
        x_ref, sh_ref, sc_ref, gt_ref, win_ref, wout_ref, cos_ref, sl_ref, sh2_ref, sink_ref,
        gg_ref, gb_ref, w0_ref, bs_ref, lb_ref, ng_ref, lng_ref, lnb_ref, ck_ref, cv_ref, s0_ref,
        y_ref, knew_ref, vnew_ref, gv_ref, s1_ref,
        qt_ref, kn_ref, vn_ref, sn_ref, sc_scr, ot_ref, ft_ref, kt_ref, qc_ref, it_ref, ob_ref, og_ref, oc_ref,
        *, layer, n_s, nb, alpha):
    step = pl.program_id(0)
    rows_all = n_s * H_A

    @pl.when(step == 0)
    def _():
        x = x_ref[...]
        h = (x * (1.0 + sc_ref[...]) + sh_ref[...]).astype(BF16)
        proj = _dot(h, win_ref[...])
        cos, s_lo, s_hi = cos_ref[...], sl_ref[...], sh2_ref[...]
        q = _rope(proj[:, OFF_Q:OFF_K], cos, s_lo, s_hi) * (HEAD_DIM ** -0.5)
        k = _rope(proj[:, OFF_K:OFF_V], cos, s_lo, s_hi)
        v = proj[:, OFF_V:OFF_U]
        knew_ref[...] = k
        vnew_ref[...] = v
        zeros64 = jnp.zeros((n_s, HEAD_DIM), F32)
        for hd in range(H_A):
            qh = q[:, hd * HEAD_DIM:(hd + 1) * HEAD_DIM]
            row = jnp.concatenate([qh, zeros64] if hd < G_A else [zeros64, qh], axis=-1)
            qt_ref[pl.ds(hd, n_s, stride=H_A), :] = row
            kn_ref[pl.ds(hd, n_s, stride=H_A), :] = k
            vn_ref[pl.ds(hd, n_s, stride=H_A), :] = v
        s_new = jnp.sum(qt_ref[...] * kn_ref[...], axis=-1, keepdims=True)
        sn_ref[...] = jnp.broadcast_to(s_new, (rows_all, LANES))

        vn = _layer_norm(proj[:, OFF_GV:OFF_CQ], gg_ref[...], gb_ref[...])
        gv_ref[...] = vn
        ob_ref[...] = proj[:, OFF_U:OFF_GV] * (vn * w0_ref[...] + bs_ref[0:1, :])

        lb = _hgrn_lower_bound(lb_ref, layer)
        f = lb + (1.0 - lb) * jax.nn.sigmoid(proj[:, OFF_CF:OFF_CI])
        ft_ref[...] = f.T
        kt_ref[...] = (1.0 - f).T
        qc_ref[...] = proj[:, OFF_CQ:OFF_CF].T
        it_ref[...] = proj[:, OFF_CI:OFF_CG].T
        og_ref[...] = ng_ref[...] * _silu(proj[:, OFF_CG:])

    base = pl.multiple_of(step * (nb * H_A), nb * H_A)
    for b in range(nb):
        qrows = qt_ref[pl.ds(base + b * H_A, H_A), :].astype(BF16)
        sc_scr[b * H_A:(b + 1) * H_A, :] = _dot(qrows, ck_ref[b].astype(BF16))
    lane = lax.broadcasted_iota(jnp.int32, (nb * H_A, WINDOW), 1)
    s = jnp.where(lane == 0, NEG_BIG, sc_scr[...])
    s_new = sn_ref[pl.ds(base, nb * H_A), 0:1]
    hrow = lax.broadcasted_iota(jnp.int32, (nb * H_A, 1), 0) & (H_A - 1)
    sink = jnp.zeros((nb * H_A, 1), F32)
    for hd in range(H_A):
        sink = jnp.where(hrow == hd, sink_ref[layer, hd], sink)
    mx = jnp.maximum(jnp.maximum(jnp.max(s, axis=-1, keepdims=True), s_new), sink)
    p = jnp.exp(s - mx)
    p_new = jnp.exp(s_new - mx)
    den = jnp.sum(p, axis=-1, keepdims=True) + p_new + jnp.exp(sink - mx)
    sc_scr[...] = p
    for b in range(nb):
        prow = sc_scr[b * H_A:(b + 1) * H_A, :].astype(BF16)
        ot_ref[pl.ds(base + b * H_A, H_A), :] = _dot_t(prow, cv_ref[b].astype(BF16))
    rows = pl.ds(base, nb * H_A)
    ot_ref[rows, :] = (ot_ref[rows, :] + p_new * vn_ref[rows, :]) * (1.0 / den)

    hrow0 = pl.multiple_of(step * DK_C, DK_C)
    i_h = it_ref[pl.ds(hrow0, DV_C), :]
    acc = jnp.zeros((DV_C, n_s), F32)
    for kk in range(DK_C):
        f_k = ft_ref[pl.ds(hrow0 + kk, 1), :]
        k_k = kt_ref[pl.ds(hrow0 + kk, 1), :]
        q_k = qc_ref[pl.ds(hrow0 + kk, 1), :]
        s1 = f_k * s0_ref[kk] + k_k * i_h
        s1_ref[kk] = s1
        acc = acc + q_k * s1
    oc_ref[pl.ds(hrow0, DV_C), :] = acc

    @pl.when(step == pl.num_programs(0) - 1)
    def _():
        o_a = []
        for hd in range(H_A):
            r = ot_ref[pl.ds(hd, n_s, stride=H_A), :]
            o_a.append(r[:, :HEAD_DIM] if hd < G_A else r[:, HEAD_DIM:])
        o_c = oc_ref[...].T
        ms = _head_sum(o_c * o_c, _head_block_ones()) * (1.0 / DV_C)
        o_c = o_c * lax.rsqrt(ms + RMS_EPS) * og_ref[...]
        mix = _dot(jnp.concatenate(o_a + [ob_ref[...], o_c], axis=-1).astype(BF16), wout_ref[...])
        z = alpha * x_ref[...] + (1.0 + gt_ref[...]) * mix
        y_ref[...] = _layer_norm(z, lng_ref[...], lnb_ref[...])


def _mixer_sample(x, mods, layer, w_in, w_out, tables, sinks, gln_g, gln_b, w0_exp, bs_exp, lb, ng_exp,
                  ln_g, ln_b, cache_k, cache_v, state, *, alpha):
    n_s, d = x.shape
    assert n_s % (H_C * SUBLANES) == 0 and n_s % LANES == 0
    nb = n_s // H_C
    depth = w_in.shape[0]
    cos, s_lo, s_hi = tables
    mod_spec = lambda j: pl.BlockSpec((None, n_s, d), lambda i: (layer, 0, j))
    tab_spec = pl.BlockSpec((1, LANES), lambda i: (0, 0))
    lay3 = lambda shape: pl.BlockSpec((None,) + shape, lambda i: (layer,) + (0,) * len(shape))
    rows = pl.BlockSpec((n_s, d), lambda i: (0, 0))
    per_head = pltpu.VMEM((D_C, n_s), F32)
    per_row = pltpu.VMEM((n_s * H_A, LANES), F32)
    return pl.pallas_call(
        functools.partial(_mixer_sample_kernel, layer=layer, n_s=n_s, nb=nb, alpha=alpha),
        grid=(H_C,),
        in_specs=[
            rows,
            mod_spec(3), mod_spec(4), mod_spec(5),
            lay3((d, D_IN)), lay3((d, d)),
            tab_spec, tab_spec, tab_spec,
            pl.BlockSpec(memory_space=pltpu.SMEM),
            lay3((1, D_B)), lay3((1, D_B)), lay3((1, D_B)), lay3((CHUNK_B, D_B)),
            pl.BlockSpec((depth, D_C), lambda i: (0, 0)),
            lay3((1, D_C)),
            pl.BlockSpec((None, 1, d), lambda i: (layer * 3 + 1, 0, 0)),
            pl.BlockSpec((None, 1, d), lambda i: (layer * 3 + 1, 0, 0)),
            pl.BlockSpec((None, nb, D_KV, WINDOW), lambda i: (layer, i, 0, 0)),
            pl.BlockSpec((None, nb, D_KV, WINDOW), lambda i: (layer, i, 0, 0)),
            pl.BlockSpec((None, None, DK_C, DV_C, n_s), lambda i: (layer, i, 0, 0, 0)),
        ],
        out_specs=[
            rows,
            pl.BlockSpec((n_s, D_KV), lambda i: (0, 0)),
            pl.BlockSpec((n_s, D_KV), lambda i: (0, 0)),
            pl.BlockSpec((n_s, D_B), lambda i: (0, 0)),
            pl.BlockSpec((None, DK_C, DV_C, n_s), lambda i: (i, 0, 0, 0)),
        ],
        out_shape=[
            jax.ShapeDtypeStruct((n_s, d), F32),
            jax.ShapeDtypeStruct((n_s, D_KV), F32),
            jax.ShapeDtypeStruct((n_s, D_KV), F32),
            jax.ShapeDtypeStruct((n_s, D_B), F32),
            jax.ShapeDtypeStruct((H_C, DK_C, DV_C, n_s), F32),
        ],
        scratch_shapes=[
            per_row,
            per_row,
            per_row,
            per_row,
            pltpu.VMEM((nb * H_A, WINDOW), F32),
            per_row,
            per_head, per_head, per_head, per_head,
            pltpu.VMEM((n_s, D_B), F32),
            pltpu.VMEM((n_s, D_C), F32),
            per_head,
        ],
        compiler_params=pltpu.CompilerParams(
            dimension_semantics=("arbitrary",), vmem_limit_bytes=VMEM_LIMIT_BYTES),
        name="mixer_sample",
    )(x, mods, mods, mods, w_in, w_out, cos, s_lo, s_hi, sinks, gln_g, gln_b, w0_exp, bs_exp, lb, ng_exp,
      ln_g, ln_b, cache_k, cache_v, state)


def _rope_tables(positions):
    half = ROT_DIM // 2
    f32 = np.float32
    inv = (ROPE_THETA ** (-np.arange(half, dtype=f32) * 2.0 / ROT_DIM)).astype(f32)
    ang = (positions.astype(f32)[:, None] * inv[None, :]).astype(f32)
    cos, sin = np.cos(ang).astype(f32), np.sin(ang).astype(f32)
    n = positions.shape[0]
    rest = HEAD_DIM - ROT_DIM
    cos_h = np.concatenate([cos, cos, np.ones((n, rest), f32)], axis=-1)
    lo_h = np.concatenate([-sin, np.zeros((n, half + rest), f32)], axis=-1)
    hi_h = np.concatenate([np.zeros((n, half), f32), sin, np.zeros((n, rest), f32)], axis=-1)
    two = lambda a: jnp.asarray(np.concatenate([a, a], axis=-1))
    return two(cos_h), two(lo_h), two(hi_h)


def kernel(x_prompt, x_sample, cache_k, cache_v, state_hgrn, c_prompt, c_sample, w_in, w_out, attn_sinks,
           gmlp_ln_g, gmlp_ln_b, gmlp_ws, gmlp_bs, hgrn_lb, hgrn_norm_g, ffn1_in, ffn1_out, ffn2_in,
           ffn2_out, ada_w, ada_b, ln_g, ln_b):
    batch, seq, d = x_prompt.shape
    n_s, dec_seq, _ = x_sample.shape
    depth = w_in.shape[0]
    assert d == D_MODEL and dec_seq == 1 and batch <= MOD_ROWS
    assert cache_k.shape[2] == WINDOW
    alpha = (2 * depth) ** 0.25

    w_in_b, w_out_b = w_in.astype(BF16), w_out.astype(BF16)
    ln_g3 = ln_g.reshape(depth * 3, 1, d)
    ln_b3 = ln_b.reshape(depth * 3, 1, d)
    gln_g = gmlp_ln_g.reshape(depth, 1, D_B)
    gln_b = gmlp_ln_b.reshape(depth, 1, D_B)
    bs_exp = jnp.repeat(jnp.swapaxes(gmlp_bs, 1, 2), HEAD_DIM, axis=2)
    w0_exp = jnp.repeat(gmlp_ws[:, :, 0, 0], HEAD_DIM, axis=1).reshape(depth, 1, D_B)
    ng_exp = jnp.tile(hgrn_norm_g, (1, H_C)).reshape(depth, 1, D_C)
    tab_p = _rope_tables(np.arange(seq))
    tab_s = _rope_tables(PAST_LEN + np.arange(dec_seq))

    c_all = jnp.concatenate([c_prompt, jnp.zeros((MOD_ROWS - batch, d), F32), c_sample], axis=0)
    mod_p, mod_s = _adaln(c_all, ada_w, ada_b)

    ck = jnp.transpose(cache_k, (0, 1, 3, 4, 2)).reshape(depth, n_s, D_KV, WINDOW)
    cv = jnp.transpose(cache_v, (0, 1, 3, 4, 2)).reshape(depth, n_s, D_KV, WINDOW)
    s0 = jnp.transpose(state_hgrn, (0, 2, 3, 4, 1))

    xp = x_prompt.reshape(batch * seq, d)
    xs = x_sample.reshape(n_s, d)
    ffn = functools.partial(_ffn, rows_per_batch=seq, alpha=alpha)
    kw, vw, st_p, kn, vnw, gvs, st_s = [], [], [], [], [], [], []
    for l in range(depth):
        xp, xs = ffn(xp, xs, mod_p, mod_s, l, 0, ffn1_in, ffn1_out, ln_g3, ln_b3)
        xp, k_l, v_l, s_l = _mixer_prompt(
            xp, mod_p, l, w_in_b, w_out_b, tab_p, attn_sinks, gln_g, gln_b, gmlp_ws, bs_exp, hgrn_lb,
            ng_exp, ln_g3, ln_b3, batch=batch, seq=seq, alpha=alpha)
        kw.append(k_l), vw.append(v_l), st_p.append(s_l)
        xs, k_l, v_l, g_l, s_l = _mixer_sample(
            xs, mod_s, l, w_in_b, w_out_b, tab_s, attn_sinks, gln_g, gln_b, w0_exp, bs_exp, hgrn_lb,
            ng_exp, ln_g3, ln_b3, ck, cv, s0, alpha=alpha)
        kn.append(k_l), vnw.append(v_l), gvs.append(g_l), st_s.append(s_l)
        xp, xs = ffn(xp, xs, mod_p, mod_s, l, 2, ffn2_in, ffn2_out, ln_g3, ln_b3)

    stack = lambda parts, shape: jnp.stack(parts, axis=0).reshape((depth,) + shape)
    return (
        xp.reshape(batch, seq, d),
        xs.reshape(n_s, dec_seq, d),
        stack(kw, (batch, WINDOW, KV_A, HEAD_DIM)),
        stack(vw, (batch, WINDOW, KV_A, HEAD_DIM)),
        stack(st_p, (batch, H_C, DK_C, DV_C)),
        stack(kn, (n_s, dec_seq, KV_A, HEAD_DIM)),
        stack(vnw, (n_s, dec_seq, KV_A, HEAD_DIM)),
        stack(gvs, (n_s, dec_seq, D_B)),
        jnp.transpose(jnp.stack(st_s, axis=0), (0, 4, 1, 2, 3)),
    )
```

```python
import functools

import numpy as np
import jax
import jax.numpy as jnp
from jax import lax
from jax.experimental import pallas as pl
from jax.experimental.pallas import tpu as pltpu

F32 = jnp.float32
BF16 = jnp.bfloat16

D_MODEL = 1024
HEAD_DIM = 64
H_A = 8
KV_A = 2
G_A = H_A // KV_A
D_A = H_A * HEAD_DIM
D_KV = KV_A * HEAD_DIM
WINDOW = 128
ROT_DIM = HEAD_DIM // 4
ROPE_THETA = 500000.0
G_B = 4
D_B = G_B * HEAD_DIM
CHUNK_B = 128
H_C = 4
DK_C = 64
DV_C = 64
D_C = H_C * DV_C
D_IN = 2304
D_FF = 2816
N_MOD = 9
LN_EPS = 1e-5
RMS_EPS = 1e-6
NEG_BIG = -1e30
LOG2_E = 1.4426950408889634
PAST_LEN = 16384

OFF_Q = 0
OFF_K = OFF_Q + D_A
OFF_V = OFF_K + D_KV
OFF_U = OFF_V + D_KV
OFF_GV = OFF_U + D_B
OFF_CQ = OFF_GV + D_B
OFF_CF = OFF_CQ + D_C
OFF_CI = OFF_CF + D_C
OFF_CG = OFF_CI + D_C

SUBLANES = 8
LANES = 128
MXU_DIM = 256
VMEM_LIMIT_BYTES = 56 * 1024 * 1024

FFN_ROWS = 1024
FFN_OUT_BLOCKS = 4
MIX_OUT_BLOCKS = 2
FF_CHUNK = MXU_DIM
FF_CHUNKS = D_FF // FF_CHUNK
ADA_COLS = 2304
MIX_ROWS = 512
HG_CHUNK = 16
MOD_ROWS = SUBLANES


def _dot(a, b):
    return jnp.dot(a, b, preferred_element_type=F32)


def _dot_t(a, b):
    return lax.dot_general(a, b, (((1,), (1,)), ((), ())), preferred_element_type=F32)


def _tdot(a, b):
    return lax.dot_general(a, b, (((0,), (0,)), ((), ())), preferred_element_type=F32)


def _layer_norm(z, g, b):
    mu = jnp.mean(z, axis=-1, keepdims=True)
    zc = z - mu
    var = jnp.mean(zc * zc, axis=-1, keepdims=True)
    return zc * lax.rsqrt(var + LN_EPS) * g + b


def _silu(x):
    return x * jax.nn.sigmoid(x)


def _head_block_ones():
    r = lax.broadcasted_iota(jnp.int32, (D_C, D_C), 0) >> 6
    c = lax.broadcasted_iota(jnp.int32, (D_C, D_C), 1) >> 6
    return jnp.where(r == c, 1.0, 0.0).astype(BF16)


def _head_sum(x, ones_bd):
    hi = x.astype(BF16)
    lo = (x - hi.astype(F32)).astype(BF16)
    return _dot(hi, ones_bd) + _dot(lo, ones_bd)


def _rope(x, cos, sin_lo, sin_hi):
    half = ROT_DIM // 2
    out = []
    for c in range(x.shape[-1] // LANES):
        xc = x[:, c * LANES:(c + 1) * LANES]
        out.append(xc * cos + pltpu.roll(xc, LANES - half, 1) * sin_lo + pltpu.roll(xc, half, 1) * sin_hi)
    return out[0] if len(out) == 1 else jnp.concatenate(out, axis=-1)


def _hgrn_lower_bound(lb_ref, layer):
    w = lb_ref[...]
    e = jnp.exp(w - jnp.max(w, axis=0, keepdims=True))
    p = e / jnp.sum(e, axis=0, keepdims=True)
    lb = jnp.zeros((1, D_C), F32)
    for j in range(1, layer + 1):
        lb = lb + p[j:j + 1, :]
    return lb


def _adaln_kernel(c_ref, w_ref, b_ref, op_ref, os_ref):
    s = _silu(c_ref[...]).astype(BF16)
    r = _dot(s, w_ref[...].astype(BF16)) + b_ref[...]
    op_ref[...] = r[:MOD_ROWS]
    os_ref[...] = r[MOD_ROWS:]


def _adaln(c_all, ada_w, ada_b):
    depth, d, n = ada_w.shape
    rows = c_all.shape[0]
    n_s = rows - MOD_ROWS
    return pl.pallas_call(
        _adaln_kernel,
        grid=(depth, n // ADA_COLS),
        in_specs=[
            pl.BlockSpec((rows, d), lambda l, j: (0, 0)),
            pl.BlockSpec((None, d, ADA_COLS), lambda l, j: (l, 0, j)),
            pl.BlockSpec((None, 1, ADA_COLS), lambda l, j: (l, 0, j)),
        ],
        out_specs=[
            pl.BlockSpec((None, MOD_ROWS, ADA_COLS), lambda l, j: (l, 0, j)),
            pl.BlockSpec((None, n_s, ADA_COLS), lambda l, j: (l, 0, j)),
        ],
        out_shape=[
            jax.ShapeDtypeStruct((depth, MOD_ROWS, n), F32),
            jax.ShapeDtypeStruct((depth, n_s, n), F32),
        ],
        compiler_params=pltpu.CompilerParams(
            dimension_semantics=("arbitrary", "arbitrary"), vmem_limit_bytes=VMEM_LIMIT_BYTES),
        name="adaln",
    )(c_all, ada_w, ada_b.reshape(depth, 1, n))


def _ffn_chunk(h, wa, wg):
    return (_silu(_dot(h, wg)) * _dot(h, wa)).astype(BF16)


def _ffn_finish(x, gate, wo_ref, act_ref, lng, lnb, alpha):
    rows = x.shape[0]
    br = rows // FFN_OUT_BLOCKS if rows % (FFN_OUT_BLOCKS * MXU_DIM) == 0 else rows
    half_gate = 0.5 * (1.0 + gate)
    out = []
    for r in range(0, rows, br):
        act = jnp.concatenate([act_ref[c, r:r + br, :] for c in range(FF_CHUNKS)], axis=1)
        hg = half_gate if half_gate.shape[0] == 1 else half_gate[r:r + br]
        out.append(_layer_norm(alpha * x[r:r + br] + hg * _dot(act, wo_ref[...]), lng, lnb))
    return out[0] if len(out) == 1 else jnp.concatenate(out, axis=0)


def _ffn_rows(x, shift, scale, gate, wa_refs, wg_refs, wo_ref, act_ref, lng, lnb, alpha):
    rows = x.shape[0]
    h = (x * (1.0 + scale) + shift).astype(BF16)
    for c in range(FF_CHUNKS):
        act_ref[c, 0:rows, :] = _ffn_chunk(h, wa_refs[c % 2][c // 2], wg_refs[c % 2][c // 2])
    return _ffn_finish(x, gate, wo_ref, act_ref, lng, lnb, alpha)


def _ffn_kernel(xp_ref, xs_ref, shp_ref, scp_ref, gtp_ref, shs_ref, scs_ref, gts_ref, wa_in, wg_in, wo_in,
                lng_ref, lnb_ref, op_ref, os_ref, wa0, wa1, wg0, wg1, wo_ref, act_ref,
                *, n_tiles, tiles_per_batch, alpha):
    step = pl.program_id(0)
    wa_refs, wg_refs = (wa0, wa1), (wg0, wg1)

    def cast_slabs(parity):
        idx = lax.shift_right_logical(step, 1)
        wa_refs[parity][idx] = wa_in[...].astype(BF16)
        wg_refs[parity][idx] = wg_in[...].astype(BF16)
        r0 = pl.multiple_of(step * FF_CHUNK, FF_CHUNK)
        wo_ref[pl.ds(r0, FF_CHUNK), :] = wo_in[...].astype(BF16)

    def tile0_chunk(parity):
        c = step - 1
        idx = lax.shift_right_logical(c, 1)
        h0 = (xp_ref[...] * (1.0 + scp_ref[0:1, :]) + shp_ref[0:1, :]).astype(BF16)
        act_ref[c] = _ffn_chunk(h0, wa_refs[parity][idx], wg_refs[parity][idx])

    @pl.when(step == 0)
    def _():
        cast_slabs(0)

    overlap = jnp.logical_and(step >= 1, step < FF_CHUNKS)
    odd = (step & 1) == 1

    @pl.when(jnp.logical_and(overlap, odd))
    def _():
        cast_slabs(1)
        tile0_chunk(0)

    @pl.when(jnp.logical_and(overlap, jnp.logical_not(odd)))
    def _():
        cast_slabs(0)
        tile0_chunk(1)

    @pl.when(step == FF_CHUNKS)
    def _():
        tile0_chunk((FF_CHUNKS - 1) % 2)
        op_ref[...] = _ffn_finish(xp_ref[...], gtp_ref[0:1, :], wo_ref, act_ref, lng_ref[...], lnb_ref[...], alpha)

    @pl.when(jnp.logical_and(step > FF_CHUNKS, step < FF_CHUNKS + n_tiles))
    def _():
        batch = (step - FF_CHUNKS) // tiles_per_batch
        op_ref[...] = _ffn_rows(
            xp_ref[...], shp_ref[pl.ds(batch, 1), :], scp_ref[pl.ds(batch, 1), :], gtp_ref[pl.ds(batch, 1), :],
            wa_refs, wg_refs, wo_ref, act_ref, lng_ref[...], lnb_ref[...], alpha)

    @pl.when(step == FF_CHUNKS + n_tiles)
    def _():
        os_ref[...] = _ffn_rows(
            xs_ref[...], shs_ref[...], scs_ref[...], gts_ref[...],
            wa_refs, wg_refs, wo_ref, act_ref, lng_ref[...], lnb_ref[...], alpha)


def _ffn(xp, xs, mod_p, mod_s, layer, sub, w_in, w_out, ln_g, ln_b, *, rows_per_batch, alpha):
    m, d = xp.shape
    n_s = xs.shape[0]
    tm = min(FFN_ROWS, rows_per_batch)
    assert m % tm == 0 and rows_per_batch % tm == 0 and n_s <= tm
    n_tiles = m // tm
    slab = lambda i: jnp.minimum(i, FF_CHUNKS - 1)
    tile = lambda i: jnp.clip(i - FF_CHUNKS, 0, n_tiles - 1)
    mod_p_spec = lambda j: pl.BlockSpec((None, MOD_ROWS, d), lambda i: (layer, 0, j))
    once = dict(pipeline_mode=pl.Buffered(1))
    mod_s_spec = lambda j: pl.BlockSpec((None, n_s, d), lambda i: (layer, 0, j), **once)
    half_slabs = pltpu.VMEM(((FF_CHUNKS + 1) // 2, d, FF_CHUNK), BF16)
    return pl.pallas_call(
        functools.partial(_ffn_kernel, n_tiles=n_tiles, tiles_per_batch=rows_per_batch // tm, alpha=alpha),
        grid=(FF_CHUNKS + n_tiles + 1,),
        in_specs=[
            pl.BlockSpec((tm, d), lambda i: (tile(i), 0)),
            pl.BlockSpec((n_s, d), lambda i: (0, 0), **once),
            mod_p_spec(3 * sub), mod_p_spec(3 * sub + 1), mod_p_spec(3 * sub + 2),
            mod_s_spec(3 * sub), mod_s_spec(3 * sub + 1), mod_s_spec(3 * sub + 2),
            pl.BlockSpec((None, d, FF_CHUNK), lambda i: (layer, 0, slab(i))),
            pl.BlockSpec((None, d, FF_CHUNK), lambda i: (layer, 0, FF_CHUNKS + slab(i))),
            pl.BlockSpec((None, FF_CHUNK, d), lambda i: (layer, slab(i), 0)),
            pl.BlockSpec((None, 1, d), lambda i: (layer * 3 + sub, 0, 0)),
            pl.BlockSpec((None, 1, d), lambda i: (layer * 3 + sub, 0, 0)),
        ],
        out_specs=[
            pl.BlockSpec((tm, d), lambda i: (tile(i), 0)),
            pl.BlockSpec((n_s, d), lambda i: (0, 0)),
        ],
        out_shape=[jax.ShapeDtypeStruct((m, d), F32), jax.ShapeDtypeStruct((n_s, d), F32)],
        scratch_shapes=[
            half_slabs, half_slabs,
            half_slabs, half_slabs,
            pltpu.VMEM((D_FF, d), BF16),
            pltpu.VMEM((FF_CHUNKS, tm, FF_CHUNK), BF16),
        ],
        compiler_params=pltpu.CompilerParams(
            dimension_semantics=("arbitrary",), vmem_limit_bytes=VMEM_LIMIT_BYTES),
        name="ffn",
    )(xp, xs, mod_p, mod_p, mod_p, mod_s, mod_s, mod_s, w_in, w_in, w_out, ln_g, ln_b)


def _attention_block(q_blk, keys, vals, mask, fill):
    outs = []
    for j in range(KV_A):
        kj = keys[:, j * HEAD_DIM:(j + 1) * HEAD_DIM].astype(BF16)
        vj = vals[:, j * HEAD_DIM:(j + 1) * HEAD_DIM].astype(BF16)
        qj = jnp.concatenate(
            [q_blk[:, (j * G_A + g) * HEAD_DIM:(j * G_A + g + 1) * HEAD_DIM] for g in range(G_A)],
            axis=0).astype(BF16)
        s = _dot_t(qj, kj)
        s = jnp.concatenate(
            [jnp.where(mask, s[g * WINDOW:(g + 1) * WINDOW], fill[j][g]) for g in range(G_A)], axis=0)
        p = jnp.exp2(s - jnp.max(s, axis=-1, keepdims=True))
        den = jnp.sum(p, axis=-1, keepdims=True)
        o = _dot(p.astype(BF16), vj) * (1.0 / den)
        outs.extend(o[g * WINDOW:(g + 1) * WINDOW] for g in range(G_A))
    return jnp.concatenate(outs, axis=-1)


def _mixer_prompt_kernel(
        x_ref, sh_ref, sc_ref, gt_ref, win_ref, wout_ref, cos_ref, sl_ref, sh2_ref, sink_ref,
        gg_ref, gb_ref, ws_ref, bs_ref, lb_ref, ng_ref, lng_ref, lnb_ref,
        y_ref, kwin_ref, vwin_ref, st_out_ref,
        kprev_ref, vprev_ref, st_ref,
        *, layer, tq, alpha):
    b_idx = pl.program_id(0)
    t_idx = pl.program_id(1)
    n_t = pl.num_programs(1)

    @pl.when(t_idx == 0)
    def _():
        kprev_ref[...] = jnp.zeros_like(kprev_ref)
        vprev_ref[...] = jnp.zeros_like(vprev_ref)
        st_ref[...] = jnp.zeros_like(st_ref)

    x = x_ref[...]
    shift = sh_ref[pl.ds(b_idx, 1), :]
    scale = sc_ref[pl.ds(b_idx, 1), :]
    gate = gt_ref[pl.ds(b_idx, 1), :]
    h = (x * (1.0 + scale) + shift).astype(BF16)
    proj = _dot(h, win_ref[...])

    cos, s_lo, s_hi = cos_ref[...], sl_ref[...], sh2_ref[...]
    q = _rope(proj[:, OFF_Q:OFF_K], cos, s_lo, s_hi) * (HEAD_DIM ** -0.5 * LOG2_E)
    k = _rope(proj[:, OFF_K:OFF_V], cos, s_lo, s_hi)
    v = proj[:, OFF_V:OFF_U]

    qi = lax.broadcasted_iota(jnp.int32, (WINDOW, 2 * WINDOW), 0)
    kj = lax.broadcasted_iota(jnp.int32, (WINDOW, 2 * WINDOW), 1)
    dist = kj - qi
    band = (dist - 1).astype(jnp.uint32) < jnp.uint32(WINDOW)
    col0 = lax.broadcasted_iota(jnp.int32, (1, 2 * WINDOW), 1) == 0
    fill = [[jnp.where(col0, sink_ref[layer, j * G_A + g] * LOG2_E, NEG_BIG) for g in range(G_A)]
            for j in range(KV_A)]
    first_row = lax.broadcasted_iota(jnp.int32, (WINDOW, D_KV), 0) == 0
    o_a = []
    for blk in range(tq // WINDOW):
        r0 = blk * WINDOW
        if blk == 0:
            k_prev, v_prev = kprev_ref[...], vprev_ref[...]
            lower = jnp.where(t_idx == 0, WINDOW, 0)
            mask = jnp.logical_and(band, kj >= lower)
        else:
            k_prev, v_prev = k[r0 - WINDOW:r0], v[r0 - WINDOW:r0]
            mask = band
        keys = jnp.concatenate([k_prev, k[r0:r0 + WINDOW]], axis=0)
        vals = jnp.concatenate([jnp.where(first_row, 0.0, v_prev), v[r0:r0 + WINDOW]], axis=0)
        o_a.append(_attention_block(q[r0:r0 + WINDOW], keys, vals, mask, fill))
    o_a = jnp.concatenate(o_a, axis=0)
    kprev_ref[...] = k[tq - WINDOW:]
    vprev_ref[...] = v[tq - WINDOW:]
    kwin_ref[...] = k[tq - WINDOW:]
    vwin_ref[...] = v[tq - WINDOW:]

    vn = _layer_norm(proj[:, OFF_GV:OFF_CQ], gg_ref[...], gb_ref[...])
    u = proj[:, OFF_U:OFF_GV]
    tr = lax.broadcasted_iota(jnp.int32, (CHUNK_B, CHUNK_B), 0)
    tc = lax.broadcasted_iota(jnp.int32, (CHUNK_B, CHUNK_B), 1)
    wm = [jnp.where(tr >= tc, ws_ref[g], 0.0).astype(BF16) for g in range(G_B)]
    o_b = []
    for c in range(tq // CHUNK_B):
        vc = vn[c * CHUNK_B:(c + 1) * CHUNK_B].astype(BF16)
        mixed = jnp.concatenate(
            [_dot(wm[g], vc[:, g * HEAD_DIM:(g + 1) * HEAD_DIM]) for g in range(G_B)], axis=-1)
        o_b.append(u[c * CHUNK_B:(c + 1) * CHUNK_B] * (mixed + bs_ref[...]))
    o_b = jnp.concatenate(o_b, axis=0)

    n_ch = tq // HG_CHUNK
    half = HG_CHUNK // 2
    cq = proj[:, OFF_CQ:OFF_CF]
    ci = proj[:, OFF_CI:OFF_CG]
    lb = _hgrn_lower_bound(lb_ref, layer)
    f = lb + (1.0 - lb) * jax.nn.sigmoid(proj[:, OFF_CF:OFF_CI])
    lf = jnp.log(f)
    kc = 1.0 - f

    sr = min(tq, MXU_DIM)
    rt = lax.broadcasted_iota(jnp.int32, (sr, sr), 0)
    ct = lax.broadcasted_iota(jnp.int32, (sr, sr), 1)
    ltri = jnp.where(ct <= rt, jnp.where(ct >= (rt & -HG_CHUNK), 1.0, 0.0), 0.0).astype(BF16)
    lf_hi = lf.astype(BF16)
    lf_r = lf - lf_hi.astype(F32)
    lf_mid = lf_r.astype(BF16)
    lf_lo = (lf_r - lf_mid.astype(F32)).astype(BF16)
    bsum = jnp.concatenate(
        [_dot(ltri, lf_hi[r:r + sr]) + _dot(ltri, lf_mid[r:r + sr]) + _dot(ltri, lf_lo[r:r + sr])
         for r in range(0, tq, sr)], axis=0) * LOG2_E

    def chunks(a):
        return a.reshape(n_ch, HG_CHUNK, D_C)

    b3, cq3, ci3 = chunks(bsum), chunks(cq), chunks(ci)
    b_end = b3[:, HG_CHUNK - 1:HG_CHUNK, :]
    eb = jnp.exp2(bsum)
    qb = cq * eb
    kb = kc * jnp.exp2(b_end - b3).reshape(tq, D_C)

    ones_bd = _head_block_ones()
    c3 = chunks(bsum - jnp.log2(jnp.maximum(kc, 0.0)))

    def pair_products(s, lo):
        e = cq3[:, lo:] * jnp.exp2(b3[:, lo:] - c3[:, s:s + 1, :])
        pos = lax.broadcasted_iota(jnp.int32, (n_ch, half, D_C), 1) + (s // half) * half
        grp = s // half - lo // half
        parts = [e[:, g * half:(g + 1) * half] for g in range((HG_CHUNK - lo) // half)]
        parts[grp] = jnp.where(pos >= s, parts[grp], 0.0)
        return parts[0] if len(parts) == 1 else jnp.concatenate(parts, axis=1)

    def head_sums(slab):
        return chunks(_dot(slab.reshape(tq, D_C).astype(BF16), ones_bd))

    o_all = jnp.zeros((n_ch, HG_CHUNK, D_C), F32)
    for s in range(half):
        o_all = o_all + head_sums(pair_products(s, 0)) * ci3[:, s:s + 1, :]
    o_up = jnp.zeros((n_ch, half, D_C), F32)
    for s in range(half, HG_CHUNK, 2):
        a3 = head_sums(jnp.concatenate([pair_products(s, half), pair_products(s + 1, half)], axis=1))
        o_up = o_up + a3[:, :half] * ci3[:, s:s + 1, :] + a3[:, half:] * ci3[:, s + 1:s + 2, :]
    o_c = (o_all + jnp.concatenate([jnp.zeros((n_ch, half, D_C), F32), o_up], axis=1)).reshape(tq, D_C)

    lane_head = lax.broadcasted_iota(jnp.int32, (tq, D_C), 1) >> 6
    im_h = [jnp.where(lane_head == hh, ci, 0.0).astype(BF16) for hh in range(H_C)]
    qb_h = [qb[:, hh * DK_C:(hh + 1) * DK_C].astype(BF16) for hh in range(H_C)]
    kb_h = [kb[:, hh * DK_C:(hh + 1) * DK_C].astype(BF16) for hh in range(H_C)]
    eb_t = eb.T
    lane_head_c = lax.broadcasted_iota(jnp.int32, (HG_CHUNK, D_C), 1) >> 6
    st = st_ref[...]
    o_inter = []
    for n in range(n_ch):
        r0, r1 = n * HG_CHUNK, (n + 1) * HG_CHUNK
        read = _dot(jnp.concatenate([a[r0:r1] for a in qb_h], axis=0), st.astype(BF16))
        o_n = jnp.zeros((HG_CHUNK, D_C), F32)
        for hh in range(H_C):
            o_n = o_n + jnp.where(lane_head_c == hh, read[hh * HG_CHUNK:(hh + 1) * HG_CHUNK], 0.0)
        o_inter.append(o_n)
        upd = _tdot(jnp.concatenate([a[r0:r1] for a in kb_h], axis=0),
                    jnp.concatenate([a[r0:r1] for a in im_h], axis=0))
        col = eb_t[:, r1 - 1:r1]
        decay = jnp.concatenate(
            [jnp.broadcast_to(col[hh * DK_C:(hh + 1) * DK_C], (DK_C, DV_C)) for hh in range(H_C)], axis=1)
        st = decay * st + upd
    st_ref[...] = st
    o_c = o_c + jnp.concatenate(o_inter, axis=0)

    @pl.when(t_idx == n_t - 1)
    def _():
        for hh in range(H_C):
            st_out_ref[hh * DK_C:(hh + 1) * DK_C, :] = st[:, hh * DV_C:(hh + 1) * DV_C]

    ms = _head_sum(o_c * o_c, ones_bd) * (1.0 / DV_C)
    o_c = o_c * lax.rsqrt(ms + RMS_EPS) * ng_ref[...] * _silu(proj[:, OFF_CG:])

    o_cat = jnp.concatenate([o_a, o_b, o_c], axis=-1).astype(BF16)
    br = tq // MIX_OUT_BLOCKS if tq % (MIX_OUT_BLOCKS * MXU_DIM) == 0 else tq
    for r in range(0, tq, br):
        z = alpha * x[r:r + br] + (1.0 + gate) * _dot(o_cat[r:r + br], wout_ref[...])
        y_ref[r:r + br, :] = _layer_norm(z, lng_ref[...], lnb_ref[...])


def _mixer_prompt(x, mods, layer, w_in, w_out, tables, sinks, gln_g, gln_b, ws, bs_exp, lb, ng_exp,
                  ln_g, ln_b, *, batch, seq, alpha):
    m, d = x.shape
    tq = min(MIX_ROWS, seq)
    assert seq % tq == 0 and tq % WINDOW == 0
    n_t = seq // tq
    depth = w_in.shape[0]
    cos, s_lo, s_hi = tables
    mod_spec = lambda j: pl.BlockSpec((None, MOD_ROWS, d), lambda b, t: (layer, 0, j))
    tab_spec = pl.BlockSpec((tq, LANES), lambda b, t: (t, 0))
    lay3 = lambda shape: pl.BlockSpec((None,) + shape, lambda b, t: (layer,) + (0,) * len(shape))
    return pl.pallas_call(
        functools.partial(_mixer_prompt_kernel, layer=layer, tq=tq, alpha=alpha),
        grid=(batch, n_t),
        in_specs=[
            pl.BlockSpec((tq, d), lambda b, t: (b * n_t + t, 0)),
            mod_spec(3), mod_spec(4), mod_spec(5),
            lay3((d, D_IN)), lay3((d, d)),
            tab_spec, tab_spec, tab_spec,
            pl.BlockSpec(memory_space=pltpu.SMEM),
            lay3((1, D_B)), lay3((1, D_B)),
            lay3((G_B, CHUNK_B, CHUNK_B)), lay3((CHUNK_B, D_B)),
            pl.BlockSpec((depth, D_C), lambda b, t: (0, 0)),
            lay3((1, D_C)),
            pl.BlockSpec((None, 1, d), lambda b, t: (layer * 3 + 1, 0, 0)),
            pl.BlockSpec((None, 1, d), lambda b, t: (layer * 3 + 1, 0, 0)),
        ],
        out_specs=[
            pl.BlockSpec((tq, d), lambda b, t: (b * n_t + t, 0)),
            pl.BlockSpec((None, WINDOW, D_KV), lambda b, t: (b, 0, 0)),
            pl.BlockSpec((None, WINDOW, D_KV), lambda b, t: (b, 0, 0)),
            pl.BlockSpec((None, D_C, DV_C), lambda b, t: (b, 0, 0)),
        ],
        out_shape=[
            jax.ShapeDtypeStruct((m, d), F32),
            jax.ShapeDtypeStruct((batch, WINDOW, D_KV), F32),
            jax.ShapeDtypeStruct((batch, WINDOW, D_KV), F32),
            jax.ShapeDtypeStruct((batch, D_C, DV_C), F32),
        ],
        scratch_shapes=[
            pltpu.VMEM((WINDOW, D_KV), F32),
            pltpu.VMEM((WINDOW, D_KV), F32),
            pltpu.VMEM((DK_C, D_C), F32),
        ],
        compiler_params=pltpu.CompilerParams(
            dimension_semantics=("arbitrary", "arbitrary"), vmem_limit_bytes=VMEM_LIMIT_BYTES),
        name="mixer_prompt",
    )(x, mods, mods, mods, w_in, w_out, cos, s_lo, s_hi, sinks, gln_g, gln_b, ws, bs_exp, lb, ng_exp,
      ln_g, ln_b)


def _mixer_sample_kernel(
        x_ref, sh_ref, sc_ref, gt_ref, win_ref, wout_ref, cos_ref, sl_ref, sh2_ref, sink_ref,
        gg_ref, gb_ref, w0_ref, bs_ref, lb_ref, ng_ref, lng_ref, lnb_ref, ck_ref, cv_ref, s0_ref,
        y_ref, knew_ref, vnew_ref, gv_ref, s1_ref,
        qt_ref, kn_ref, vn_ref, sn_ref, sc_scr, ot_ref, ft_ref, kt_ref, qc_ref, it_ref, ob_ref, og_ref, oc_ref,
        *, layer, n_s, nb, alpha):
    step = pl.program_id(0)
    rows_all = n_s * H_A

    @pl.when(step == 0)
    def _():
        x = x_ref[...]
        h = (x * (1.0 + sc_ref[...]) + sh_ref[...]).astype(BF16)
        proj = _dot(h, win_ref[...])
        cos, s_lo, s_hi = cos_ref[...], sl_ref[...], sh2_ref[...]
        q = _rope(proj[:, OFF_Q:OFF_K], cos, s_lo, s_hi) * (HEAD_DIM ** -0.5)
        k = _rope(proj[:, OFF_K:OFF_V], cos, s_lo, s_hi)
        v = proj[:, OFF_V:OFF_U]
        knew_ref[...] = k
        vnew_ref[...] = v
        zeros64 = jnp.zeros((n_s, HEAD_DIM), F32)
        for hd in range(H_A):
            qh = q[:, hd * HEAD_DIM:(hd + 1) * HEAD_DIM]
            row = jnp.concatenate([qh, zeros64] if hd < G_A else [zeros64, qh], axis=-1)
            qt_ref[pl.ds(hd, n_s, stride=H_A), :] = row
            kn_ref[pl.ds(hd, n_s, stride=H_A), :] = k
            vn_ref[pl.ds(hd, n_s, stride=H_A), :] = v
        s_new = jnp.sum(qt_ref[...] * kn_ref[...], axis=-1, keepdims=True)
        sn_ref[...] = jnp.broadcast_to(s_new, (rows_all, LANES))

        vn = _layer_norm(proj[:, OFF_GV:OFF_CQ], gg_ref[...], gb_ref[...])
        gv_ref[...] = vn
        ob_ref[...] = proj[:, OFF_U:OFF_GV] * (vn * w0_ref[...] + bs_ref[0:1, :])

        lb = _hgrn_lower_bound(lb_ref, layer)
        f = lb + (1.0 - lb) * jax.nn.sigmoid(proj[:, OFF_CF:OFF_CI])
        ft_ref[...] = f.T
        kt_ref[...] = (1.0 - f).T
        qc_ref[...] = proj[:, OFF_CQ:OFF_CF].T
        it_ref[...] = proj[:, OFF_CI:OFF_CG].T
        og_ref[...] = ng_ref[...] * _silu(proj[:, OFF_CG:])

    base = pl.multiple_of(step * (nb * H_A), nb * H_A)
    for b in range(nb):
        qrows = qt_ref[pl.ds(base + b * H_A, H_A), :].astype(BF16)
        sc_scr[b * H_A:(b + 1) * H_A, :] = _dot(qrows, ck_ref[b].astype(BF16))
    lane = lax.broadcasted_iota(jnp.int32, (nb * H_A, WINDOW), 1)
    s = jnp.where(lane == 0, NEG_BIG, sc_scr[...])
    s_new = sn_ref[pl.ds(base, nb * H_A), 0:1]
    hrow = lax.broadcasted_iota(jnp.int32, (nb * H_A, 1), 0) & (H_A - 1)
    sink = jnp.zeros((nb * H_A, 1), F32)
    for hd in range(H_A):
        sink = jnp.where(hrow == hd, sink_ref[layer, hd], sink)
    mx = jnp.maximum(jnp.maximum(jnp.max(s, axis=-1, keepdims=True), s_new), sink)
    p = jnp.exp(s - mx)
    p_new = jnp.exp(s_new - mx)
    den = jnp.sum(p, axis=-1, keepdims=True) + p_new + jnp.exp(sink - mx)
    sc_scr[...] = p
    for b in range(nb):
        prow = sc_scr[b * H_A:(b + 1) * H_A, :].astype(BF16)
        ot_ref[pl.ds(base + b * H_A, H_A), :] = _dot_t(prow, cv_ref[b].astype(BF16))
    rows = pl.ds(base, nb * H_A)
    ot_ref[rows, :] = (ot_ref[rows, :] + p_new * vn_ref[rows, :]) * (1.0 / den)

    hrow0 = pl.multiple_of(step * DK_C, DK_C)
    i_h = it_ref[pl.ds(hrow0, DV_C), :]
    acc = jnp.zeros((DV_C, n_s), F32)
    for kk in range(DK_C):
        f_k = ft_ref[pl.ds(hrow0 + kk, 1), :]
        k_k = kt_ref[pl.ds(hrow0 + kk, 1), :]
        q_k = qc_ref[pl.ds(hrow0 + kk, 1), :]
        s1 = f_k * s0_ref[kk] + k_k * i_h
        s1_ref[kk] = s1
        acc = acc + q_k * s1
    oc_ref[pl.ds(hrow0, DV_C), :] = acc

    @pl.when(step == pl.num_programs(0) - 1)
    def _():
        o_a = []
        for hd in range(H_A):
            r = ot_ref[pl.ds(hd, n_s, stride=H_A), :]
            o_a.append(r[:, :HEAD_DIM] if hd < G_A else r[:, HEAD_DIM:])
        o_c = oc_ref[...].T
        ms = _head_sum(o_c * o_c, _head_block_ones()) * (1.0 / DV_C)
        o_c = o_c * lax.rsqrt(ms + RMS_EPS) * og_ref[...]
        mix = _dot(jnp.concatenate(o_a + [ob_ref[...], o_c], axis=-1).astype(BF16), wout_ref[...])
        z = alpha * x_ref[...] + (1.0 + gt_ref[...]) * mix
        y_ref[...] = _layer_norm(z, lng_ref[...], lnb_ref[...])


def _mixer_sample(x, mods, layer, w_in, w_out, tables, sinks, gln_g, gln_b, w0_exp, bs_exp, lb, ng_exp,
                  ln_g, ln_b, cache_k, cache_v, state, *, alpha):
    n_s, d = x.shape
    assert n_s % (H_C * SUBLANES) == 0 and n_s % LANES == 0
    nb = n_s // H_C
    depth = w_in.shape[0]
    cos, s_lo, s_hi = tables
    mod_spec = lambda j: pl.BlockSpec((None, n_s, d), lambda i: (layer, 0, j))
    tab_spec = pl.BlockSpec((1, LANES), lambda i: (0, 0))
    lay3 = lambda shape: pl.BlockSpec((None,) + shape, lambda i: (layer,) + (0,) * len(shape))
    rows = pl.BlockSpec((n_s, d), lambda i: (0, 0))
    per_head = pltpu.VMEM((D_C, n_s), F32)
    per_row = pltpu.VMEM((n_s * H_A, LANES), F32)
    return pl.pallas_call(
        functools.partial(_mixer_sample_kernel, layer=layer, n_s=n_s, nb=nb, alpha=alpha),
        grid=(H_C,),
        in_specs=[
            rows,
            mod_spec(3), mod_spec(4), mod_spec(5),
            lay3((d, D_IN)), lay3((d, d)),
            tab_spec, tab_spec, tab_spec,
            pl.BlockSpec(memory_space=pltpu.SMEM),
            lay3((1, D_B)), lay3((1, D_B)), lay3((1, D_B)), lay3((CHUNK_B, D_B)),
            pl.BlockSpec((depth, D_C), lambda i: (0, 0)),
            lay3((1, D_C)),
            pl.BlockSpec((None, 1, d), lambda i: (layer * 3 + 1, 0, 0)),
            pl.BlockSpec((None, 1, d), lambda i: (layer * 3 + 1, 0, 0)),
            pl.BlockSpec((None, nb, D_KV, WINDOW), lambda i: (layer, i, 0, 0)),
            pl.BlockSpec((None, nb, D_KV, WINDOW), lambda i: (layer, i, 0, 0)),
            pl.BlockSpec((None, None, DK_C, DV_C, n_s), lambda i: (layer, i, 0, 0, 0)),
        ],
        out_specs=[
            rows,
            pl.BlockSpec((n_s, D_KV), lambda i: (0, 0)),
            pl.BlockSpec((n_s, D_KV), lambda i: (0, 0)),
            pl.BlockSpec((n_s, D_B), lambda i: (0, 0)),
            pl.BlockSpec((None, DK_C, DV_C, n_s), lambda i: (i, 0, 0, 0)),
        ],
        out_shape=[
            jax.ShapeDtypeStruct((n_s, d), F32),
            jax.ShapeDtypeStruct((n_s, D_KV), F32),
            jax.ShapeDtypeStruct((n_s, D_KV), F32),
            jax.ShapeDtypeStruct((n_s, D_B), F32),
            jax.ShapeDtypeStruct((H_C, DK_C, DV_C, n_s), F32),
        ],
        scratch_shapes=[
            per_row,
            per_row,
            per_row,
            per_row,
            pltpu.VMEM((nb * H_A, WINDOW), F32),
            per_row,
            per_head, per_head, per_head, per_head,
            pltpu.VMEM((n_s, D_B), F32),
            pltpu.VMEM((n_s, D_C), F32),
            per_head,
        ],
        compiler_params=pltpu.CompilerParams(
            dimension_semantics=("arbitrary",), vmem_limit_bytes=VMEM_LIMIT_BYTES),
        name="mixer_sample",
    )(x, mods, mods, mods, w_in, w_out, cos, s_lo, s_hi, sinks, gln_g, gln_b, w0_exp, bs_exp, lb, ng_exp,
      ln_g, ln_b, cache_k, cache_v, state)


def _rope_tables(positions):
    half = ROT_DIM // 2
    f32 = np.float32
    inv = (ROPE_THETA ** (-np.arange(half, dtype=f32) * 2.0 / ROT_DIM)).astype(f32)
    ang = (positions.astype(f32)[:, None] * inv[None, :]).astype(f32)
    cos, sin = np.cos(ang).astype(f32), np.sin(ang).astype(f32)
    n = positions.shape[0]
    rest = HEAD_DIM - ROT_DIM
    cos_h = np.concatenate([cos, cos, np.ones((n, rest), f32)], axis=-1)
    lo_h = np.concatenate([-sin, np.zeros((n, half + rest), f32)], axis=-1)
    hi_h = np.concatenate([np.zeros((n, half), f32), sin, np.zeros((n, rest), f32)], axis=-1)
    two = lambda a: jnp.asarray(np.concatenate([a, a], axis=-1))
    return two(cos_h), two(lo_h), two(hi_h)


def kernel(x_prompt, x_sample, cache_k, cache_v, state_hgrn, c_prompt, c_sample, w_in, w_out, attn_sinks,
           gmlp_ln_g, gmlp_ln_b, gmlp_ws, gmlp_bs, hgrn_lb, hgrn_norm_g, ffn1_in, ffn1_out, ffn2_in,
           ffn2_out, ada_w, ada_b, ln_g, ln_b):
    batch, seq, d = x_prompt.shape
    n_s, dec_seq, _ = x_sample.shape
    depth = w_in.shape[0]
    assert d == D_MODEL and dec_seq == 1 and batch <= MOD_ROWS
    assert cache_k.shape[2] == WINDOW
    alpha = (2 * depth) ** 0.25

    w_in_b, w_out_b = w_in.astype(BF16), w_out.astype(BF16)
    ln_g3 = ln_g.reshape(depth * 3, 1, d)
    ln_b3 = ln_b.reshape(depth * 3, 1, d)
    gln_g = gmlp_ln_g.reshape(depth, 1, D_B)
    gln_b = gmlp_ln_b.reshape(depth, 1, D_B)
    bs_exp = jnp.repeat(jnp.swapaxes(gmlp_bs, 1, 2), HEAD_DIM, axis=2)
    w0_exp = jnp.repeat(gmlp_ws[:, :, 0, 0], HEAD_DIM, axis=1).reshape(depth, 1, D_B)
    ng_exp = jnp.tile(hgrn_norm_g, (1, H_C)).reshape(depth, 1, D_C)
    tab_p = _rope_tables(np.arange(seq))
    tab_s = _rope_tables(PAST_LEN + np.arange(dec_seq))

    c_all = jnp.concatenate([c_prompt, jnp.zeros((MOD_ROWS - batch, d), F32), c_sample], axis=0)
    mod_p, mod_s = _adaln(c_all, ada_w, ada_b)

    ck = jnp.transpose(cache_k, (0, 1, 3, 4, 2)).reshape(depth, n_s, D_KV, WINDOW)
    cv = jnp.transpose(cache_v, (0, 1, 3, 4, 2)).reshape(depth, n_s, D_KV, WINDOW)
    s0 = jnp.transpose(state_hgrn, (0, 2, 3, 4, 1))

    xp = x_prompt.reshape(batch * seq, d)
    xs = x_sample.reshape(n_s, d)
    ffn = functools.partial(_ffn, rows_per_batch=seq, alpha=alpha)
    kw, vw, st_p, kn, vnw, gvs, st_s = [], [], [], [], [], [], []
    for l in range(depth):
        xp, xs = ffn(xp, xs, mod_p, mod_s, l, 0, ffn1_in, ffn1_out, ln_g3, ln_b3)
        xp, k_l, v_l, s_l = _mixer_prompt(
            xp, mod_p, l, w_in_b, w_out_b, tab_p, attn_sinks, gln_g, gln_b, gmlp_ws, bs_exp, hgrn_lb,
            ng_exp, ln_g3, ln_b3, batch=batch, seq=seq, alpha=alpha)
        kw.append(k_l), vw.append(v_l), st_p.append(s_l)
        xs, k_l, v_l, g_l, s_l = _mixer_sample(
            xs, mod_s, l, w_in_b, w_out_b, tab_s, attn_sinks, gln_g, gln_b, w0_exp, bs_exp, hgrn_lb,
            ng_exp, ln_g3, ln_b3, ck, cv, s0, alpha=alpha)
        kn.append(k_l), vnw.append(v_l), gvs.append(g_l), st_s.append(s_l)
        xp, xs = ffn(xp, xs, mod_p, mod_s, l, 2, ffn2_in, ffn2_out, ln_g3, ln_b3)

    stack = lambda parts, shape: jnp.stack(parts, axis=0).reshape((depth,) + shape)
    return (
        xp.reshape(batch, seq, d),
        xs.reshape(n_s, dec_seq, d),
        stack(kw, (batch, WINDOW, KV_A, HEAD_DIM)),
        stack(vw, (batch, WINDOW, KV_A, HEAD_DIM)),
        stack(st_p, (batch, H_C, DK_C, DV_C)),
        stack(kn, (n_s, dec_seq, KV_A, HEAD_DIM)),
        stack(vnw, (n_s, dec_seq, KV_A, HEAD_DIM)),
        stack(gvs, (n_s, dec_seq, D_B)),
        jnp.transpose(jnp.stack(st_s, axis=0), (0, 4, 1, 2, 3)),
    )
```

```python
import functools

import numpy as np
import jax
import jax.numpy as jnp
from jax import lax
from jax.experimental import pallas as pl
from jax.experimental.pallas import tpu as pltpu

F32 = jnp.float32
BF16 = jnp.bfloat16

D_MODEL = 1024
HEAD_DIM = 64
H_A = 8
KV_A = 2
G_A = H_A // KV_A
D_A = H_A * HEAD_DIM
D_KV = KV_A * HEAD_DIM
WINDOW = 128
ROT_DIM = HEAD_DIM // 4
ROPE_THETA = 500000.0
G_B = 4
D_B = G_B * HEAD_DIM
CHUNK_B = 128
H_C = 4
DK_C = 64
DV_C = 64
D_C = H_C * DV_C
HEAD_SHIFT = DV_C.bit_length() - 1
assert 1 << HEAD_SHIFT == DV_C == DK_C
D_IN = 2304
D_FF = 2816
N_MOD = 9
LN_EPS = 1e-5
RMS_EPS = 1e-6
NEG_BIG = -1e30
LOG2_E = 1.4426950408889634
PAST_LEN = 16384

OFF_Q = 0
OFF_K = OFF_Q + D_A
OFF_V = OFF_K + D_KV
OFF_U = OFF_V + D_KV
OFF_GV = OFF_U + D_B
OFF_CQ = OFF_GV + D_B
OFF_CF = OFF_CQ + D_C
OFF_CI = OFF_CF + D_C
OFF_CG = OFF_CI + D_C

SUBLANES = 8
LANES = 128
MXU_DIM = 256
VMEM_LIMIT_BYTES = 56 * 1024 * 1024

FFN_ROWS = 1024
FFN_OUT_BLOCKS = 4
MIX_OUT_BLOCKS = 2
FF_CHUNK = MXU_DIM
FF_CONV_STEPS = 11
FF_CONV_COLS = 2 * D_FF // FF_CONV_STEPS
FF_CONV_ROWS = D_FF // FF_CONV_STEPS
ADA_COLS = 2304
MIX_ROWS = 512
HG_CHUNK = 16
MOD_ROWS = SUBLANES


def _dot(a, b):
    return jnp.dot(a, b, preferred_element_type=F32)


def _dot_t(a, b):
    return lax.dot_general(a, b, (((1,), (1,)), ((), ())), preferred_element_type=F32)


def _tdot(a, b):
    return lax.dot_general(a, b, (((0,), (0,)), ((), ())), preferred_element_type=F32)


def _layer_norm(z, g, b):
    mu = jnp.mean(z, axis=-1, keepdims=True)
    zc = z - mu
    var = jnp.mean(zc * zc, axis=-1, keepdims=True)
    return zc * lax.rsqrt(var + LN_EPS) * g + b


def _silu(x):
    return x * jax.nn.sigmoid(x)


def _head_block_ones():
    r = lax.broadcasted_iota(jnp.int32, (D_C, D_C), 0) >> HEAD_SHIFT
    c = lax.broadcasted_iota(jnp.int32, (D_C, D_C), 1) >> HEAD_SHIFT
    return jnp.where(r == c, 1.0, 0.0).astype(BF16)


def _head_sum(x, ones_bd):
    hi = x.astype(BF16)
    lo = (x - hi.astype(F32)).astype(BF16)
    return _dot(hi, ones_bd) + _dot(lo, ones_bd)


def _rope(x, cos, sin_lo, sin_hi):
    half = ROT_DIM // 2
    out = []
    for c in range(x.shape[-1] // LANES):
        xc = x[:, c * LANES:(c + 1) * LANES]
        out.append(xc * cos + pltpu.roll(xc, LANES - half, 1) * sin_lo + pltpu.roll(xc, half, 1) * sin_hi)
    return out[0] if len(out) == 1 else jnp.concatenate(out, axis=-1)


def _hgrn_lower_bound(lb_ref, layer):
    w = lb_ref[...]
    e = jnp.exp(w - jnp.max(w, axis=0, keepdims=True))
    p = e / jnp.sum(e, axis=0, keepdims=True)
    lb = jnp.zeros((1, D_C), F32)
    for j in range(1, layer + 1):
        lb = lb + p[j:j + 1, :]
    return lb


def _adaln_kernel(c_ref, w_ref, b_ref, op_ref, os_ref):
    s = _silu(c_ref[...]).astype(BF16)
    r = _dot(s, w_ref[...].astype(BF16)) + b_ref[...]
    op_ref[...] = r[:MOD_ROWS]
    os_ref[...] = r[MOD_ROWS:]


def _adaln(c_all, ada_w, ada_b):
    depth, d, n = ada_w.shape
    rows = c_all.shape[0]
    n_s = rows - MOD_ROWS
    return pl.pallas_call(
        _adaln_kernel,
        grid=(depth, n // ADA_COLS),
        in_specs=[
            pl.BlockSpec((rows, d), lambda l, j: (0, 0)),
            pl.BlockSpec((None, d, ADA_COLS), lambda l, j: (l, 0, j)),
            pl.BlockSpec((None, 1, ADA_COLS), lambda l, j: (l, 0, j)),
        ],
        out_specs=[
            pl.BlockSpec((None, MOD_ROWS, ADA_COLS), lambda l, j: (l, 0, j)),
            pl.BlockSpec((None, n_s, ADA_COLS), lambda l, j: (l, 0, j)),
        ],
        out_shape=[
            jax.ShapeDtypeStruct((depth, MOD_ROWS, n), F32),
            jax.ShapeDtypeStruct((depth, n_s, n), F32),
        ],
        compiler_params=pltpu.CompilerParams(
            dimension_semantics=("arbitrary", "arbitrary"), vmem_limit_bytes=VMEM_LIMIT_BYTES),
        name="adaln",
    )(c_all, ada_w, ada_b.reshape(depth, 1, n))


def _ffn_rows(x, shift, scale, gate, wi_ref, wo_ref, act_ref, lng, lnb, alpha):
    rows = x.shape[0]
    h = (x * (1.0 + scale) + shift).astype(BF16)
    for c in range(D_FF // FF_CHUNK):
        lo = c * FF_CHUNK
        ga = D_FF + lo
        a = _dot(h, wi_ref[lo // FF_CONV_COLS, :, lo % FF_CONV_COLS:lo % FF_CONV_COLS + FF_CHUNK])
        g = _dot(h, wi_ref[ga // FF_CONV_COLS, :, ga % FF_CONV_COLS:ga % FF_CONV_COLS + FF_CHUNK])
        act_ref[0:rows, lo:lo + FF_CHUNK] = (_silu(g) * a).astype(BF16)
    br = rows // FFN_OUT_BLOCKS if rows % (FFN_OUT_BLOCKS * MXU_DIM) == 0 else rows
    half_gate = 0.5 * (1.0 + gate)
    out = []
    for r in range(0, rows, br):
        y = _dot(act_ref[r:r + br, :], wo_ref[...])
        hg = half_gate if half_gate.shape[0] == 1 else half_gate[r:r + br]
        out.append(_layer_norm(alpha * x[r:r + br] + hg * y, lng, lnb))
    return out[0] if len(out) == 1 else jnp.concatenate(out, axis=0)


def _ffn_kernel(xp_ref, xs_ref, shp_ref, scp_ref, gtp_ref, shs_ref, scs_ref, gts_ref, win_ref, wout_ref,
                lng_ref, lnb_ref, op_ref, os_ref, wi_ref, wo_ref, act_ref, *, n_tiles, tiles_per_batch, alpha):
    step = pl.program_id(0)

    @pl.when(step < FF_CONV_STEPS)
    def _():
        wi_ref[step] = win_ref[...].astype(BF16)
        r0 = pl.multiple_of(step * FF_CONV_ROWS, FF_CONV_ROWS)
        wo_ref[pl.ds(r0, FF_CONV_ROWS), :] = wout_ref[...].astype(BF16)

    @pl.when(jnp.logical_and(step >= FF_CONV_STEPS, step < FF_CONV_STEPS + n_tiles))
    def _():
        batch = (step - FF_CONV_STEPS) // tiles_per_batch
        op_ref[...] = _ffn_rows(
            xp_ref[...], shp_ref[pl.ds(batch, 1), :], scp_ref[pl.ds(batch, 1), :], gtp_ref[pl.ds(batch, 1), :],
            wi_ref, wo_ref, act_ref, lng_ref[...], lnb_ref[...], alpha)

    @pl.when(step == FF_CONV_STEPS + n_tiles)
    def _():
        os_ref[...] = _ffn_rows(
            xs_ref[...], shs_ref[...], scs_ref[...], gts_ref[...],
            wi_ref, wo_ref, act_ref, lng_ref[...], lnb_ref[...], alpha)


def _ffn(xp, xs, mod_p, mod_s, layer, sub, w_in, w_out, ln_g, ln_b, *, rows_per_batch, alpha):
    m, d = xp.shape
    n_s = xs.shape[0]
    tm = min(FFN_ROWS, rows_per_batch)
    assert m % tm == 0 and rows_per_batch % tm == 0 and n_s <= tm
    n_tiles = m // tm
    last_conv = FF_CONV_STEPS - 1
    tile = lambda i: jnp.clip(i - FF_CONV_STEPS, 0, n_tiles - 1)
    mod_p_spec = lambda j: pl.BlockSpec((None, MOD_ROWS, d), lambda i: (layer, 0, j))
    mod_s_spec = lambda j: pl.BlockSpec((None, n_s, d), lambda i: (layer, 0, j))
    return pl.pallas_call(
        functools.partial(_ffn_kernel, n_tiles=n_tiles, tiles_per_batch=rows_per_batch // tm, alpha=alpha),
        grid=(FF_CONV_STEPS + n_tiles + 1,),
        in_specs=[
            pl.BlockSpec((tm, d), lambda i: (tile(i), 0)),
            pl.BlockSpec((n_s, d), lambda i: (0, 0)),
            mod_p_spec(3 * sub), mod_p_spec(3 * sub + 1), mod_p_spec(3 * sub + 2),
            mod_s_spec(3 * sub), mod_s_spec(3 * sub + 1), mod_s_spec(3 * sub + 2),
            pl.BlockSpec((None, d, FF_CONV_COLS), lambda i: (layer, 0, jnp.minimum(i, last_conv))),
            pl.BlockSpec((None, FF_CONV_ROWS, d), lambda i: (layer, jnp.minimum(i, last_conv), 0)),
            pl.BlockSpec((None, 1, d), lambda i: (layer * 3 + sub, 0, 0)),
            pl.BlockSpec((None, 1, d), lambda i: (layer * 3 + sub, 0, 0)),
        ],
        out_specs=[
            pl.BlockSpec((tm, d), lambda i: (tile(i), 0)),
            pl.BlockSpec((n_s, d), lambda i: (0, 0)),
        ],
        out_shape=[jax.ShapeDtypeStruct((m, d), F32), jax.ShapeDtypeStruct((n_s, d), F32)],
        scratch_shapes=[
            pltpu.VMEM((FF_CONV_STEPS, d, FF_CONV_COLS), BF16),
            pltpu.VMEM((D_FF, d), BF16),
            pltpu.VMEM((tm, D_FF), BF16),
        ],
        compiler_params=pltpu.CompilerParams(
            dimension_semantics=("arbitrary",), vmem_limit_bytes=VMEM_LIMIT_BYTES),
        name="ffn",
    )(xp, xs, mod_p, mod_p, mod_p, mod_s, mod_s, mod_s, w_in, w_out, ln_g, ln_b)


def _attention_block(q_blk, keys, vals, mask, fill):
    outs = []
    for j in range(KV_A):
        kj = keys[:, j * HEAD_DIM:(j + 1) * HEAD_DIM].astype(BF16)
        vj = vals[:, j * HEAD_DIM:(j + 1) * HEAD_DIM].astype(BF16)
        qj = jnp.concatenate(
            [q_blk[:, (j * G_A + g) * HEAD_DIM:(j * G_A + g + 1) * HEAD_DIM] for g in range(G_A)],
            axis=0).astype(BF16)
        s = _dot_t(qj, kj)
        s = jnp.concatenate(
            [jnp.where(mask, s[g * WINDOW:(g + 1) * WINDOW], fill[j][g]) for g in range(G_A)], axis=0)
        p = jnp.exp2(s - jnp.max(s, axis=-1, keepdims=True))
        den = jnp.sum(p, axis=-1, keepdims=True)
        o = _dot(p.astype(BF16), vj) * (1.0 / den)
        outs.extend(o[g * WINDOW:(g + 1) * WINDOW] for g in range(G_A))
    return jnp.concatenate(outs, axis=-1)


def _mixer_prompt_kernel(
        x_ref, sh_ref, sc_ref, gt_ref, win_ref, wout_ref, cos_ref, sl_ref, sh2_ref, sink_ref,
        gg_ref, gb_ref, ws_ref, bs_ref, lb_ref, ng_ref, lng_ref, lnb_ref,
        y_ref, kwin_ref, vwin_ref, st_out_ref,
        kprev_ref, vprev_ref, st_ref,
        *, layer, tq, alpha):
    b_idx = pl.program_id(0)
    t_idx = pl.program_id(1)
    n_t = pl.num_programs(1)

    @pl.when(t_idx == 0)
    def _():
        kprev_ref[...] = jnp.zeros_like(kprev_ref)
        vprev_ref[...] = jnp.zeros_like(vprev_ref)
        st_ref[...] = jnp.zeros_like(st_ref)

    x = x_ref[...]
    shift = sh_ref[pl.ds(b_idx, 1), :]
    scale = sc_ref[pl.ds(b_idx, 1), :]
    gate = gt_ref[pl.ds(b_idx, 1), :]
    h = (x * (1.0 + scale) + shift).astype(BF16)
    proj = _dot(h, win_ref[...])

    cos, s_lo, s_hi = cos_ref[...], sl_ref[...], sh2_ref[...]
    q = _rope(proj[:, OFF_Q:OFF_K], cos, s_lo, s_hi) * (HEAD_DIM ** -0.5 * LOG2_E)
    k = _rope(proj[:, OFF_K:OFF_V], cos, s_lo, s_hi)
    v = proj[:, OFF_V:OFF_U]

    qi = lax.broadcasted_iota(jnp.int32, (WINDOW, 2 * WINDOW), 0)
    kj = lax.broadcasted_iota(jnp.int32, (WINDOW, 2 * WINDOW), 1)
    dist = kj - qi
    band = (dist - 1).astype(jnp.uint32) < jnp.uint32(WINDOW)
    col0 = lax.broadcasted_iota(jnp.int32, (1, 2 * WINDOW), 1) == 0
    fill = [[jnp.where(col0, sink_ref[layer, j * G_A + g] * LOG2_E, NEG_BIG) for g in range(G_A)]
            for j in range(KV_A)]
    first_row = lax.broadcasted_iota(jnp.int32, (WINDOW, D_KV), 0) == 0
    o_a = []
    for blk in range(tq // WINDOW):
        r0 = blk * WINDOW
        if blk == 0:
            k_prev, v_prev = kprev_ref[...], vprev_ref[...]
            lower = jnp.where(t_idx == 0, WINDOW, 0)
            mask = jnp.logical_and(band, kj >= lower)
        else:
            k_prev, v_prev = k[r0 - WINDOW:r0], v[r0 - WINDOW:r0]
            mask = band
        keys = jnp.concatenate([k_prev, k[r0:r0 + WINDOW]], axis=0)
        vals = jnp.concatenate([jnp.where(first_row, 0.0, v_prev), v[r0:r0 + WINDOW]], axis=0)
        o_a.append(_attention_block(q[r0:r0 + WINDOW], keys, vals, mask, fill))
    o_a = jnp.concatenate(o_a, axis=0)
    kprev_ref[...] = k[tq - WINDOW:]
    vprev_ref[...] = v[tq - WINDOW:]
    kwin_ref[...] = k[tq - WINDOW:]
    vwin_ref[...] = v[tq - WINDOW:]

    vn = _layer_norm(proj[:, OFF_GV:OFF_CQ], gg_ref[...], gb_ref[...])
    u = proj[:, OFF_U:OFF_GV]
    tr = lax.broadcasted_iota(jnp.int32, (CHUNK_B, CHUNK_B), 0)
    tc = lax.broadcasted_iota(jnp.int32, (CHUNK_B, CHUNK_B), 1)
    wm = [jnp.where(tr >= tc, ws_ref[g], 0.0).astype(BF16) for g in range(G_B)]
    o_b = []
    for c in range(tq // CHUNK_B):
        vc = vn[c * CHUNK_B:(c + 1) * CHUNK_B].astype(BF16)
        mixed = jnp.concatenate(
            [_dot(wm[g], vc[:, g * HEAD_DIM:(g + 1) * HEAD_DIM]) for g in range(G_B)], axis=-1)
        o_b.append(u[c * CHUNK_B:(c + 1) * CHUNK_B] * (mixed + bs_ref[...]))
    o_b = jnp.concatenate(o_b, axis=0)

    n_ch = tq // HG_CHUNK
    half = HG_CHUNK // 2
    cq = proj[:, OFF_CQ:OFF_CF]
    ci = proj[:, OFF_CI:OFF_CG]
    lb = _hgrn_lower_bound(lb_ref, layer)
    f = lb + (1.0 - lb) * jax.nn.sigmoid(proj[:, OFF_CF:OFF_CI])
    lf = jnp.log(f)
    kc = 1.0 - f

    sr = min(tq, MXU_DIM)
    rt = lax.broadcasted_iota(jnp.int32, (sr, sr), 0)
    ct = lax.broadcasted_iota(jnp.int32, (sr, sr), 1)
    ltri = jnp.where(ct <= rt, jnp.where(ct >= (rt & -HG_CHUNK), 1.0, 0.0), 0.0).astype(BF16)
    lf_hi = lf.astype(BF16)
    lf_r = lf - lf_hi.astype(F32)
    lf_mid = lf_r.astype(BF16)
    lf_lo = (lf_r - lf_mid.astype(F32)).astype(BF16)
    bsum = jnp.concatenate(
        [_dot(ltri, lf_hi[r:r + sr]) + _dot(ltri, lf_mid[r:r + sr]) + _dot(ltri, lf_lo[r:r + sr])
         for r in range(0, tq, sr)], axis=0) * LOG2_E

    def chunks(a):
        return a.reshape(n_ch, HG_CHUNK, D_C)

    b3, cq3, ci3 = chunks(bsum), chunks(cq), chunks(ci)
    b_end = b3[:, HG_CHUNK - 1:HG_CHUNK, :]
    eb = jnp.exp2(bsum)
    qb = cq * eb
    kb = kc * jnp.exp2(b_end - b3).reshape(tq, D_C)

    ones_bd = _head_block_ones()
    c3 = chunks(bsum - jnp.log2(jnp.maximum(kc, 0.0)))

    def pair_products(s, lo):
        e = cq3[:, lo:] * jnp.exp2(b3[:, lo:] - c3[:, s:s + 1, :])
        pos = lax.broadcasted_iota(jnp.int32, (n_ch, half, D_C), 1) + (s // half) * half
        grp = s // half - lo // half
        parts = [e[:, g * half:(g + 1) * half] for g in range((HG_CHUNK - lo) // half)]
        parts[grp] = jnp.where(pos >= s, parts[grp], 0.0)
        return parts[0] if len(parts) == 1 else jnp.concatenate(parts, axis=1)

    def head_sums(slab):
        return chunks(_dot(slab.reshape(tq, D_C).astype(BF16), ones_bd))

    o_all = jnp.zeros((n_ch, HG_CHUNK, D_C), F32)
    for s in range(half):
        o_all = o_all + head_sums(pair_products(s, 0)) * ci3[:, s:s + 1, :]
    o_up = jnp.zeros((n_ch, half, D_C), F32)
    for s in range(half, HG_CHUNK, 2):
        a3 = head_sums(jnp.concatenate([pair_products(s, half), pair_products(s + 1, half)], axis=1))
        o_up = o_up + a3[:, :half] * ci3[:, s:s + 1, :] + a3[:, half:] * ci3[:, s + 1:s + 2, :]
    o_c = (o_all + jnp.concatenate([jnp.zeros((n_ch, half, D_C), F32), o_up], axis=1)).reshape(tq, D_C)

    lane_head = lax.broadcasted_iota(jnp.int32, (tq, D_C), 1) >> HEAD_SHIFT
    im_h = [jnp.where(lane_head == hh, ci, 0.0).astype(BF16) for hh in range(H_C)]
    qb_h = [qb[:, hh * DK_C:(hh + 1) * DK_C].astype(BF16) for hh in range(H_C)]
    kb_h = [kb[:, hh * DK_C:(hh + 1) * DK_C].astype(BF16) for hh in range(H_C)]
    eb_t = eb.T
    lane_head_c = lax.broadcasted_iota(jnp.int32, (HG_CHUNK, D_C), 1) >> HEAD_SHIFT
    st = st_ref[...]
    o_inter = []
    for n in range(n_ch):
        r0, r1 = n * HG_CHUNK, (n + 1) * HG_CHUNK
        read = _dot(jnp.concatenate([a[r0:r1] for a in qb_h], axis=0), st.astype(BF16))
        o_n = jnp.zeros((HG_CHUNK, D_C), F32)
        for hh in range(H_C):
            o_n = o_n + jnp.where(lane_head_c == hh, read[hh * HG_CHUNK:(hh + 1) * HG_CHUNK], 0.0)
        o_inter.append(o_n)
        upd = _tdot(jnp.concatenate([a[r0:r1] for a in kb_h], axis=0),
                    jnp.concatenate([a[r0:r1] for a in im_h], axis=0))
        col = eb_t[:, r1 - 1:r1]
        decay = jnp.concatenate(
            [jnp.broadcast_to(col[hh * DK_C:(hh + 1) * DK_C], (DK_C, DV_C)) for hh in range(H_C)], axis=1)
        st = decay * st + upd
    st_ref[...] = st
    o_c = o_c + jnp.concatenate(o_inter, axis=0)

    @pl.when(t_idx == n_t - 1)
    def _():
        for hh in range(H_C):
            st_out_ref[hh * DK_C:(hh + 1) * DK_C, :] = st[:, hh * DV_C:(hh + 1) * DV_C]

    ms = _head_sum(o_c * o_c, ones_bd) * (1.0 / DV_C)
    o_c = o_c * lax.rsqrt(ms + RMS_EPS) * ng_ref[...] * _silu(proj[:, OFF_CG:])

    o_cat = jnp.concatenate([o_a, o_b, o_c], axis=-1).astype(BF16)
    br = tq // MIX_OUT_BLOCKS if tq % (MIX_OUT_BLOCKS * MXU_DIM) == 0 else tq
    for r in range(0, tq, br):
        z = alpha * x[r:r + br] + (1.0 + gate) * _dot(o_cat[r:r + br], wout_ref[...])
        y_ref[r:r + br, :] = _layer_norm(z, lng_ref[...], lnb_ref[...])


def _mixer_prompt(x, mods, layer, w_in, w_out, tables, sinks, gln_g, gln_b, ws, bs_exp, lb, ng_exp,
                  ln_g, ln_b, *, batch, seq, alpha):
    m, d = x.shape
    tq = min(MIX_ROWS, seq)
    assert seq % tq == 0 and tq % WINDOW == 0
    n_t = seq // tq
    depth = w_in.shape[0]
    cos, s_lo, s_hi = tables
    mod_spec = lambda j: pl.BlockSpec((None, MOD_ROWS, d), lambda b, t: (layer, 0, j))
    tab_spec = pl.BlockSpec((tq, LANES), lambda b, t: (t, 0))
    lay3 = lambda shape: pl.BlockSpec((None,) + shape, lambda b, t: (layer,) + (0,) * len(shape))
    return pl.pallas_call(
        functools.partial(_mixer_prompt_kernel, layer=layer, tq=tq, alpha=alpha),
        grid=(batch, n_t),
        in_specs=[
            pl.BlockSpec((tq, d), lambda b, t: (b * n_t + t, 0)),
            mod_spec(3), mod_spec(4), mod_spec(5),
            lay3((d, D_IN)), lay3((d, d)),
            tab_spec, tab_spec, tab_spec,
            pl.BlockSpec(memory_space=pltpu.SMEM),
            lay3((1, D_B)), lay3((1, D_B)),
            lay3((G_B, CHUNK_B, CHUNK_B)), lay3((CHUNK_B, D_B)),
            pl.BlockSpec((depth, D_C), lambda b, t: (0, 0)),
            lay3((1, D_C)),
            pl.BlockSpec((None, 1, d), lambda b, t: (layer * 3 + 1, 0, 0)),
            pl.BlockSpec((None, 1, d), lambda b, t: (layer * 3 + 1, 0, 0)),
        ],
        out_specs=[
            pl.BlockSpec((tq, d), lambda b, t: (b * n_t + t, 0)),
            pl.BlockSpec((None, WINDOW, D_KV), lambda b, t: (b, 0, 0)),
            pl.BlockSpec((None, WINDOW, D_KV), lambda b, t: (b, 0, 0)),
            pl.BlockSpec((None, D_C, DV_C), lambda b, t: (b, 0, 0)),
        ],
        out_shape=[
            jax.ShapeDtypeStruct((m, d), F32),
            jax.ShapeDtypeStruct((batch, WINDOW, D_KV), F32),
            jax.ShapeDtypeStruct((batch, WINDOW, D_KV), F32),
            jax.ShapeDtypeStruct((batch, D_C, DV_C), F32),
        ],
        scratch_shapes=[
            pltpu.VMEM((WINDOW, D_KV), F32),
            pltpu.VMEM((WINDOW, D_KV), F32),
            pltpu.VMEM((DK_C, D_C), F32),
        ],
        compiler_params=pltpu.CompilerParams(
            dimension_semantics=("arbitrary", "arbitrary"), vmem_limit_bytes=VMEM_LIMIT_BYTES),
        name="mixer_prompt",
    )(x, mods, mods, mods, w_in, w_out, cos, s_lo, s_hi, sinks, gln_g, gln_b, ws, bs_exp, lb, ng_exp,
      ln_g, ln_b)


def _mixer_sample_kernel(
        x_ref, sh_ref, sc_ref, gt_ref, win_ref, wout_ref, cos_ref, sl_ref, sh2_ref, sink_ref,
        gg_ref, gb_ref, w0_ref, bs_ref, lb_ref, ng_ref, lng_ref, lnb_ref, ck_ref, cv_ref, s0_ref,
        y_ref, knew_ref, vnew_ref, gv_ref, s1_ref,
        qt_ref, kn_ref, vn_ref, sn_ref, sc_scr, ot_ref, ft_ref, kt_ref, qc_ref, it_ref, ob_ref, og_ref, oc_ref,
        *, layer, n_s, nb, alpha):
    step = pl.program_id(0)
    rows_all = n_s * H_A

    @pl.when(step == 0)
    def _():
        x = x_ref[...]
        h = (x * (1.0 + sc_ref[...]) + sh_ref[...]).astype(BF16)
        proj = _dot(h, win_ref[...])
        cos, s_lo, s_hi = cos_ref[...], sl_ref[...], sh2_ref[...]
        q = _rope(proj[:, OFF_Q:OFF_K], cos, s_lo, s_hi) * (HEAD_DIM ** -0.5)
        k = _rope(proj[:, OFF_K:OFF_V], cos, s_lo, s_hi)
        v = proj[:, OFF_V:OFF_U]
        knew_ref[...] = k
        vnew_ref[...] = v
        zeros64 = jnp.zeros((n_s, HEAD_DIM), F32)
        for hd in range(H_A):
            qh = q[:, hd * HEAD_DIM:(hd + 1) * HEAD_DIM]
            row = jnp.concatenate([qh, zeros64] if hd < G_A else [zeros64, qh], axis=-1)
            qt_ref[pl.ds(hd, n_s, stride=H_A), :] = row
            kn_ref[pl.ds(hd, n_s, stride=H_A), :] = k
            vn_ref[pl.ds(hd, n_s, stride=H_A), :] = v
        s_new = jnp.sum(qt_ref[...] * kn_ref[...], axis=-1, keepdims=True)
        sn_ref[...] = jnp.broadcast_to(s_new, (rows_all, LANES))

        vn = _layer_norm(proj[:, OFF_GV:OFF_CQ], gg_ref[...], gb_ref[...])
        gv_ref[...] = vn
        ob_ref[...] = proj[:, OFF_U:OFF_GV] * (vn * w0_ref[...] + bs_ref[0:1, :])

        lb = _hgrn_lower_bound(lb_ref, layer)
        f = lb + (1.0 - lb) * jax.nn.sigmoid(proj[:, OFF_CF:OFF_CI])
        ft_ref[...] = f.T
        kt_ref[...] = (1.0 - f).T
        qc_ref[...] = proj[:, OFF_CQ:OFF_CF].T
        it_ref[...] = proj[:, OFF_CI:OFF_CG].T
        og_ref[...] = ng_ref[...] * _silu(proj[:, OFF_CG:])

    base = pl.multiple_of(step * (nb * H_A), nb * H_A)
    for b in range(nb):
        qrows = qt_ref[pl.ds(base + b * H_A, H_A), :].astype(BF16)
        sc_scr[b * H_A:(b + 1) * H_A, :] = _dot(qrows, ck_ref[b].astype(BF16))
    lane = lax.broadcasted_iota(jnp.int32, (nb * H_A, WINDOW), 1)
    s = jnp.where(lane == 0, NEG_BIG, sc_scr[...])
    s_new = sn_ref[pl.ds(base, nb * H_A), 0:1]
    hrow = lax.broadcasted_iota(jnp.int32, (nb * H_A, 1), 0) & (H_A - 1)
    sink = jnp.zeros((nb * H_A, 1), F32)
    for hd in range(H_A):
        sink = jnp.where(hrow == hd, sink_ref[layer, hd], sink)
    mx = jnp.maximum(jnp.maximum(jnp.max(s, axis=-1, keepdims=True), s_new), sink)
    p = jnp.exp(s - mx)
    p_new = jnp.exp(s_new - mx)
    den = jnp.sum(p, axis=-1, keepdims=True) + p_new + jnp.exp(sink - mx)
    sc_scr[...] = p
    for b in range(nb):
        prow = sc_scr[b * H_A:(b + 1) * H_A, :].astype(BF16)
        ot_ref[pl.ds(base + b * H_A, H_A), :] = _dot_t(prow, cv_ref[b].astype(BF16))
    rows = pl.ds(base, nb * H_A)
    ot_ref[rows, :] = (ot_ref[rows, :] + p_new * vn_ref[rows, :]) * (1.0 / den)

    hrow0 = pl.multiple_of(step * DK_C, DK_C)
    i_h = it_ref[pl.ds(hrow0, DV_C), :]
    acc = jnp.zeros((DV_C, n_s), F32)
    for kk in range(DK_C):
        f_k = ft_ref[pl.ds(hrow0 + kk, 1), :]
        k_k = kt_ref[pl.ds(hrow0 + kk, 1), :]
        q_k = qc_ref[pl.ds(hrow0 + kk, 1), :]
        s1 = f_k * s0_ref[kk] + k_k * i_h
        s1_ref[kk] = s1
        acc = acc + q_k * s1
    oc_ref[pl.ds(hrow0, DV_C), :] = acc

    @pl.when(step == pl.num_programs(0) - 1)
    def _():
        o_a = []
        for hd in range(H_A):
            r = ot_ref[pl.ds(hd, n_s, stride=H_A), :]
            o_a.append(r[:, :HEAD_DIM] if hd < G_A else r[:, HEAD_DIM:])
        o_c = oc_ref[...].T
        ms = _head_sum(o_c * o_c, _head_block_ones()) * (1.0 / DV_C)
        o_c = o_c * lax.rsqrt(ms + RMS_EPS) * og_ref[...]
        mix = _dot(jnp.concatenate(o_a + [ob_ref[...], o_c], axis=-1).astype(BF16), wout_ref[...])
        z = alpha * x_ref[...] + (1.0 + gt_ref[...]) * mix
        y_ref[...] = _layer_norm(z, lng_ref[...], lnb_ref[...])


def _mixer_sample(x, mods, layer, w_in, w_out, tables, sinks, gln_g, gln_b, w0_exp, bs_exp, lb, ng_exp,
                  ln_g, ln_b, cache_k, cache_v, state, *, alpha):
    n_s, d = x.shape
    assert n_s % (H_C * SUBLANES) == 0 and n_s % LANES == 0
    nb = n_s // H_C
    depth = w_in.shape[0]
    cos, s_lo, s_hi = tables
    mod_spec = lambda j: pl.BlockSpec((None, n_s, d), lambda i: (layer, 0, j))
    tab_spec = pl.BlockSpec((1, LANES), lambda i: (0, 0))
    lay3 = lambda shape: pl.BlockSpec((None,) + shape, lambda i: (layer,) + (0,) * len(shape))
    rows = pl.BlockSpec((n_s, d), lambda i: (0, 0))
    per_head = pltpu.VMEM((D_C, n_s), F32)
    per_row = pltpu.VMEM((n_s * H_A, LANES), F32)
    return pl.pallas_call(
        functools.partial(_mixer_sample_kernel, layer=layer, n_s=n_s, nb=nb, alpha=alpha),
        grid=(H_C,),
        in_specs=[
            rows,
            mod_spec(3), mod_spec(4), mod_spec(5),
            lay3((d, D_IN)), lay3((d, d)),
            tab_spec, tab_spec, tab_spec,
            pl.BlockSpec(memory_space=pltpu.SMEM),
            lay3((1, D_B)), lay3((1, D_B)), lay3((1, D_B)), lay3((CHUNK_B, D_B)),
            pl.BlockSpec((depth, D_C), lambda i: (0, 0)),
            lay3((1, D_C)),
            pl.BlockSpec((None, 1, d), lambda i: (layer * 3 + 1, 0, 0)),
            pl.BlockSpec((None, 1, d), lambda i: (layer * 3 + 1, 0, 0)),
            pl.BlockSpec((None, nb, D_KV, WINDOW), lambda i: (layer, i, 0, 0)),
            pl.BlockSpec((None, nb, D_KV, WINDOW), lambda i: (layer, i, 0, 0)),
            pl.BlockSpec((None, None, DK_C, DV_C, n_s), lambda i: (layer, i, 0, 0, 0)),
        ],
        out_specs=[
            rows,
            pl.BlockSpec((n_s, D_KV), lambda i: (0, 0)),
            pl.BlockSpec((n_s, D_KV), lambda i: (0, 0)),
            pl.BlockSpec((n_s, D_B), lambda i: (0, 0)),
            pl.BlockSpec((None, DK_C, DV_C, n_s), lambda i: (i, 0, 0, 0)),
        ],
        out_shape=[
            jax.ShapeDtypeStruct((n_s, d), F32),
            jax.ShapeDtypeStruct((n_s, D_KV), F32),
            jax.ShapeDtypeStruct((n_s, D_KV), F32),
            jax.ShapeDtypeStruct((n_s, D_B), F32),
            jax.ShapeDtypeStruct((H_C, DK_C, DV_C, n_s), F32),
        ],
        scratch_shapes=[
            per_row,
            per_row,
            per_row,
            per_row,
            pltpu.VMEM((nb * H_A, WINDOW), F32),
            per_row,
            per_head, per_head, per_head, per_head,
            pltpu.VMEM((n_s, D_B), F32),
            pltpu.VMEM((n_s, D_C), F32),
            per_head,
        ],
        compiler_params=pltpu.CompilerParams(
            dimension_semantics=("arbitrary",), vmem_limit_bytes=VMEM_LIMIT_BYTES),
        name="mixer_sample",
    )(x, mods, mods, mods, w_in, w_out, cos, s_lo, s_hi, sinks, gln_g, gln_b, w0_exp, bs_exp, lb, ng_exp,
      ln_g, ln_b, cache_k, cache_v, state)


def _rope_tables(positions):
    half = ROT_DIM // 2
    f32 = np.float32
    inv = (ROPE_THETA ** (-np.arange(half, dtype=f32) * 2.0 / ROT_DIM)).astype(f32)
    ang = (positions.astype(f32)[:, None] * inv[None, :]).astype(f32)
    cos, sin = np.cos(ang).astype(f32), np.sin(ang).astype(f32)
    n = positions.shape[0]
    rest = HEAD_DIM - ROT_DIM
    cos_h = np.concatenate([cos, cos, np.ones((n, rest), f32)], axis=-1)
    lo_h = np.concatenate([-sin, np.zeros((n, half + rest), f32)], axis=-1)
    hi_h = np.concatenate([np.zeros((n, half), f32), sin, np.zeros((n, rest), f32)], axis=-1)
    two = lambda a: jnp.asarray(np.concatenate([a, a], axis=-1))
    return two(cos_h), two(lo_h), two(hi_h)


def kernel(x_prompt, x_sample, cache_k, cache_v, state_hgrn, c_prompt, c_sample, w_in, w_out, attn_sinks,
           gmlp_ln_g, gmlp_ln_b, gmlp_ws, gmlp_bs, hgrn_lb, hgrn_norm_g, ffn1_in, ffn1_out, ffn2_in,
           ffn2_out, ada_w, ada_b, ln_g, ln_b):
    batch, seq, d = x_prompt.shape
    n_s, dec_seq, _ = x_sample.shape
    depth = w_in.shape[0]
    assert d == D_MODEL and dec_seq == 1 and batch <= MOD_ROWS
    assert cache_k.shape[2] == WINDOW
    alpha = (2 * depth) ** 0.25

    w_in_b, w_out_b = w_in.astype(BF16), w_out.astype(BF16)
    ln_g3 = ln_g.reshape(depth * 3, 1, d)
    ln_b3 = ln_b.reshape(depth * 3, 1, d)
    gln_g = gmlp_ln_g.reshape(depth, 1, D_B)
    gln_b = gmlp_ln_b.reshape(depth, 1, D_B)
    bs_exp = jnp.repeat(jnp.swapaxes(gmlp_bs, 1, 2), HEAD_DIM, axis=2)
    w0_exp = jnp.repeat(gmlp_ws[:, :, 0, 0], HEAD_DIM, axis=1).reshape(depth, 1, D_B)
    ng_exp = jnp.tile(hgrn_norm_g, (1, H_C)).reshape(depth, 1, D_C)
    tab_p = _rope_tables(np.arange(seq))
    tab_s = _rope_tables(PAST_LEN + np.arange(dec_seq))

    c_all = jnp.concatenate([c_prompt, jnp.zeros((MOD_ROWS - batch, d), F32), c_sample], axis=0)
    mod_p, mod_s = _adaln(c_all, ada_w, ada_b)

    ck = jnp.transpose(cache_k, (0, 1, 3, 4, 2)).reshape(depth, n_s, D_KV, WINDOW)
    cv = jnp.transpose(cache_v, (0, 1, 3, 4, 2)).reshape(depth, n_s, D_KV, WINDOW)
    s0 = jnp.transpose(state_hgrn, (0, 2, 3, 4, 1))

    xp = x_prompt.reshape(batch * seq, d)
    xs = x_sample.reshape(n_s, d)
    ffn = functools.partial(_ffn, rows_per_batch=seq, alpha=alpha)
    kw, vw, st_p, kn, vnw, gvs, st_s = [], [], [], [], [], [], []
    for l in range(depth):
        xp, xs = ffn(xp, xs, mod_p, mod_s, l, 0, ffn1_in, ffn1_out, ln_g3, ln_b3)
        xp, k_l, v_l, s_l = _mixer_prompt(
            xp, mod_p, l, w_in_b, w_out_b, tab_p, attn_sinks, gln_g, gln_b, gmlp_ws, bs_exp, hgrn_lb,
            ng_exp, ln_g3, ln_b3, batch=batch, seq=seq, alpha=alpha)
        kw.append(k_l), vw.append(v_l), st_p.append(s_l)
        xs, k_l, v_l, g_l, s_l = _mixer_sample(
            xs, mod_s, l, w_in_b, w_out_b, tab_s, attn_sinks, gln_g, gln_b, w0_exp, bs_exp, hgrn_lb,
            ng_exp, ln_g3, ln_b3, ck, cv, s0, alpha=alpha)
        kn.append(k_l), vnw.append(v_l), gvs.append(g_l), st_s.append(s_l)
        xp, xs = ffn(xp, xs, mod_p, mod_s, l, 2, ffn2_in, ffn2_out, ln_g3, ln_b3)

    stack = lambda parts, shape: jnp.stack(parts, axis=0).reshape((depth,) + shape)
    return (
        xp.reshape(batch, seq, d),
        xs.reshape(n_s, dec_seq, d),
        stack(kw, (batch, WINDOW, KV_A, HEAD_DIM)),
        stack(vw, (batch, WINDOW, KV_A, HEAD_DIM)),
        stack(st_p, (batch, H_C, DK_C, DV_C)),
        stack(kn, (n_s, dec_seq, KV_A, HEAD_DIM)),
        stack(vnw, (n_s, dec_seq, KV_A, HEAD_DIM)),
        stack(gvs, (n_s, dec_seq, D_B)),
        jnp.transpose(jnp.stack(st_s, axis=0), (0, 4, 1, 2, 3)),
    )
```

```python
import functools

import numpy as np
import jax
import jax.numpy as jnp
from jax import lax
from jax.experimental import pallas as pl
from jax.experimental.pallas import tpu as pltpu

F32 = jnp.float32
BF16 = jnp.bfloat16

D_MODEL = 1024
HEAD_DIM = 64
H_A = 8
KV_A = 2
G_A = H_A // KV_A
D_A = H_A * HEAD_DIM
D_KV = KV_A * HEAD_DIM
WINDOW = 128
ROT_DIM = HEAD_DIM // 4
ROPE_THETA = 500000.0
G_B = 4
D_B = G_B * HEAD_DIM
CHUNK_B = 128
H_C = 4
DK_C = 64
DV_C = 64
D_C = H_C * DV_C
HEAD_SHIFT = DV_C.bit_length() - 1
assert 1 << HEAD_SHIFT == DV_C == DK_C
D_IN = 2304
D_FF = 2816
N_MOD = 9
LN_EPS = 1e-5
RMS_EPS = 1e-6
NEG_BIG = -1e30
LOG2_E = 1.4426950408889634
PAST_LEN = 16384

OFF_Q = 0
OFF_K = OFF_Q + D_A
OFF_V = OFF_K + D_KV
OFF_U = OFF_V + D_KV
OFF_GV = OFF_U + D_B
OFF_CQ = OFF_GV + D_B
OFF_CF = OFF_CQ + D_C
OFF_CI = OFF_CF + D_C
OFF_CG = OFF_CI + D_C

SUBLANES = 8
LANES = 128
MXU_DIM = 256
VMEM_LIMIT_BYTES = 56 * 1024 * 1024

FFN_ROWS = 1024
FFN_OUT_BLOCKS = 4
MIX_OUT_BLOCKS = 2
FF_CHUNK = MXU_DIM
FF_CONV_STEPS = 11
FF_CONV_COLS = 2 * D_FF // FF_CONV_STEPS
FF_CONV_ROWS = D_FF // FF_CONV_STEPS
ADA_COLS = 2304
MIX_ROWS = 1024
HG_CHUNK = 16
MOD_ROWS = SUBLANES


def _dot(a, b):
    return jnp.dot(a, b, preferred_element_type=F32)


def _dot_t(a, b):
    return lax.dot_general(a, b, (((1,), (1,)), ((), ())), preferred_element_type=F32)


def _tdot(a, b):
    return lax.dot_general(a, b, (((0,), (0,)), ((), ())), preferred_element_type=F32)


def _layer_norm(z, g, b):
    mu = jnp.mean(z, axis=-1, keepdims=True)
    zc = z - mu
    var = jnp.mean(zc * zc, axis=-1, keepdims=True)
    return zc * lax.rsqrt(var + LN_EPS) * g + b


def _silu(x):
    return x * jax.nn.sigmoid(x)


def _head_block_ones():
    r = lax.broadcasted_iota(jnp.int32, (D_C, D_C), 0) >> HEAD_SHIFT
    c = lax.broadcasted_iota(jnp.int32, (D_C, D_C), 1) >> HEAD_SHIFT
    return jnp.where(r == c, 1.0, 0.0).astype(BF16)


def _head_sum(x, ones_bd):
    hi = x.astype(BF16)
    lo = (x - hi.astype(F32)).astype(BF16)
    return _dot(hi, ones_bd) + _dot(lo, ones_bd)


def _rope(x, cos, sin_lo, sin_hi):
    half = ROT_DIM // 2
    out = []
    for c in range(x.shape[-1] // LANES):
        xc = x[:, c * LANES:(c + 1) * LANES]
        out.append(xc * cos + pltpu.roll(xc, LANES - half, 1) * sin_lo + pltpu.roll(xc, half, 1) * sin_hi)
    return out[0] if len(out) == 1 else jnp.concatenate(out, axis=-1)


def _hgrn_lower_bound(lb_ref, layer):
    w = lb_ref[...]
    e = jnp.exp(w - jnp.max(w, axis=0, keepdims=True))
    p = e / jnp.sum(e, axis=0, keepdims=True)
    lb = jnp.zeros((1, D_C), F32)
    for j in range(1, layer + 1):
        lb = lb + p[j:j + 1, :]
    return lb


def _adaln_kernel(c_ref, w_ref, b_ref, op_ref, os_ref):
    s = _silu(c_ref[...]).astype(BF16)
    r = _dot(s, w_ref[...].astype(BF16)) + b_ref[...]
    op_ref[...] = r[:MOD_ROWS]
    os_ref[...] = r[MOD_ROWS:]


def _adaln(c_all, ada_w, ada_b):
    depth, d, n = ada_w.shape
    rows = c_all.shape[0]
    n_s = rows - MOD_ROWS
    return pl.pallas_call(
        _adaln_kernel,
        grid=(depth, n // ADA_COLS),
        in_specs=[
            pl.BlockSpec((rows, d), lambda l, j: (0, 0)),
            pl.BlockSpec((None, d, ADA_COLS), lambda l, j: (l, 0, j)),
            pl.BlockSpec((None, 1, ADA_COLS), lambda l, j: (l, 0, j)),
        ],
        out_specs=[
            pl.BlockSpec((None, MOD_ROWS, ADA_COLS), lambda l, j: (l, 0, j)),
            pl.BlockSpec((None, n_s, ADA_COLS), lambda l, j: (l, 0, j)),
        ],
        out_shape=[
            jax.ShapeDtypeStruct((depth, MOD_ROWS, n), F32),
            jax.ShapeDtypeStruct((depth, n_s, n), F32),
        ],
        compiler_params=pltpu.CompilerParams(
            dimension_semantics=("arbitrary", "arbitrary"), vmem_limit_bytes=VMEM_LIMIT_BYTES),
        name="adaln",
    )(c_all, ada_w, ada_b.reshape(depth, 1, n))


def _ffn_rows(x, shift, scale, gate, wi_ref, wo_ref, act_ref, lng, lnb, alpha):
    rows = x.shape[0]
    h = (x * (1.0 + scale) + shift).astype(BF16)
    for c in range(D_FF // FF_CHUNK):
        lo = c * FF_CHUNK
        ga = D_FF + lo
        a = _dot(h, wi_ref[lo // FF_CONV_COLS, :, lo % FF_CONV_COLS:lo % FF_CONV_COLS + FF_CHUNK])
        g = _dot(h, wi_ref[ga // FF_CONV_COLS, :, ga % FF_CONV_COLS:ga % FF_CONV_COLS + FF_CHUNK])
        act_ref[0:rows, lo:lo + FF_CHUNK] = (_silu(g) * a).astype(BF16)
    br = rows // FFN_OUT_BLOCKS if rows % (FFN_OUT_BLOCKS * MXU_DIM) == 0 else rows
    half_gate = 0.5 * (1.0 + gate)
    out = []
    for r in range(0, rows, br):
        y = _dot(act_ref[r:r + br, :], wo_ref[...])
        hg = half_gate if half_gate.shape[0] == 1 else half_gate[r:r + br]
        out.append(_layer_norm(alpha * x[r:r + br] + hg * y, lng, lnb))
    return out[0] if len(out) == 1 else jnp.concatenate(out, axis=0)


def _ffn_kernel(xp_ref, xs_ref, shp_ref, scp_ref, gtp_ref, shs_ref, scs_ref, gts_ref, win_ref, wout_ref,
                lng_ref, lnb_ref, op_ref, os_ref, wi_ref, wo_ref, act_ref, *, n_tiles, tiles_per_batch, alpha):
    step = pl.program_id(0)

    @pl.when(step < FF_CONV_STEPS)
    def _():
        wi_ref[step] = win_ref[...].astype(BF16)
        r0 = pl.multiple_of(step * FF_CONV_ROWS, FF_CONV_ROWS)
        wo_ref[pl.ds(r0, FF_CONV_ROWS), :] = wout_ref[...].astype(BF16)

    @pl.when(jnp.logical_and(step >= FF_CONV_STEPS, step < FF_CONV_STEPS + n_tiles))
    def _():
        batch = (step - FF_CONV_STEPS) // tiles_per_batch
        op_ref[...] = _ffn_rows(
            xp_ref[...], shp_ref[pl.ds(batch, 1), :], scp_ref[pl.ds(batch, 1), :], gtp_ref[pl.ds(batch, 1), :],
            wi_ref, wo_ref, act_ref, lng_ref[...], lnb_ref[...], alpha)

    @pl.when(step == FF_CONV_STEPS + n_tiles)
    def _():
        os_ref[...] = _ffn_rows(
            xs_ref[...], shs_ref[...], scs_ref[...], gts_ref[...],
            wi_ref, wo_ref, act_ref, lng_ref[...], lnb_ref[...], alpha)


def _ffn(xp, xs, mod_p, mod_s, layer, sub, w_in, w_out, ln_g, ln_b, *, rows_per_batch, alpha):
    m, d = xp.shape
    n_s = xs.shape[0]
    tm = min(FFN_ROWS, rows_per_batch)
    assert m % tm == 0 and rows_per_batch % tm == 0 and n_s <= tm
    n_tiles = m // tm
    last_conv = FF_CONV_STEPS - 1
    tile = lambda i: jnp.clip(i - FF_CONV_STEPS, 0, n_tiles - 1)
    mod_p_spec = lambda j: pl.BlockSpec((None, MOD_ROWS, d), lambda i: (layer, 0, j))
    mod_s_spec = lambda j: pl.BlockSpec((None, n_s, d), lambda i: (layer, 0, j))
    return pl.pallas_call(
        functools.partial(_ffn_kernel, n_tiles=n_tiles, tiles_per_batch=rows_per_batch // tm, alpha=alpha),
        grid=(FF_CONV_STEPS + n_tiles + 1,),
        in_specs=[
            pl.BlockSpec((tm, d), lambda i: (tile(i), 0)),
            pl.BlockSpec((n_s, d), lambda i: (0, 0)),
            mod_p_spec(3 * sub), mod_p_spec(3 * sub + 1), mod_p_spec(3 * sub + 2),
            mod_s_spec(3 * sub), mod_s_spec(3 * sub + 1), mod_s_spec(3 * sub + 2),
            pl.BlockSpec((None, d, FF_CONV_COLS), lambda i: (layer, 0, jnp.minimum(i, last_conv))),
            pl.BlockSpec((None, FF_CONV_ROWS, d), lambda i: (layer, jnp.minimum(i, last_conv), 0)),
            pl.BlockSpec((None, 1, d), lambda i: (layer * 3 + sub, 0, 0)),
            pl.BlockSpec((None, 1, d), lambda i: (layer * 3 + sub, 0, 0)),
        ],
        out_specs=[
            pl.BlockSpec((tm, d), lambda i: (tile(i), 0)),
            pl.BlockSpec((n_s, d), lambda i: (0, 0)),
        ],
        out_shape=[jax.ShapeDtypeStruct((m, d), F32), jax.ShapeDtypeStruct((n_s, d), F32)],
        scratch_shapes=[
            pltpu.VMEM((FF_CONV_STEPS, d, FF_CONV_COLS), BF16),
            pltpu.VMEM((D_FF, d), BF16),
            pltpu.VMEM((tm, D_FF), BF16),
        ],
        compiler_params=pltpu.CompilerParams(
            dimension_semantics=("arbitrary",), vmem_limit_bytes=VMEM_LIMIT_BYTES),
        name="ffn",
    )(xp, xs, mod_p, mod_p, mod_p, mod_s, mod_s, mod_s, w_in, w_out, ln_g, ln_b)


def _attention_block(q_blk, keys, vals, mask, fill):
    outs = []
    for j in range(KV_A):
        kj = keys[:, j * HEAD_DIM:(j + 1) * HEAD_DIM].astype(BF16)
        vj = vals[:, j * HEAD_DIM:(j + 1) * HEAD_DIM].astype(BF16)
        qj = jnp.concatenate(
            [q_blk[:, (j * G_A + g) * HEAD_DIM:(j * G_A + g + 1) * HEAD_DIM] for g in range(G_A)],
            axis=0).astype(BF16)
        s = _dot_t(qj, kj)
        s = jnp.concatenate(
            [jnp.where(mask, s[g * WINDOW:(g + 1) * WINDOW], fill[j][g]) for g in range(G_A)], axis=0)
        p = jnp.exp2(s - jnp.max(s, axis=-1, keepdims=True))
        den = jnp.sum(p, axis=-1, keepdims=True)
        o = _dot(p.astype(BF16), vj) * (1.0 / den)
        outs.extend(o[g * WINDOW:(g + 1) * WINDOW] for g in range(G_A))
    return jnp.concatenate(outs, axis=-1)


def _mixer_prompt_kernel(
        x_ref, sh_ref, sc_ref, gt_ref, win_ref, wout_ref, cos_ref, sl_ref, sh2_ref, sink_ref,
        gg_ref, gb_ref, ws_ref, bs_ref, lb_ref, ng_ref, lng_ref, lnb_ref,
        y_ref, kwin_ref, vwin_ref, st_out_ref,
        kprev_ref, vprev_ref, st_ref,
        *, layer, tq, alpha):
    b_idx = pl.program_id(0)
    t_idx = pl.program_id(1)
    n_t = pl.num_programs(1)

    @pl.when(t_idx == 0)
    def _():
        kprev_ref[...] = jnp.zeros_like(kprev_ref)
        vprev_ref[...] = jnp.zeros_like(vprev_ref)
        st_ref[...] = jnp.zeros_like(st_ref)

    x = x_ref[...]
    shift = sh_ref[pl.ds(b_idx, 1), :]
    scale = sc_ref[pl.ds(b_idx, 1), :]
    gate = gt_ref[pl.ds(b_idx, 1), :]
    h = (x * (1.0 + scale) + shift).astype(BF16)
    proj = _dot(h, win_ref[...])

    cos, s_lo, s_hi = cos_ref[...], sl_ref[...], sh2_ref[...]
    q = _rope(proj[:, OFF_Q:OFF_K], cos, s_lo, s_hi) * (HEAD_DIM ** -0.5 * LOG2_E)
    k = _rope(proj[:, OFF_K:OFF_V], cos, s_lo, s_hi)
    v = proj[:, OFF_V:OFF_U]

    qi = lax.broadcasted_iota(jnp.int32, (WINDOW, 2 * WINDOW), 0)
    kj = lax.broadcasted_iota(jnp.int32, (WINDOW, 2 * WINDOW), 1)
    dist = kj - qi
    band = (dist - 1).astype(jnp.uint32) < jnp.uint32(WINDOW)
    col0 = lax.broadcasted_iota(jnp.int32, (1, 2 * WINDOW), 1) == 0
    fill = [[jnp.where(col0, sink_ref[layer, j * G_A + g] * LOG2_E, NEG_BIG) for g in range(G_A)]
            for j in range(KV_A)]
    first_row = lax.broadcasted_iota(jnp.int32, (WINDOW, D_KV), 0) == 0
    o_a = []
    for blk in range(tq // WINDOW):
        r0 = blk * WINDOW
        if blk == 0:
            k_prev, v_prev = kprev_ref[...], vprev_ref[...]
            lower = jnp.where(t_idx == 0, WINDOW, 0)
            mask = jnp.logical_and(band, kj >= lower)
        else:
            k_prev, v_prev = k[r0 - WINDOW:r0], v[r0 - WINDOW:r0]
            mask = band
        keys = jnp.concatenate([k_prev, k[r0:r0 + WINDOW]], axis=0)
        vals = jnp.concatenate([jnp.where(first_row, 0.0, v_prev), v[r0:r0 + WINDOW]], axis=0)
        o_a.append(_attention_block(q[r0:r0 + WINDOW], keys, vals, mask, fill))
    o_a = jnp.concatenate(o_a, axis=0)
    kprev_ref[...] = k[tq - WINDOW:]
    vprev_ref[...] = v[tq - WINDOW:]
    kwin_ref[...] = k[tq - WINDOW:]
    vwin_ref[...] = v[tq - WINDOW:]

    vn = _layer_norm(proj[:, OFF_GV:OFF_CQ], gg_ref[...], gb_ref[...])
    u = proj[:, OFF_U:OFF_GV]
    tr = lax.broadcasted_iota(jnp.int32, (CHUNK_B, CHUNK_B), 0)
    tc = lax.broadcasted_iota(jnp.int32, (CHUNK_B, CHUNK_B), 1)
    wm = [jnp.where(tr >= tc, ws_ref[g], 0.0).astype(BF16) for g in range(G_B)]
    o_b = []
    for c in range(tq // CHUNK_B):
        vc = vn[c * CHUNK_B:(c + 1) * CHUNK_B].astype(BF16)
        mixed = jnp.concatenate(
            [_dot(wm[g], vc[:, g * HEAD_DIM:(g + 1) * HEAD_DIM]) for g in range(G_B)], axis=-1)
        o_b.append(u[c * CHUNK_B:(c + 1) * CHUNK_B] * (mixed + bs_ref[...]))
    o_b = jnp.concatenate(o_b, axis=0)

    n_ch = tq // HG_CHUNK
    half = HG_CHUNK // 2
    cq = proj[:, OFF_CQ:OFF_CF]
    ci = proj[:, OFF_CI:OFF_CG]
    lb = _hgrn_lower_bound(lb_ref, layer)
    f = lb + (1.0 - lb) * jax.nn.sigmoid(proj[:, OFF_CF:OFF_CI])
    lf = jnp.log(f)
    kc = 1.0 - f

    sr = min(tq, MXU_DIM)
    rt = lax.broadcasted_iota(jnp.int32, (sr, sr), 0)
    ct = lax.broadcasted_iota(jnp.int32, (sr, sr), 1)
    ltri = jnp.where(ct <= rt, jnp.where(ct >= (rt & -HG_CHUNK), 1.0, 0.0), 0.0).astype(BF16)
    lf_hi = lf.astype(BF16)
    lf_r = lf - lf_hi.astype(F32)
    lf_mid = lf_r.astype(BF16)
    lf_lo = (lf_r - lf_mid.astype(F32)).astype(BF16)
    bsum = jnp.concatenate(
        [_dot(ltri, lf_hi[r:r + sr]) + _dot(ltri, lf_mid[r:r + sr]) + _dot(ltri, lf_lo[r:r + sr])
         for r in range(0, tq, sr)], axis=0) * LOG2_E

    def chunks(a):
        return a.reshape(n_ch, HG_CHUNK, D_C)

    b3, cq3, ci3 = chunks(bsum), chunks(cq), chunks(ci)
    b_end = b3[:, HG_CHUNK - 1:HG_CHUNK, :]
    eb = jnp.exp2(bsum)
    qb = cq * eb
    kb = kc * jnp.exp2(b_end - b3).reshape(tq, D_C)

    ones_bd = _head_block_ones()
    c3 = chunks(bsum - jnp.log2(jnp.maximum(kc, 0.0)))

    def pair_products(s, lo):
        e = cq3[:, lo:] * jnp.exp2(b3[:, lo:] - c3[:, s:s + 1, :])
        pos = lax.broadcasted_iota(jnp.int32, (n_ch, half, D_C), 1) + (s // half) * half
        grp = s // half - lo // half
        parts = [e[:, g * half:(g + 1) * half] for g in range((HG_CHUNK - lo) // half)]
        parts[grp] = jnp.where(pos >= s, parts[grp], 0.0)
        return parts[0] if len(parts) == 1 else jnp.concatenate(parts, axis=1)

    def head_sums(slab):
        return chunks(_dot(slab.reshape(tq, D_C).astype(BF16), ones_bd))

    o_all = jnp.zeros((n_ch, HG_CHUNK, D_C), F32)
    for s in range(half):
        o_all = o_all + head_sums(pair_products(s, 0)) * ci3[:, s:s + 1, :]
    o_up = jnp.zeros((n_ch, half, D_C), F32)
    for s in range(half, HG_CHUNK, 2):
        a3 = head_sums(jnp.concatenate([pair_products(s, half), pair_products(s + 1, half)], axis=1))
        o_up = o_up + a3[:, :half] * ci3[:, s:s + 1, :] + a3[:, half:] * ci3[:, s + 1:s + 2, :]
    o_c = (o_all + jnp.concatenate([jnp.zeros((n_ch, half, D_C), F32), o_up], axis=1)).reshape(tq, D_C)

    lane_head = lax.broadcasted_iota(jnp.int32, (tq, D_C), 1) >> HEAD_SHIFT
    im_h = [jnp.where(lane_head == hh, ci, 0.0).astype(BF16) for hh in range(H_C)]
    qb_h = [qb[:, hh * DK_C:(hh + 1) * DK_C].astype(BF16) for hh in range(H_C)]
    kb_h = [kb[:, hh * DK_C:(hh + 1) * DK_C].astype(BF16) for hh in range(H_C)]
    eb_t = eb.T
    lane_head_c = lax.broadcasted_iota(jnp.int32, (HG_CHUNK, D_C), 1) >> HEAD_SHIFT
    st = st_ref[...]
    o_inter = []
    for n in range(n_ch):
        r0, r1 = n * HG_CHUNK, (n + 1) * HG_CHUNK
        read = _dot(jnp.concatenate([a[r0:r1] for a in qb_h], axis=0), st.astype(BF16))
        o_n = jnp.zeros((HG_CHUNK, D_C), F32)
        for hh in range(H_C):
            o_n = o_n + jnp.where(lane_head_c == hh, read[hh * HG_CHUNK:(hh + 1) * HG_CHUNK], 0.0)
        o_inter.append(o_n)
        upd = _tdot(jnp.concatenate([a[r0:r1] for a in kb_h], axis=0),
                    jnp.concatenate([a[r0:r1] for a in im_h], axis=0))
        col = eb_t[:, r1 - 1:r1]
        decay = jnp.concatenate(
            [jnp.broadcast_to(col[hh * DK_C:(hh + 1) * DK_C], (DK_C, DV_C)) for hh in range(H_C)], axis=1)
        st = decay * st + upd
    st_ref[...] = st
    o_c = o_c + jnp.concatenate(o_inter, axis=0)

    @pl.when(t_idx == n_t - 1)
    def _():
        for hh in range(H_C):
            st_out_ref[hh * DK_C:(hh + 1) * DK_C, :] = st[:, hh * DV_C:(hh + 1) * DV_C]

    ms = _head_sum(o_c * o_c, ones_bd) * (1.0 / DV_C)
    o_c = o_c * lax.rsqrt(ms + RMS_EPS) * ng_ref[...] * _silu(proj[:, OFF_CG:])

    o_cat = jnp.concatenate([o_a, o_b, o_c], axis=-1).astype(BF16)
    br = tq // MIX_OUT_BLOCKS if tq % (MIX_OUT_BLOCKS * MXU_DIM) == 0 else tq
    for r in range(0, tq, br):
        z = alpha * x[r:r + br] + (1.0 + gate) * _dot(o_cat[r:r + br], wout_ref[...])
        y_ref[r:r + br, :] = _layer_norm(z, lng_ref[...], lnb_ref[...])


def _mixer_prompt(x, mods, layer, w_in, w_out, tables, sinks, gln_g, gln_b, ws, bs_exp, lb, ng_exp,
                  ln_g, ln_b, *, batch, seq, alpha):
    m, d = x.shape
    tq = min(MIX_ROWS, seq)
    assert seq % tq == 0 and tq % WINDOW == 0
    n_t = seq // tq
    depth = w_in.shape[0]
    cos, s_lo, s_hi = tables
    mod_spec = lambda j: pl.BlockSpec((None, MOD_ROWS, d), lambda b, t: (layer, 0, j))
    tab_spec = pl.BlockSpec((tq, LANES), lambda b, t: (t, 0))
    lay3 = lambda shape: pl.BlockSpec((None,) + shape, lambda b, t: (layer,) + (0,) * len(shape))
    return pl.pallas_call(
        functools.partial(_mixer_prompt_kernel, layer=layer, tq=tq, alpha=alpha),
        grid=(batch, n_t),
        in_specs=[
            pl.BlockSpec((tq, d), lambda b, t: (b * n_t + t, 0)),
            mod_spec(3), mod_spec(4), mod_spec(5),
            lay3((d, D_IN)), lay3((d, d)),
            tab_spec, tab_spec, tab_spec,
            pl.BlockSpec(memory_space=pltpu.SMEM),
            lay3((1, D_B)), lay3((1, D_B)),
            lay3((G_B, CHUNK_B, CHUNK_B)), lay3((CHUNK_B, D_B)),
            pl.BlockSpec((depth, D_C), lambda b, t: (0, 0)),
            lay3((1, D_C)),
            pl.BlockSpec((None, 1, d), lambda b, t: (layer * 3 + 1, 0, 0)),
            pl.BlockSpec((None, 1, d), lambda b, t: (layer * 3 + 1, 0, 0)),
        ],
        out_specs=[
            pl.BlockSpec((tq, d), lambda b, t: (b * n_t + t, 0)),
            pl.BlockSpec((None, WINDOW, D_KV), lambda b, t: (b, 0, 0)),
            pl.BlockSpec((None, WINDOW, D_KV), lambda b, t: (b, 0, 0)),
            pl.BlockSpec((None, D_C, DV_C), lambda b, t: (b, 0, 0)),
        ],
        out_shape=[
            jax.ShapeDtypeStruct((m, d), F32),
            jax.ShapeDtypeStruct((batch, WINDOW, D_KV), F32),
            jax.ShapeDtypeStruct((batch, WINDOW, D_KV), F32),
            jax.ShapeDtypeStruct((batch, D_C, DV_C), F32),
        ],
        scratch_shapes=[
            pltpu.VMEM((WINDOW, D_KV), F32),
            pltpu.VMEM((WINDOW, D_KV), F32),
            pltpu.VMEM((DK_C, D_C), F32),
        ],
        compiler_params=pltpu.CompilerParams(
            dimension_semantics=("arbitrary", "arbitrary"), vmem_limit_bytes=VMEM_LIMIT_BYTES),
        name="mixer_prompt",
    )(x, mods, mods, mods, w_in, w_out, cos, s_lo, s_hi, sinks, gln_g, gln_b, ws, bs_exp, lb, ng_exp,
      ln_g, ln_b)


def _mixer_sample_kernel(
        x_ref, sh_ref, sc_ref, gt_ref, win_ref, wout_ref, cos_ref, sl_ref, sh2_ref, sink_ref,
        gg_ref, gb_ref, w0_ref, bs_ref, lb_ref, ng_ref, lng_ref, lnb_ref, ck_ref, cv_ref, s0_ref,
        y_ref, knew_ref, vnew_ref, gv_ref, s1_ref,
        qt_ref, kn_ref, vn_ref, sn_ref, sc_scr, ot_ref, ft_ref, kt_ref, qc_ref, it_ref, ob_ref, og_ref, oc_ref,
        *, layer, n_s, nb, alpha):
    step = pl.program_id(0)
    rows_all = n_s * H_A

    @pl.when(step == 0)
    def _():
        x = x_ref[...]
        h = (x * (1.0 + sc_ref[...]) + sh_ref[...]).astype(BF16)
        proj = _dot(h, win_ref[...])
        cos, s_lo, s_hi = cos_ref[...], sl_ref[...], sh2_ref[...]
        q = _rope(proj[:, OFF_Q:OFF_K], cos, s_lo, s_hi) * (HEAD_DIM ** -0.5)
        k = _rope(proj[:, OFF_K:OFF_V], cos, s_lo, s_hi)
        v = proj[:, OFF_V:OFF_U]
        knew_ref[...] = k
        vnew_ref[...] = v
        zeros64 = jnp.zeros((n_s, HEAD_DIM), F32)
        for hd in range(H_A):
            qh = q[:, hd * HEAD_DIM:(hd + 1) * HEAD_DIM]
            row = jnp.concatenate([qh, zeros64] if hd < G_A else [zeros64, qh], axis=-1)
            qt_ref[pl.ds(hd, n_s, stride=H_A), :] = row
            kn_ref[pl.ds(hd, n_s, stride=H_A), :] = k
            vn_ref[pl.ds(hd, n_s, stride=H_A), :] = v
        s_new = jnp.sum(qt_ref[...] * kn_ref[...], axis=-1, keepdims=True)
        sn_ref[...] = jnp.broadcast_to(s_new, (rows_all, LANES))

        vn = _layer_norm(proj[:, OFF_GV:OFF_CQ], gg_ref[...], gb_ref[...])
        gv_ref[...] = vn
        ob_ref[...] = proj[:, OFF_U:OFF_GV] * (vn * w0_ref[...] + bs_ref[0:1, :])

        lb = _hgrn_lower_bound(lb_ref, layer)
        f = lb + (1.0 - lb) * jax.nn.sigmoid(proj[:, OFF_CF:OFF_CI])
        ft_ref[...] = f.T
        kt_ref[...] = (1.0 - f).T
        qc_ref[...] = proj[:, OFF_CQ:OFF_CF].T
        it_ref[...] = proj[:, OFF_CI:OFF_CG].T
        og_ref[...] = ng_ref[...] * _silu(proj[:, OFF_CG:])

    base = pl.multiple_of(step * (nb * H_A), nb * H_A)
    for b in range(nb):
        qrows = qt_ref[pl.ds(base + b * H_A, H_A), :].astype(BF16)
        sc_scr[b * H_A:(b + 1) * H_A, :] = _dot(qrows, ck_ref[b].astype(BF16))
    lane = lax.broadcasted_iota(jnp.int32, (nb * H_A, WINDOW), 1)
    s = jnp.where(lane == 0, NEG_BIG, sc_scr[...])
    s_new = sn_ref[pl.ds(base, nb * H_A), 0:1]
    hrow = lax.broadcasted_iota(jnp.int32, (nb * H_A, 1), 0) & (H_A - 1)
    sink = jnp.zeros((nb * H_A, 1), F32)
    for hd in range(H_A):
        sink = jnp.where(hrow == hd, sink_ref[layer, hd], sink)
    mx = jnp.maximum(jnp.maximum(jnp.max(s, axis=-1, keepdims=True), s_new), sink)
    p = jnp.exp(s - mx)
    p_new = jnp.exp(s_new - mx)
    den = jnp.sum(p, axis=-1, keepdims=True) + p_new + jnp.exp(sink - mx)
    sc_scr[...] = p
    for b in range(nb):
        prow = sc_scr[b * H_A:(b + 1) * H_A, :].astype(BF16)
        ot_ref[pl.ds(base + b * H_A, H_A), :] = _dot_t(prow, cv_ref[b].astype(BF16))
    rows = pl.ds(base, nb * H_A)
    ot_ref[rows, :] = (ot_ref[rows, :] + p_new * vn_ref[rows, :]) * (1.0 / den)

    hrow0 = pl.multiple_of(step * DK_C, DK_C)
    i_h = it_ref[pl.ds(hrow0, DV_C), :]
    acc = jnp.zeros((DV_C, n_s), F32)
    for kk in range(DK_C):
        f_k = ft_ref[pl.ds(hrow0 + kk, 1), :]
        k_k = kt_ref[pl.ds(hrow0 + kk, 1), :]
        q_k = qc_ref[pl.ds(hrow0 + kk, 1), :]
        s1 = f_k * s0_ref[kk] + k_k * i_h
        s1_ref[kk] = s1
        acc = acc + q_k * s1
    oc_ref[pl.ds(hrow0, DV_C), :] = acc

    @pl.when(step == pl.num_programs(0) - 1)
    def _():
        o_a = []
        for hd in range(H_A):
            r = ot_ref[pl.ds(hd, n_s, stride=H_A), :]
            o_a.append(r[:, :HEAD_DIM] if hd < G_A else r[:, HEAD_DIM:])
        o_c = oc_ref[...].T
        ms = _head_sum(o_c * o_c, _head_block_ones()) * (1.0 / DV_C)
        o_c = o_c * lax.rsqrt(ms + RMS_EPS) * og_ref[...]
        mix = _dot(jnp.concatenate(o_a + [ob_ref[...], o_c], axis=-1).astype(BF16), wout_ref[...])
        z = alpha * x_ref[...] + (1.0 + gt_ref[...]) * mix
        y_ref[...] = _layer_norm(z, lng_ref[...], lnb_ref[...])


def _mixer_sample(x, mods, layer, w_in, w_out, tables, sinks, gln_g, gln_b, w0_exp, bs_exp, lb, ng_exp,
                  ln_g, ln_b, cache_k, cache_v, state, *, alpha):
    n_s, d = x.shape
    assert n_s % (H_C * SUBLANES) == 0 and n_s % LANES == 0
    nb = n_s // H_C
    depth = w_in.shape[0]
    cos, s_lo, s_hi = tables
    mod_spec = lambda j: pl.BlockSpec((None, n_s, d), lambda i: (layer, 0, j))
    tab_spec = pl.BlockSpec((1, LANES), lambda i: (0, 0))
    lay3 = lambda shape: pl.BlockSpec((None,) + shape, lambda i: (layer,) + (0,) * len(shape))
    rows = pl.BlockSpec((n_s, d), lambda i: (0, 0))
    per_head = pltpu.VMEM((D_C, n_s), F32)
    per_row = pltpu.VMEM((n_s * H_A, LANES), F32)
    return pl.pallas_call(
        functools.partial(_mixer_sample_kernel, layer=layer, n_s=n_s, nb=nb, alpha=alpha),
        grid=(H_C,),
        in_specs=[
            rows,
            mod_spec(3), mod_spec(4), mod_spec(5),
            lay3((d, D_IN)), lay3((d, d)),
            tab_spec, tab_spec, tab_spec,
            pl.BlockSpec(memory_space=pltpu.SMEM),
            lay3((1, D_B)), lay3((1, D_B)), lay3((1, D_B)), lay3((CHUNK_B, D_B)),
            pl.BlockSpec((depth, D_C), lambda i: (0, 0)),
            lay3((1, D_C)),
            pl.BlockSpec((None, 1, d), lambda i: (layer * 3 + 1, 0, 0)),
            pl.BlockSpec((None, 1, d), lambda i: (layer * 3 + 1, 0, 0)),
            pl.BlockSpec((None, nb, D_KV, WINDOW), lambda i: (layer, i, 0, 0)),
            pl.BlockSpec((None, nb, D_KV, WINDOW), lambda i: (layer, i, 0, 0)),
            pl.BlockSpec((None, None, DK_C, DV_C, n_s), lambda i: (layer, i, 0, 0, 0)),
        ],
        out_specs=[
            rows,
            pl.BlockSpec((n_s, D_KV), lambda i: (0, 0)),
            pl.BlockSpec((n_s, D_KV), lambda i: (0, 0)),
            pl.BlockSpec((n_s, D_B), lambda i: (0, 0)),
            pl.BlockSpec((None, DK_C, DV_C, n_s), lambda i: (i, 0, 0, 0)),
        ],
        out_shape=[
            jax.ShapeDtypeStruct((n_s, d), F32),
            jax.ShapeDtypeStruct((n_s, D_KV), F32),
            jax.ShapeDtypeStruct((n_s, D_KV), F32),
            jax.ShapeDtypeStruct((n_s, D_B), F32),
            jax.ShapeDtypeStruct((H_C, DK_C, DV_C, n_s), F32),
        ],
        scratch_shapes=[
            per_row,
            per_row,
            per_row,
            per_row,
            pltpu.VMEM((nb * H_A, WINDOW), F32),
            per_row,
            per_head, per_head, per_head, per_head,
            pltpu.VMEM((n_s, D_B), F32),
            pltpu.VMEM((n_s, D_C), F32),
            per_head,
        ],
        compiler_params=pltpu.CompilerParams(
            dimension_semantics=("arbitrary",), vmem_limit_bytes=VMEM_LIMIT_BYTES),
        name="mixer_sample",
    )(x, mods, mods, mods, w_in, w_out, cos, s_lo, s_hi, sinks, gln_g, gln_b, w0_exp, bs_exp, lb, ng_exp,
      ln_g, ln_b, cache_k, cache_v, state)


def _rope_tables(positions):
    half = ROT_DIM // 2
    f32 = np.float32
    inv = (ROPE_THETA ** (-np.arange(half, dtype=f32) * 2.0 / ROT_DIM)).astype(f32)
    ang = (positions.astype(f32)[:, None] * inv[None, :]).astype(f32)
    cos, sin = np.cos(ang).astype(f32), np.sin(ang).astype(f32)
    n = positions.shape[0]
    rest = HEAD_DIM - ROT_DIM
    cos_h = np.concatenate([cos, cos, np.ones((n, rest), f32)], axis=-1)
    lo_h = np.concatenate([-sin, np.zeros((n, half + rest), f32)], axis=-1)
    hi_h = np.concatenate([np.zeros((n, half), f32), sin, np.zeros((n, rest), f32)], axis=-1)
    two = lambda a: jnp.asarray(np.concatenate([a, a], axis=-1))
    return two(cos_h), two(lo_h), two(hi_h)


def kernel(x_prompt, x_sample, cache_k, cache_v, state_hgrn, c_prompt, c_sample, w_in, w_out, attn_sinks,
           gmlp_ln_g, gmlp_ln_b, gmlp_ws, gmlp_bs, hgrn_lb, hgrn_norm_g, ffn1_in, ffn1_out, ffn2_in,
           ffn2_out, ada_w, ada_b, ln_g, ln_b):
    batch, seq, d = x_prompt.shape
    n_s, dec_seq, _ = x_sample.shape
    depth = w_in.shape[0]
    assert d == D_MODEL and dec_seq == 1 and batch <= MOD_ROWS
    assert cache_k.shape[2] == WINDOW
    alpha = (2 * depth) ** 0.25

    w_in_b, w_out_b = w_in.astype(BF16), w_out.astype(BF16)
    ln_g3 = ln_g.reshape(depth * 3, 1, d)
    ln_b3 = ln_b.reshape(depth * 3, 1, d)
    gln_g = gmlp_ln_g.reshape(depth, 1, D_B)
    gln_b = gmlp_ln_b.reshape(depth, 1, D_B)
    bs_exp = jnp.repeat(jnp.swapaxes(gmlp_bs, 1, 2), HEAD_DIM, axis=2)
    w0_exp = jnp.repeat(gmlp_ws[:, :, 0, 0], HEAD_DIM, axis=1).reshape(depth, 1, D_B)
    ng_exp = jnp.tile(hgrn_norm_g, (1, H_C)).reshape(depth, 1, D_C)
    tab_p = _rope_tables(np.arange(seq))
    tab_s = _rope_tables(PAST_LEN + np.arange(dec_seq))

    c_all = jnp.concatenate([c_prompt, jnp.zeros((MOD_ROWS - batch, d), F32), c_sample], axis=0)
    mod_p, mod_s = _adaln(c_all, ada_w, ada_b)

    ck = jnp.transpose(cache_k, (0, 1, 3, 4, 2)).reshape(depth, n_s, D_KV, WINDOW)
    cv = jnp.transpose(cache_v, (0, 1, 3, 4, 2)).reshape(depth, n_s, D_KV, WINDOW)
    s0 = jnp.transpose(state_hgrn, (0, 2, 3, 4, 1))

    xp = x_prompt.reshape(batch * seq, d)
    xs = x_sample.reshape(n_s, d)
    ffn = functools.partial(_ffn, rows_per_batch=seq, alpha=alpha)
    kw, vw, st_p, kn, vnw, gvs, st_s = [], [], [], [], [], [], []
    for l in range(depth):
        xp, xs = ffn(xp, xs, mod_p, mod_s, l, 0, ffn1_in, ffn1_out, ln_g3, ln_b3)
        xp, k_l, v_l, s_l = _mixer_prompt(
            xp, mod_p, l, w_in_b, w_out_b, tab_p, attn_sinks, gln_g, gln_b, gmlp_ws, bs_exp, hgrn_lb,
            ng_exp, ln_g3, ln_b3, batch=batch, seq=seq, alpha=alpha)
        kw.append(k_l), vw.append(v_l), st_p.append(s_l)
        xs, k_l, v_l, g_l, s_l = _mixer_sample(
            xs, mod_s, l, w_in_b, w_out_b, tab_s, attn_sinks, gln_g, gln_b, w0_exp, bs_exp, hgrn_lb,
            ng_exp, ln_g3, ln_b3, ck, cv, s0, alpha=alpha)
        kn.append(k_l), vnw.append(v_l), gvs.append(g_l), st_s.append(s_l)
        xp, xs = ffn(xp, xs, mod_p, mod_s, l, 2, ffn2_in, ffn2_out, ln_g3, ln_b3)

    stack = lambda parts, shape: jnp.stack(parts, axis=0).reshape((depth,) + shape)
    return (
        xp.reshape(batch, seq, d),
        xs.reshape(n_s, dec_seq, d),
        stack(kw, (batch, WINDOW, KV_A, HEAD_DIM)),
        stack(vw, (batch, WINDOW, KV_A, HEAD_DIM)),
        stack(st_p, (batch, H_C, DK_C, DV_C)),
        stack(kn, (n_s, dec_seq, KV_A, HEAD_DIM)),
        stack(vnw, (n_s, dec_seq, KV_A, HEAD_DIM)),
        stack(gvs, (n_s, dec_seq, D_B)),
        jnp.transpose(jnp.stack(st_s, axis=0), (0, 4, 1, 2, 3)),
    )
```

```python
import functools

import numpy as np
import jax
import jax.numpy as jnp
from jax import lax
from jax.experimental import pallas as pl
from jax.experimental.pallas import tpu as pltpu

F32 = jnp.float32
BF16 = jnp.bfloat16

D_MODEL = 1024
HEAD_DIM = 64
H_A = 8
KV_A = 2
G_A = H_A // KV_A
D_A = H_A * HEAD_DIM
D_KV = KV_A * HEAD_DIM
WINDOW = 128
ROT_DIM = HEAD_DIM // 4
ROPE_THETA = 500000.0
G_B = 4
D_B = G_B * HEAD_DIM
CHUNK_B = 128
H_C = 4
DK_C = 64
DV_C = 64
D_C = H_C * DV_C
HEAD_SHIFT = DV_C.bit_length() - 1
assert 1 << HEAD_SHIFT == DV_C == DK_C
D_IN = 2304
D_FF = 2816
N_MOD = 9
LN_EPS = 1e-5
RMS_EPS = 1e-6
NEG_BIG = -1e30
LOG2_E = 1.4426950408889634
PAST_LEN = 16384

OFF_Q = 0
OFF_K = OFF_Q + D_A
OFF_V = OFF_K + D_KV
OFF_U = OFF_V + D_KV
OFF_GV = OFF_U + D_B
OFF_CQ = OFF_GV + D_B
OFF_CF = OFF_CQ + D_C
OFF_CI = OFF_CF + D_C
OFF_CG = OFF_CI + D_C

SUBLANES = 8
LANES = 128
MXU_DIM = 256
VMEM_LIMIT_BYTES = 56 * 1024 * 1024

FFN_ROWS = 1024
FFN_OUT_BLOCKS = 4
MIX_OUT_BLOCKS = 2
FF_CHUNK = MXU_DIM
FF_CONV_STEPS = D_FF // FF_CHUNK
FF_CONV_COLS = 2 * D_FF // FF_CONV_STEPS
FF_CONV_ROWS = D_FF // FF_CONV_STEPS
ADA_COLS = 2304
MIX_ROWS = 1024
HG_CHUNK = 16
MOD_ROWS = SUBLANES


def _dot(a, b):
    return jnp.dot(a, b, preferred_element_type=F32)


def _dot_t(a, b):
    return lax.dot_general(a, b, (((1,), (1,)), ((), ())), preferred_element_type=F32)


def _tdot(a, b):
    return lax.dot_general(a, b, (((0,), (0,)), ((), ())), preferred_element_type=F32)


def _layer_norm(z, g, b):
    mu = jnp.mean(z, axis=-1, keepdims=True)
    zc = z - mu
    var = jnp.mean(zc * zc, axis=-1, keepdims=True)
    return zc * lax.rsqrt(var + LN_EPS) * g + b


def _silu(x):
    return x * jax.nn.sigmoid(x)


def _head_block_ones():
    r = lax.broadcasted_iota(jnp.int32, (D_C, D_C), 0) >> HEAD_SHIFT
    c = lax.broadcasted_iota(jnp.int32, (D_C, D_C), 1) >> HEAD_SHIFT
    return jnp.where(r == c, 1.0, 0.0).astype(BF16)


def _head_sum(x, ones_bd):
    hi = x.astype(BF16)
    lo = (x - hi.astype(F32)).astype(BF16)
    return _dot(hi, ones_bd) + _dot(lo, ones_bd)


def _rope(x, cos, sin_lo, sin_hi):
    half = ROT_DIM // 2
    out = []
    for c in range(x.shape[-1] // LANES):
        xc = x[:, c * LANES:(c + 1) * LANES]
        out.append(xc * cos + pltpu.roll(xc, LANES - half, 1) * sin_lo + pltpu.roll(xc, half, 1) * sin_hi)
    return out[0] if len(out) == 1 else jnp.concatenate(out, axis=-1)


def _hgrn_lower_bound(lb_ref, layer):
    w = lb_ref[...]
    e = jnp.exp(w - jnp.max(w, axis=0, keepdims=True))
    p = e / jnp.sum(e, axis=0, keepdims=True)
    lb = jnp.zeros((1, D_C), F32)
    for j in range(1, layer + 1):
        lb = lb + p[j:j + 1, :]
    return lb


def _adaln_kernel(c_ref, w_ref, b_ref, op_ref, os_ref):
    s = _silu(c_ref[...]).astype(BF16)
    r = _dot(s, w_ref[...].astype(BF16)) + b_ref[...]
    op_ref[...] = r[:MOD_ROWS]
    os_ref[...] = r[MOD_ROWS:]


def _adaln(c_all, ada_w, ada_b):
    depth, d, n = ada_w.shape
    rows = c_all.shape[0]
    n_s = rows - MOD_ROWS
    return pl.pallas_call(
        _adaln_kernel,
        grid=(depth, n // ADA_COLS),
        in_specs=[
            pl.BlockSpec((rows, d), lambda l, j: (0, 0)),
            pl.BlockSpec((None, d, ADA_COLS), lambda l, j: (l, 0, j)),
            pl.BlockSpec((None, 1, ADA_COLS), lambda l, j: (l, 0, j)),
        ],
        out_specs=[
            pl.BlockSpec((None, MOD_ROWS, ADA_COLS), lambda l, j: (l, 0, j)),
            pl.BlockSpec((None, n_s, ADA_COLS), lambda l, j: (l, 0, j)),
        ],
        out_shape=[
            jax.ShapeDtypeStruct((depth, MOD_ROWS, n), F32),
            jax.ShapeDtypeStruct((depth, n_s, n), F32),
        ],
        compiler_params=pltpu.CompilerParams(
            dimension_semantics=("arbitrary", "arbitrary"), vmem_limit_bytes=VMEM_LIMIT_BYTES),
        name="adaln",
    )(c_all, ada_w, ada_b.reshape(depth, 1, n))


def _ffn_rows(x, shift, scale, gate, wi_ref, wo_ref, act_ref, lng, lnb, alpha):
    rows = x.shape[0]
    h = (x * (1.0 + scale) + shift).astype(BF16)
    for c in range(D_FF // FF_CHUNK):
        lo = c * FF_CHUNK
        ga = D_FF + lo
        a = _dot(h, wi_ref[lo // FF_CONV_COLS, :, lo % FF_CONV_COLS:lo % FF_CONV_COLS + FF_CHUNK])
        g = _dot(h, wi_ref[ga // FF_CONV_COLS, :, ga % FF_CONV_COLS:ga % FF_CONV_COLS + FF_CHUNK])
        act_ref[0:rows, lo:lo + FF_CHUNK] = (_silu(g) * a).astype(BF16)
    br = rows // FFN_OUT_BLOCKS if rows % (FFN_OUT_BLOCKS * MXU_DIM) == 0 else rows
    half_gate = 0.5 * (1.0 + gate)
    out = []
    for r in range(0, rows, br):
        y = _dot(act_ref[r:r + br, :], wo_ref[...])
        hg = half_gate if half_gate.shape[0] == 1 else half_gate[r:r + br]
        out.append(_layer_norm(alpha * x[r:r + br] + hg * y, lng, lnb))
    return out[0] if len(out) == 1 else jnp.concatenate(out, axis=0)


def _ffn_kernel(xp_ref, xs_ref, shp_ref, scp_ref, gtp_ref, shs_ref, scs_ref, gts_ref, win_ref, wout_ref,
                lng_ref, lnb_ref, op_ref, os_ref, wi_ref, wo_ref, act_ref, *, n_tiles, tiles_per_batch, alpha):
    step = pl.program_id(0)

    @pl.when(step < FF_CONV_STEPS)
    def _():
        wi_ref[step] = win_ref[...].astype(BF16)
        r0 = pl.multiple_of(step * FF_CONV_ROWS, FF_CONV_ROWS)
        wo_ref[pl.ds(r0, FF_CONV_ROWS), :] = wout_ref[...].astype(BF16)

    @pl.when(jnp.logical_and(step >= FF_CONV_STEPS, step < FF_CONV_STEPS + n_tiles))
    def _():
        batch = (step - FF_CONV_STEPS) // tiles_per_batch
        op_ref[...] = _ffn_rows(
            xp_ref[...], shp_ref[pl.ds(batch, 1), :], scp_ref[pl.ds(batch, 1), :], gtp_ref[pl.ds(batch, 1), :],
            wi_ref, wo_ref, act_ref, lng_ref[...], lnb_ref[...], alpha)

    @pl.when(step == FF_CONV_STEPS + n_tiles)
    def _():
        os_ref[...] = _ffn_rows(
            xs_ref[...], shs_ref[...], scs_ref[...], gts_ref[...],
            wi_ref, wo_ref, act_ref, lng_ref[...], lnb_ref[...], alpha)


def _ffn(xp, xs, mod_p, mod_s, layer, sub, w_in, w_out, ln_g, ln_b, *, rows_per_batch, alpha):
    m, d = xp.shape
    n_s = xs.shape[0]
    tm = min(FFN_ROWS, rows_per_batch)
    assert m % tm == 0 and rows_per_batch % tm == 0 and n_s <= tm
    n_tiles = m // tm
    last_conv = FF_CONV_STEPS - 1
    tile = lambda i: jnp.clip(i - FF_CONV_STEPS, 0, n_tiles - 1)
    mod_p_spec = lambda j: pl.BlockSpec((None, MOD_ROWS, d), lambda i: (layer, 0, j))
    mod_s_spec = lambda j: pl.BlockSpec((None, n_s, d), lambda i: (layer, 0, j))
    return pl.pallas_call(
        functools.partial(_ffn_kernel, n_tiles=n_tiles, tiles_per_batch=rows_per_batch // tm, alpha=alpha),
        grid=(FF_CONV_STEPS + n_tiles + 1,),
        in_specs=[
            pl.BlockSpec((tm, d), lambda i: (tile(i), 0)),
            pl.BlockSpec((n_s, d), lambda i: (0, 0)),
            mod_p_spec(3 * sub), mod_p_spec(3 * sub + 1), mod_p_spec(3 * sub + 2),
            mod_s_spec(3 * sub), mod_s_spec(3 * sub + 1), mod_s_spec(3 * sub + 2),
            pl.BlockSpec((None, d, FF_CONV_COLS), lambda i: (layer, 0, jnp.minimum(i, last_conv))),
            pl.BlockSpec((None, FF_CONV_ROWS, d), lambda i: (layer, jnp.minimum(i, last_conv), 0)),
            pl.BlockSpec((None, 1, d), lambda i: (layer * 3 + sub, 0, 0)),
            pl.BlockSpec((None, 1, d), lambda i: (layer * 3 + sub, 0, 0)),
        ],
        out_specs=[
            pl.BlockSpec((tm, d), lambda i: (tile(i), 0)),
            pl.BlockSpec((n_s, d), lambda i: (0, 0)),
        ],
        out_shape=[jax.ShapeDtypeStruct((m, d), F32), jax.ShapeDtypeStruct((n_s, d), F32)],
        scratch_shapes=[
            pltpu.VMEM((FF_CONV_STEPS, d, FF_CONV_COLS), BF16),
            pltpu.VMEM((D_FF, d), BF16),
            pltpu.VMEM((tm, D_FF), BF16),
        ],
        compiler_params=pltpu.CompilerParams(
            dimension_semantics=("arbitrary",), vmem_limit_bytes=VMEM_LIMIT_BYTES),
        name="ffn",
    )(xp, xs, mod_p, mod_p, mod_p, mod_s, mod_s, mod_s, w_in, w_out, ln_g, ln_b)


def _attention_block(q_blk, keys, vals, mask, fill):
    outs = []
    for j in range(KV_A):
        kj = keys[:, j * HEAD_DIM:(j + 1) * HEAD_DIM].astype(BF16)
        vj = vals[:, j * HEAD_DIM:(j + 1) * HEAD_DIM].astype(BF16)
        qj = jnp.concatenate(
            [q_blk[:, (j * G_A + g) * HEAD_DIM:(j * G_A + g + 1) * HEAD_DIM] for g in range(G_A)],
            axis=0).astype(BF16)
        s = _dot_t(qj, kj)
        s = jnp.concatenate(
            [jnp.where(mask, s[g * WINDOW:(g + 1) * WINDOW], fill[j][g]) for g in range(G_A)], axis=0)
        p = jnp.exp2(s - jnp.max(s, axis=-1, keepdims=True))
        den = jnp.sum(p, axis=-1, keepdims=True)
        o = _dot(p.astype(BF16), vj) * (1.0 / den)
        outs.extend(o[g * WINDOW:(g + 1) * WINDOW] for g in range(G_A))
    return jnp.concatenate(outs, axis=-1)


def _mixer_prompt_kernel(
        x_ref, sh_ref, sc_ref, gt_ref, win_ref, wout_ref, cos_ref, sl_ref, sh2_ref, sink_ref,
        gg_ref, gb_ref, ws_ref, bs_ref, lb_ref, ng_ref, lng_ref, lnb_ref,
        y_ref, kwin_ref, vwin_ref, st_out_ref,
        kprev_ref, vprev_ref, st_ref,
        *, layer, tq, alpha):
    b_idx = pl.program_id(0)
    t_idx = pl.program_id(1)
    n_t = pl.num_programs(1)

    @pl.when(t_idx == 0)
    def _():
        kprev_ref[...] = jnp.zeros_like(kprev_ref)
        vprev_ref[...] = jnp.zeros_like(vprev_ref)
        st_ref[...] = jnp.zeros_like(st_ref)

    x = x_ref[...]
    shift = sh_ref[pl.ds(b_idx, 1), :]
    scale = sc_ref[pl.ds(b_idx, 1), :]
    gate = gt_ref[pl.ds(b_idx, 1), :]
    h = (x * (1.0 + scale) + shift).astype(BF16)
    proj = _dot(h, win_ref[...])

    cos, s_lo, s_hi = cos_ref[...], sl_ref[...], sh2_ref[...]
    q = _rope(proj[:, OFF_Q:OFF_K], cos, s_lo, s_hi) * (HEAD_DIM ** -0.5 * LOG2_E)
    k = _rope(proj[:, OFF_K:OFF_V], cos, s_lo, s_hi)
    v = proj[:, OFF_V:OFF_U]

    qi = lax.broadcasted_iota(jnp.int32, (WINDOW, 2 * WINDOW), 0)
    kj = lax.broadcasted_iota(jnp.int32, (WINDOW, 2 * WINDOW), 1)
    dist = kj - qi
    band = (dist - 1).astype(jnp.uint32) < jnp.uint32(WINDOW)
    col0 = lax.broadcasted_iota(jnp.int32, (1, 2 * WINDOW), 1) == 0
    fill = [[jnp.where(col0, sink_ref[layer, j * G_A + g] * LOG2_E, NEG_BIG) for g in range(G_A)]
            for j in range(KV_A)]
    first_row = lax.broadcasted_iota(jnp.int32, (WINDOW, D_KV), 0) == 0
    o_a = []
    for blk in range(tq // WINDOW):
        r0 = blk * WINDOW
        if blk == 0:
            k_prev, v_prev = kprev_ref[...], vprev_ref[...]
            lower = jnp.where(t_idx == 0, WINDOW, 0)
            mask = jnp.logical_and(band, kj >= lower)
        else:
            k_prev, v_prev = k[r0 - WINDOW:r0], v[r0 - WINDOW:r0]
            mask = band
        keys = jnp.concatenate([k_prev, k[r0:r0 + WINDOW]], axis=0)
        vals = jnp.concatenate([jnp.where(first_row, 0.0, v_prev), v[r0:r0 + WINDOW]], axis=0)
        o_a.append(_attention_block(q[r0:r0 + WINDOW], keys, vals, mask, fill))
    o_a = jnp.concatenate(o_a, axis=0)
    kprev_ref[...] = k[tq - WINDOW:]
    vprev_ref[...] = v[tq - WINDOW:]
    kwin_ref[...] = k[tq - WINDOW:]
    vwin_ref[...] = v[tq - WINDOW:]

    vn = _layer_norm(proj[:, OFF_GV:OFF_CQ], gg_ref[...], gb_ref[...])
    u = proj[:, OFF_U:OFF_GV]
    tr = lax.broadcasted_iota(jnp.int32, (CHUNK_B, CHUNK_B), 0)
    tc = lax.broadcasted_iota(jnp.int32, (CHUNK_B, CHUNK_B), 1)
    wm = [jnp.where(tr >= tc, ws_ref[g], 0.0).astype(BF16) for g in range(G_B)]
    o_b = []
    for c in range(tq // CHUNK_B):
        vc = vn[c * CHUNK_B:(c + 1) * CHUNK_B].astype(BF16)
        mixed = jnp.concatenate(
            [_dot(wm[g], vc[:, g * HEAD_DIM:(g + 1) * HEAD_DIM]) for g in range(G_B)], axis=-1)
        o_b.append(u[c * CHUNK_B:(c + 1) * CHUNK_B] * (mixed + bs_ref[...]))
    o_b = jnp.concatenate(o_b, axis=0)

    n_ch = tq // HG_CHUNK
    half = HG_CHUNK // 2
    cq = proj[:, OFF_CQ:OFF_CF]
    ci = proj[:, OFF_CI:OFF_CG]
    lb = _hgrn_lower_bound(lb_ref, layer)
    f = lb + (1.0 - lb) * jax.nn.sigmoid(proj[:, OFF_CF:OFF_CI])
    lf = jnp.log(f)
    kc = 1.0 - f

    sr = min(tq, MXU_DIM)
    rt = lax.broadcasted_iota(jnp.int32, (sr, sr), 0)
    ct = lax.broadcasted_iota(jnp.int32, (sr, sr), 1)
    ltri = jnp.where(ct <= rt, jnp.where(ct >= (rt & -HG_CHUNK), 1.0, 0.0), 0.0).astype(BF16)
    lf_hi = lf.astype(BF16)
    lf_r = lf - lf_hi.astype(F32)
    lf_mid = lf_r.astype(BF16)
    lf_lo = (lf_r - lf_mid.astype(F32)).astype(BF16)
    bsum = jnp.concatenate(
        [_dot(ltri, lf_hi[r:r + sr]) + _dot(ltri, lf_mid[r:r + sr]) + _dot(ltri, lf_lo[r:r + sr])
         for r in range(0, tq, sr)], axis=0) * LOG2_E

    def chunks(a):
        return a.reshape(n_ch, HG_CHUNK, D_C)

    b3, cq3, ci3 = chunks(bsum), chunks(cq), chunks(ci)
    b_end = b3[:, HG_CHUNK - 1:HG_CHUNK, :]
    eb = jnp.exp2(bsum)
    qb = cq * eb
    kb = kc * jnp.exp2(b_end - b3).reshape(tq, D_C)

    ones_bd = _head_block_ones()
    c3 = chunks(bsum - jnp.log2(jnp.maximum(kc, 0.0)))

    def pair_products(s, lo):
        e = cq3[:, lo:] * jnp.exp2(b3[:, lo:] - c3[:, s:s + 1, :])
        pos = lax.broadcasted_iota(jnp.int32, (n_ch, half, D_C), 1) + (s // half) * half
        grp = s // half - lo // half
        parts = [e[:, g * half:(g + 1) * half] for g in range((HG_CHUNK - lo) // half)]
        parts[grp] = jnp.where(pos >= s, parts[grp], 0.0)
        return parts[0] if len(parts) == 1 else jnp.concatenate(parts, axis=1)

    def head_sums(slab):
        return chunks(_dot(slab.reshape(tq, D_C).astype(BF16), ones_bd))

    o_all = jnp.zeros((n_ch, HG_CHUNK, D_C), F32)
    for s in range(half):
        o_all = o_all + head_sums(pair_products(s, 0)) * ci3[:, s:s + 1, :]
    o_up = jnp.zeros((n_ch, half, D_C), F32)
    for s in range(half, HG_CHUNK, 2):
        a3 = head_sums(jnp.concatenate([pair_products(s, half), pair_products(s + 1, half)], axis=1))
        o_up = o_up + a3[:, :half] * ci3[:, s:s + 1, :] + a3[:, half:] * ci3[:, s + 1:s + 2, :]
    o_c = (o_all + jnp.concatenate([jnp.zeros((n_ch, half, D_C), F32), o_up], axis=1)).reshape(tq, D_C)

    lane_head = lax.broadcasted_iota(jnp.int32, (tq, D_C), 1) >> HEAD_SHIFT
    im_h = [jnp.where(lane_head == hh, ci, 0.0).astype(BF16) for hh in range(H_C)]
    qb_h = [qb[:, hh * DK_C:(hh + 1) * DK_C].astype(BF16) for hh in range(H_C)]
    kb_h = [kb[:, hh * DK_C:(hh + 1) * DK_C].astype(BF16) for hh in range(H_C)]
    eb_t = eb.T
    lane_head_c = lax.broadcasted_iota(jnp.int32, (HG_CHUNK, D_C), 1) >> HEAD_SHIFT
    st = st_ref[...]
    o_inter = []
    for n in range(n_ch):
        r0, r1 = n * HG_CHUNK, (n + 1) * HG_CHUNK
        read = _dot(jnp.concatenate([a[r0:r1] for a in qb_h], axis=0), st.astype(BF16))
        o_n = jnp.zeros((HG_CHUNK, D_C), F32)
        for hh in range(H_C):
            o_n = o_n + jnp.where(lane_head_c == hh, read[hh * HG_CHUNK:(hh + 1) * HG_CHUNK], 0.0)
        o_inter.append(o_n)
        upd = _tdot(jnp.concatenate([a[r0:r1] for a in kb_h], axis=0),
                    jnp.concatenate([a[r0:r1] for a in im_h], axis=0))
        col = eb_t[:, r1 - 1:r1]
        decay = jnp.concatenate(
            [jnp.broadcast_to(col[hh * DK_C:(hh + 1) * DK_C], (DK_C, DV_C)) for hh in range(H_C)], axis=1)
        st = decay * st + upd
    st_ref[...] = st
    o_c = o_c + jnp.concatenate(o_inter, axis=0)

    @pl.when(t_idx == n_t - 1)
    def _():
        for hh in range(H_C):
            st_out_ref[hh * DK_C:(hh + 1) * DK_C, :] = st[:, hh * DV_C:(hh + 1) * DV_C]

    ms = _head_sum(o_c * o_c, ones_bd) * (1.0 / DV_C)
    o_c = o_c * lax.rsqrt(ms + RMS_EPS) * ng_ref[...] * _silu(proj[:, OFF_CG:])

    o_cat = jnp.concatenate([o_a, o_b, o_c], axis=-1).astype(BF16)
    br = tq // MIX_OUT_BLOCKS if tq % (MIX_OUT_BLOCKS * MXU_DIM) == 0 else tq
    for r in range(0, tq, br):
        z = alpha * x[r:r + br] + (1.0 + gate) * _dot(o_cat[r:r + br], wout_ref[...])
        y_ref[r:r + br, :] = _layer_norm(z, lng_ref[...], lnb_ref[...])


def _mixer_prompt(x, mods, layer, w_in, w_out, tables, sinks, gln_g, gln_b, ws, bs_exp, lb, ng_exp,
                  ln_g, ln_b, *, batch, seq, alpha):
    m, d = x.shape
    tq = min(MIX_ROWS, seq)
    assert seq % tq == 0 and tq % WINDOW == 0
    n_t = seq // tq
    depth = w_in.shape[0]
    cos, s_lo, s_hi = tables
    mod_spec = lambda j: pl.BlockSpec((None, MOD_ROWS, d), lambda b, t: (layer, 0, j))
    tab_spec = pl.BlockSpec((tq, LANES), lambda b, t: (t, 0))
    lay3 = lambda shape: pl.BlockSpec((None,) + shape, lambda b, t: (layer,) + (0,) * len(shape))
    return pl.pallas_call(
        functools.partial(_mixer_prompt_kernel, layer=layer, tq=tq, alpha=alpha),
        grid=(batch, n_t),
        in_specs=[
            pl.BlockSpec((tq, d), lambda b, t: (b * n_t + t, 0)),
            mod_spec(3), mod_spec(4), mod_spec(5),
            lay3((d, D_IN)), lay3((d, d)),
            tab_spec, tab_spec, tab_spec,
            pl.BlockSpec(memory_space=pltpu.SMEM),
            lay3((1, D_B)), lay3((1, D_B)),
            lay3((G_B, CHUNK_B, CHUNK_B)), lay3((CHUNK_B, D_B)),
            pl.BlockSpec((depth, D_C), lambda b, t: (0, 0)),
            lay3((1, D_C)),
            pl.BlockSpec((None, 1, d), lambda b, t: (layer * 3 + 1, 0, 0)),
            pl.BlockSpec((None, 1, d), lambda b, t: (layer * 3 + 1, 0, 0)),
        ],
        out_specs=[
            pl.BlockSpec((tq, d), lambda b, t: (b * n_t + t, 0)),
            pl.BlockSpec((None, WINDOW, D_KV), lambda b, t: (b, 0, 0)),
            pl.BlockSpec((None, WINDOW, D_KV), lambda b, t: (b, 0, 0)),
            pl.BlockSpec((None, D_C, DV_C), lambda b, t: (b, 0, 0)),
        ],
        out_shape=[
            jax.ShapeDtypeStruct((m, d), F32),
            jax.ShapeDtypeStruct((batch, WINDOW, D_KV), F32),
            jax.ShapeDtypeStruct((batch, WINDOW, D_KV), F32),
            jax.ShapeDtypeStruct((batch, D_C, DV_C), F32),
        ],
        scratch_shapes=[
            pltpu.VMEM((WINDOW, D_KV), F32),
            pltpu.VMEM((WINDOW, D_KV), F32),
            pltpu.VMEM((DK_C, D_C), F32),
        ],
        compiler_params=pltpu.CompilerParams(
            dimension_semantics=("arbitrary", "arbitrary"), vmem_limit_bytes=VMEM_LIMIT_BYTES),
        name="mixer_prompt",
    )(x, mods, mods, mods, w_in, w_out, cos, s_lo, s_hi, sinks, gln_g, gln_b, ws, bs_exp, lb, ng_exp,
      ln_g, ln_b)


def _mixer_sample_kernel(
        x_ref, sh_ref, sc_ref, gt_ref, win_ref, wout_ref, cos_ref, sl_ref, sh2_ref, sink_ref,
        gg_ref, gb_ref, w0_ref, bs_ref, lb_ref, ng_ref, lng_ref, lnb_ref, ck_ref, cv_ref, s0_ref,
        y_ref, knew_ref, vnew_ref, gv_ref, s1_ref,
        qt_ref, kn_ref, vn_ref, sn_ref, sc_scr, ot_ref, ft_ref, kt_ref, qc_ref, it_ref, ob_ref, og_ref, oc_ref,
        *, layer, n_s, nb, alpha):
    step = pl.program_id(0)
    rows_all = n_s * H_A

    @pl.when(step == 0)
    def _():
        x = x_ref[...]
        h = (x * (1.0 + sc_ref[...]) + sh_ref[...]).astype(BF16)
        proj = _dot(h, win_ref[...])
        cos, s_lo, s_hi = cos_ref[...], sl_ref[...], sh2_ref[...]
        q = _rope(proj[:, OFF_Q:OFF_K], cos, s_lo, s_hi) * (HEAD_DIM ** -0.5)
        k = _rope(proj[:, OFF_K:OFF_V], cos, s_lo, s_hi)
        v = proj[:, OFF_V:OFF_U]
        knew_ref[...] = k
        vnew_ref[...] = v
        zeros64 = jnp.zeros((n_s, HEAD_DIM), F32)
        for hd in range(H_A):
            qh = q[:, hd * HEAD_DIM:(hd + 1) * HEAD_DIM]
            row = jnp.concatenate([qh, zeros64] if hd < G_A else [zeros64, qh], axis=-1)
            qt_ref[pl.ds(hd, n_s, stride=H_A), :] = row
            kn_ref[pl.ds(hd, n_s, stride=H_A), :] = k
            vn_ref[pl.ds(hd, n_s, stride=H_A), :] = v
        s_new = jnp.sum(qt_ref[...] * kn_ref[...], axis=-1, keepdims=True)
        sn_ref[...] = jnp.broadcast_to(s_new, (rows_all, LANES))

        vn = _layer_norm(proj[:, OFF_GV:OFF_CQ], gg_ref[...], gb_ref[...])
        gv_ref[...] = vn
        ob_ref[...] = proj[:, OFF_U:OFF_GV] * (vn * w0_ref[...] + bs_ref[0:1, :])

        lb = _hgrn_lower_bound(lb_ref, layer)
        f = lb + (1.0 - lb) * jax.nn.sigmoid(proj[:, OFF_CF:OFF_CI])
        ft_ref[...] = f.T
        kt_ref[...] = (1.0 - f).T
        qc_ref[...] = proj[:, OFF_CQ:OFF_CF].T
        it_ref[...] = proj[:, OFF_CI:OFF_CG].T
        og_ref[...] = ng_ref[...] * _silu(proj[:, OFF_CG:])

    base = pl.multiple_of(step * (nb * H_A), nb * H_A)
    for b in range(nb):
        qrows = qt_ref[pl.ds(base + b * H_A, H_A), :].astype(BF16)
        sc_scr[b * H_A:(b + 1) * H_A, :] = _dot(qrows, ck_ref[b].astype(BF16))
    lane = lax.broadcasted_iota(jnp.int32, (nb * H_A, WINDOW), 1)
    s = jnp.where(lane == 0, NEG_BIG, sc_scr[...])
    s_new = sn_ref[pl.ds(base, nb * H_A), 0:1]
    hrow = lax.broadcasted_iota(jnp.int32, (nb * H_A, 1), 0) & (H_A - 1)
    sink = jnp.zeros((nb * H_A, 1), F32)
    for hd in range(H_A):
        sink = jnp.where(hrow == hd, sink_ref[layer, hd], sink)
    mx = jnp.maximum(jnp.maximum(jnp.max(s, axis=-1, keepdims=True), s_new), sink)
    p = jnp.exp(s - mx)
    p_new = jnp.exp(s_new - mx)
    den = jnp.sum(p, axis=-1, keepdims=True) + p_new + jnp.exp(sink - mx)
    sc_scr[...] = p
    for b in range(nb):
        prow = sc_scr[b * H_A:(b + 1) * H_A, :].astype(BF16)
        ot_ref[pl.ds(base + b * H_A, H_A), :] = _dot_t(prow, cv_ref[b].astype(BF16))
    rows = pl.ds(base, nb * H_A)
    ot_ref[rows, :] = (ot_ref[rows, :] + p_new * vn_ref[rows, :]) * (1.0 / den)

    hrow0 = pl.multiple_of(step * DK_C, DK_C)
    i_h = it_ref[pl.ds(hrow0, DV_C), :]
    acc = jnp.zeros((DV_C, n_s), F32)
    for kk in range(DK_C):
        f_k = ft_ref[pl.ds(hrow0 + kk, 1), :]
        k_k = kt_ref[pl.ds(hrow0 + kk, 1), :]
        q_k = qc_ref[pl.ds(hrow0 + kk, 1), :]
        s1 = f_k * s0_ref[kk] + k_k * i_h
        s1_ref[kk] = s1
        acc = acc + q_k * s1
    oc_ref[pl.ds(hrow0, DV_C), :] = acc

    @pl.when(step == pl.num_programs(0) - 1)
    def _():
        o_a = []
        for hd in range(H_A):
            r = ot_ref[pl.ds(hd, n_s, stride=H_A), :]
            o_a.append(r[:, :HEAD_DIM] if hd < G_A else r[:, HEAD_DIM:])
        o_c = oc_ref[...].T
        ms = _head_sum(o_c * o_c, _head_block_ones()) * (1.0 / DV_C)
        o_c = o_c * lax.rsqrt(ms + RMS_EPS) * og_ref[...]
        mix = _dot(jnp.concatenate(o_a + [ob_ref[...], o_c], axis=-1).astype(BF16), wout_ref[...])
        z = alpha * x_ref[...] + (1.0 + gt_ref[...]) * mix
        y_ref[...] = _layer_norm(z, lng_ref[...], lnb_ref[...])


def _mixer_sample(x, mods, layer, w_in, w_out, tables, sinks, gln_g, gln_b, w0_exp, bs_exp, lb, ng_exp,
                  ln_g, ln_b, cache_k, cache_v, state, *, alpha):
    n_s, d = x.shape
    assert n_s % (H_C * SUBLANES) == 0 and n_s % LANES == 0
    nb = n_s // H_C
    depth = w_in.shape[0]
    cos, s_lo, s_hi = tables
    mod_spec = lambda j: pl.BlockSpec((None, n_s, d), lambda i: (layer, 0, j))
    tab_spec = pl.BlockSpec((1, LANES), lambda i: (0, 0))
    lay3 = lambda shape: pl.BlockSpec((None,) + shape, lambda i: (layer,) + (0,) * len(shape))
    rows = pl.BlockSpec((n_s, d), lambda i: (0, 0))
    per_head = pltpu.VMEM((D_C, n_s), F32)
    per_row = pltpu.VMEM((n_s * H_A, LANES), F32)
    return pl.pallas_call(
        functools.partial(_mixer_sample_kernel, layer=layer, n_s=n_s, nb=nb, alpha=alpha),
        grid=(H_C,),
        in_specs=[
            rows,
            mod_spec(3), mod_spec(4), mod_spec(5),
            lay3((d, D_IN)), lay3((d, d)),
            tab_spec, tab_spec, tab_spec,
            pl.BlockSpec(memory_space=pltpu.SMEM),
            lay3((1, D_B)), lay3((1, D_B)), lay3((1, D_B)), lay3((CHUNK_B, D_B)),
            pl.BlockSpec((depth, D_C), lambda i: (0, 0)),
            lay3((1, D_C)),
            pl.BlockSpec((None, 1, d), lambda i: (layer * 3 + 1, 0, 0)),
            pl.BlockSpec((None, 1, d), lambda i: (layer * 3 + 1, 0, 0)),
            pl.BlockSpec((None, nb, D_KV, WINDOW), lambda i: (layer, i, 0, 0)),
            pl.BlockSpec((None, nb, D_KV, WINDOW), lambda i: (layer, i, 0, 0)),
            pl.BlockSpec((None, None, DK_C, DV_C, n_s), lambda i: (layer, i, 0, 0, 0)),
        ],
        out_specs=[
            rows,
            pl.BlockSpec((n_s, D_KV), lambda i: (0, 0)),
            pl.BlockSpec((n_s, D_KV), lambda i: (0, 0)),
            pl.BlockSpec((n_s, D_B), lambda i: (0, 0)),
            pl.BlockSpec((None, DK_C, DV_C, n_s), lambda i: (i, 0, 0, 0)),
        ],
        out_shape=[
            jax.ShapeDtypeStruct((n_s, d), F32),
            jax.ShapeDtypeStruct((n_s, D_KV), F32),
            jax.ShapeDtypeStruct((n_s, D_KV), F32),
            jax.ShapeDtypeStruct((n_s, D_B), F32),
            jax.ShapeDtypeStruct((H_C, DK_C, DV_C, n_s), F32),
        ],
        scratch_shapes=[
            per_row,
            per_row,
            per_row,
            per_row,
            pltpu.VMEM((nb * H_A, WINDOW), F32),
            per_row,
            per_head, per_head, per_head, per_head,
            pltpu.VMEM((n_s, D_B), F32),
            pltpu.VMEM((n_s, D_C), F32),
            per_head,
        ],
        compiler_params=pltpu.CompilerParams(
            dimension_semantics=("arbitrary",), vmem_limit_bytes=VMEM_LIMIT_BYTES),
        name="mixer_sample",
    )(x, mods, mods, mods, w_in, w_out, cos, s_lo, s_hi, sinks, gln_g, gln_b, w0_exp, bs_exp, lb, ng_exp,
      ln_g, ln_b, cache_k, cache_v, state)


def _rope_tables(positions):
    half = ROT_DIM // 2
    f32 = np.float32
    inv = (ROPE_THETA ** (-np.arange(half, dtype=f32) * 2.0 / ROT_DIM)).astype(f32)
    ang = (positions.astype(f32)[:, None] * inv[None, :]).astype(f32)
    cos, sin = np.cos(ang).astype(f32), np.sin(ang).astype(f32)
    n = positions.shape[0]
    rest = HEAD_DIM - ROT_DIM
    cos_h = np.concatenate([cos, cos, np.ones((n, rest), f32)], axis=-1)
    lo_h = np.concatenate([-sin, np.zeros((n, half + rest), f32)], axis=-1)
    hi_h = np.concatenate([np.zeros((n, half), f32), sin, np.zeros((n, rest), f32)], axis=-1)
    two = lambda a: jnp.asarray(np.concatenate([a, a], axis=-1))
    return two(cos_h), two(lo_h), two(hi_h)


def kernel(x_prompt, x_sample, cache_k, cache_v, state_hgrn, c_prompt, c_sample, w_in, w_out, attn_sinks,
           gmlp_ln_g, gmlp_ln_b, gmlp_ws, gmlp_bs, hgrn_lb, hgrn_norm_g, ffn1_in, ffn1_out, ffn2_in,
           ffn2_out, ada_w, ada_b, ln_g, ln_b):
    batch, seq, d = x_prompt.shape
    n_s, dec_seq, _ = x_sample.shape
    depth = w_in.shape[0]
    assert d == D_MODEL and dec_seq == 1 and batch <= MOD_ROWS
    assert cache_k.shape[2] == WINDOW
    alpha = (2 * depth) ** 0.25

    w_in_b, w_out_b = w_in.astype(BF16), w_out.astype(BF16)
    ln_g3 = ln_g.reshape(depth * 3, 1, d)
    ln_b3 = ln_b.reshape(depth * 3, 1, d)
    gln_g = gmlp_ln_g.reshape(depth, 1, D_B)
    gln_b = gmlp_ln_b.reshape(depth, 1, D_B)
    bs_exp = jnp.repeat(jnp.swapaxes(gmlp_bs, 1, 2), HEAD_DIM, axis=2)
    w0_exp = jnp.repeat(gmlp_ws[:, :, 0, 0], HEAD_DIM, axis=1).reshape(depth, 1, D_B)
    ng_exp = jnp.tile(hgrn_norm_g, (1, H_C)).reshape(depth, 1, D_C)
    tab_p = _rope_tables(np.arange(seq))
    tab_s = _rope_tables(PAST_LEN + np.arange(dec_seq))

    c_all = jnp.concatenate([c_prompt, jnp.zeros((MOD_ROWS - batch, d), F32), c_sample], axis=0)
    mod_p, mod_s = _adaln(c_all, ada_w, ada_b)

    ck = jnp.transpose(cache_k, (0, 1, 3, 4, 2)).reshape(depth, n_s, D_KV, WINDOW)
    cv = jnp.transpose(cache_v, (0, 1, 3, 4, 2)).reshape(depth, n_s, D_KV, WINDOW)
    s0 = jnp.transpose(state_hgrn, (0, 2, 3, 4, 1))

    xp = x_prompt.reshape(batch * seq, d)
    xs = x_sample.reshape(n_s, d)
    ffn = functools.partial(_ffn, rows_per_batch=seq, alpha=alpha)
    kw, vw, st_p, kn, vnw, gvs, st_s = [], [], [], [], [], [], []
    for l in range(depth):
        xp, xs = ffn(xp, xs, mod_p, mod_s, l, 0, ffn1_in, ffn1_out, ln_g3, ln_b3)
        xp, k_l, v_l, s_l = _mixer_prompt(
            xp, mod_p, l, w_in_b, w_out_b, tab_p, attn_sinks, gln_g, gln_b, gmlp_ws, bs_exp, hgrn_lb,
            ng_exp, ln_g3, ln_b3, batch=batch, seq=seq, alpha=alpha)
        kw.append(k_l), vw.append(v_l), st_p.append(s_l)
        xs, k_l, v_l, g_l, s_l = _mixer_sample(
            xs, mod_s, l, w_in_b, w_out_b, tab_s, attn_sinks, gln_g, gln_b, w0_exp, bs_exp, hgrn_lb,
            ng_exp, ln_g3, ln_b3, ck, cv, s0, alpha=alpha)
        kn.append(k_l), vnw.append(v_l), gvs.append(g_l), st_s.append(s_l)
        xp, xs = ffn(xp, xs, mod_p, mod_s, l, 2, ffn2_in, ffn2_out, ln_g3, ln_b3)

    stack = lambda parts, shape: jnp.stack(parts, axis=0).reshape((depth,) + shape)
    return (
        xp.reshape(batch, seq, d),
        xs.reshape(n_s, dec_seq, d),
        stack(kw, (batch, WINDOW, KV_A, HEAD_DIM)),
        stack(vw, (batch, WINDOW, KV_A, HEAD_DIM)),
        stack(st_p, (batch, H_C, DK_C, DV_C)),
        stack(kn, (n_s, dec_seq, KV_A, HEAD_DIM)),
        stack(vnw, (n_s, dec_seq, KV_A, HEAD_DIM)),
        stack(gvs, (n_s, dec_seq, D_B)),
        jnp.transpose(jnp.stack(st_s, axis=0), (0, 4, 1, 2, 3)),
    )
```

```python
import functools

import numpy as np
import jax
import jax.numpy as jnp
from jax import lax
from jax.experimental import pallas as pl
from jax.experimental.pallas import tpu as pltpu

F32 = jnp.float32
BF16 = jnp.bfloat16

D_MODEL = 1024
HEAD_DIM = 64
H_A = 8
KV_A = 2
G_A = H_A // KV_A
D_A = H_A * HEAD_DIM
D_KV = KV_A * HEAD_DIM
WINDOW = 128
ROT_DIM = HEAD_DIM // 4
ROPE_THETA = 500000.0
G_B = 4
D_B = G_B * HEAD_DIM
CHUNK_B = 128
H_C = 4
DK_C = 64
DV_C = 64
D_C = H_C * DV_C
HEAD_SHIFT = DV_C.bit_length() - 1
assert 1 << HEAD_SHIFT == DV_C == DK_C
D_IN = 2304
D_FF = 2816
N_MOD = 9
LN_EPS = 1e-5
RMS_EPS = 1e-6
NEG_BIG = -1e30
LOG2_E = 1.4426950408889634
PAST_LEN = 16384

OFF_Q = 0
OFF_K = OFF_Q + D_A
OFF_V = OFF_K + D_KV
OFF_U = OFF_V + D_KV
OFF_GV = OFF_U + D_B
OFF_CQ = OFF_GV + D_B
OFF_CF = OFF_CQ + D_C
OFF_CI = OFF_CF + D_C
OFF_CG = OFF_CI + D_C

SUBLANES = 8
LANES = 128
MXU_DIM = 256
VMEM_LIMIT_BYTES = 56 * 1024 * 1024

FFN_ROWS = 1024
FFN_OUT_BLOCKS = 4
MIX_OUT_BLOCKS = 2
FF_CHUNK = MXU_DIM
FF_CONV_STEPS = D_FF // FF_CHUNK
FF_CONV_COLS = 2 * D_FF // FF_CONV_STEPS
FF_CONV_ROWS = D_FF // FF_CONV_STEPS
ADA_COLS = 2304
MIX_ROWS = 1024
HG_CHUNK = 16
MOD_ROWS = SUBLANES


def _dot(a, b):
    return jnp.dot(a, b, preferred_element_type=F32)


def _dot_t(a, b):
    return lax.dot_general(a, b, (((1,), (1,)), ((), ())), preferred_element_type=F32)


def _tdot(a, b):
    return lax.dot_general(a, b, (((0,), (0,)), ((), ())), preferred_element_type=F32)


def _layer_norm(z, g, b):
    mu = jnp.mean(z, axis=-1, keepdims=True)
    zc = z - mu
    var = jnp.mean(zc * zc, axis=-1, keepdims=True)
    return zc * lax.rsqrt(var + LN_EPS) * g + b


def _silu(x):
    return x * jax.nn.sigmoid(x)


def _head_block_ones():
    r = lax.broadcasted_iota(jnp.int32, (D_C, D_C), 0) >> HEAD_SHIFT
    c = lax.broadcasted_iota(jnp.int32, (D_C, D_C), 1) >> HEAD_SHIFT
    return jnp.where(r == c, 1.0, 0.0).astype(BF16)


def _head_sum(x, ones_bd):
    hi = x.astype(BF16)
    lo = (x - hi.astype(F32)).astype(BF16)
    return _dot(hi, ones_bd) + _dot(lo, ones_bd)


def _rope(x, cos, sin_lo, sin_hi):
    half = ROT_DIM // 2
    out = []
    for c in range(x.shape[-1] // LANES):
        xc = x[:, c * LANES:(c + 1) * LANES]
        out.append(xc * cos + pltpu.roll(xc, LANES - half, 1) * sin_lo + pltpu.roll(xc, half, 1) * sin_hi)
    return out[0] if len(out) == 1 else jnp.concatenate(out, axis=-1)


def _hgrn_lower_bound(lb_ref, layer):
    w = lb_ref[...]
    e = jnp.exp(w - jnp.max(w, axis=0, keepdims=True))
    p = e / jnp.sum(e, axis=0, keepdims=True)
    lb = jnp.zeros((1, D_C), F32)
    for j in range(1, layer + 1):
        lb = lb + p[j:j + 1, :]
    return lb


def _adaln_kernel(c_ref, wl_ref, wh_ref, b_ref, op_ref, os_ref):
    s = _silu(c_ref[...]).astype(BF16)
    r = jnp.concatenate([_dot(s, wl_ref[...].astype(BF16)), _dot(s, wh_ref[...].astype(BF16))], axis=1) + b_ref[...]
    op_ref[...] = r[:MOD_ROWS]
    os_ref[...] = r[MOD_ROWS:]


def _adaln(c_all, ada_w, ada_b):
    depth, d, n = ada_w.shape
    rows = c_all.shape[0]
    n_s = rows - MOD_ROWS
    return pl.pallas_call(
        _adaln_kernel,
        grid=(depth, n // ADA_COLS),
        in_specs=[
            pl.BlockSpec((rows, d), lambda l, j: (0, 0)),
            pl.BlockSpec((None, d, ADA_COLS // 2), lambda l, j: (l, 0, 2 * j)),
            pl.BlockSpec((None, d, ADA_COLS // 2), lambda l, j: (l, 0, 2 * j + 1)),
            pl.BlockSpec((None, 1, ADA_COLS), lambda l, j: (l, 0, j)),
        ],
        out_specs=[
            pl.BlockSpec((None, MOD_ROWS, ADA_COLS), lambda l, j: (l, 0, j)),
            pl.BlockSpec((None, n_s, ADA_COLS), lambda l, j: (l, 0, j)),
        ],
        out_shape=[
            jax.ShapeDtypeStruct((depth, MOD_ROWS, n), F32),
            jax.ShapeDtypeStruct((depth, n_s, n), F32),
        ],
        compiler_params=pltpu.CompilerParams(
            dimension_semantics=("arbitrary", "arbitrary"), vmem_limit_bytes=VMEM_LIMIT_BYTES),
        name="adaln",
    )(c_all, ada_w, ada_w, ada_b.reshape(depth, 1, n))


def _ffn_rows(x, shift, scale, gate, wi_ref, wo_ref, act_ref, lng, lnb, alpha):
    rows = x.shape[0]
    h = (x * (1.0 + scale) + shift).astype(BF16)
    for c in range(D_FF // FF_CHUNK):
        lo = c * FF_CHUNK
        ga = D_FF + lo
        a = _dot(h, wi_ref[lo // FF_CONV_COLS, :, lo % FF_CONV_COLS:lo % FF_CONV_COLS + FF_CHUNK])
        g = _dot(h, wi_ref[ga // FF_CONV_COLS, :, ga % FF_CONV_COLS:ga % FF_CONV_COLS + FF_CHUNK])
        act_ref[0:rows, lo:lo + FF_CHUNK] = (_silu(g) * a).astype(BF16)
    br = rows // FFN_OUT_BLOCKS if rows % (FFN_OUT_BLOCKS * MXU_DIM) == 0 else rows
    half_gate = 0.5 * (1.0 + gate)
    out = []
    for r in range(0, rows, br):
        y = _dot(act_ref[r:r + br, :], wo_ref[...])
        hg = half_gate if half_gate.shape[0] == 1 else half_gate[r:r + br]
        out.append(_layer_norm(alpha * x[r:r + br] + hg * y, lng, lnb))
    return out[0] if len(out) == 1 else jnp.concatenate(out, axis=0)


def _ffn_kernel(xp_ref, xs_ref, shp_ref, scp_ref, gtp_ref, shs_ref, scs_ref, gts_ref, win_ref, wout_ref,
                lng_ref, lnb_ref, op_ref, os_ref, wi_ref, wo_ref, act_ref, *, n_tiles, tiles_per_batch, alpha):
    step = pl.program_id(0)

    @pl.when(step < FF_CONV_STEPS)
    def _():
        wi_ref[step] = win_ref[...].astype(BF16)
        r0 = pl.multiple_of(step * FF_CONV_ROWS, FF_CONV_ROWS)
        wo_ref[pl.ds(r0, FF_CONV_ROWS), :] = wout_ref[...].astype(BF16)

    @pl.when(jnp.logical_and(step >= FF_CONV_STEPS, step < FF_CONV_STEPS + n_tiles))
    def _():
        batch = (step - FF_CONV_STEPS) // tiles_per_batch
        op_ref[...] = _ffn_rows(
            xp_ref[...], shp_ref[pl.ds(batch, 1), :], scp_ref[pl.ds(batch, 1), :], gtp_ref[pl.ds(batch, 1), :],
            wi_ref, wo_ref, act_ref, lng_ref[...], lnb_ref[...], alpha)

    @pl.when(step == FF_CONV_STEPS + n_tiles)
    def _():
        os_ref[...] = _ffn_rows(
            xs_ref[...], shs_ref[...], scs_ref[...], gts_ref[...],
            wi_ref, wo_ref, act_ref, lng_ref[...], lnb_ref[...], alpha)


def _ffn(xp, xs, mod_p, mod_s, layer, sub, w_in, w_out, ln_g, ln_b, *, rows_per_batch, alpha):
    m, d = xp.shape
    n_s = xs.shape[0]
    tm = min(FFN_ROWS, rows_per_batch)
    assert m % tm == 0 and rows_per_batch % tm == 0 and n_s <= tm
    n_tiles = m // tm
    last_conv = FF_CONV_STEPS - 1
    tile = lambda i: jnp.clip(i - FF_CONV_STEPS, 0, n_tiles - 1)
    mod_p_spec = lambda j: pl.BlockSpec((None, MOD_ROWS, d), lambda i: (layer, 0, j))
    mod_s_spec = lambda j: pl.BlockSpec((None, n_s, d), lambda i: (layer, 0, j))
    return pl.pallas_call(
        functools.partial(_ffn_kernel, n_tiles=n_tiles, tiles_per_batch=rows_per_batch // tm, alpha=alpha),
        grid=(FF_CONV_STEPS + n_tiles + 1,),
        in_specs=[
            pl.BlockSpec((tm, d), lambda i: (tile(i), 0)),
            pl.BlockSpec((n_s, d), lambda i: (0, 0)),
            mod_p_spec(3 * sub), mod_p_spec(3 * sub + 1), mod_p_spec(3 * sub + 2),
            mod_s_spec(3 * sub), mod_s_spec(3 * sub + 1), mod_s_spec(3 * sub + 2),
            pl.BlockSpec((None, d, FF_CONV_COLS), lambda i: (layer, 0, jnp.minimum(i, last_conv))),
            pl.BlockSpec((None, FF_CONV_ROWS, d), lambda i: (layer, jnp.minimum(i, last_conv), 0)),
            pl.BlockSpec((None, 1, d), lambda i: (layer * 3 + sub, 0, 0)),
            pl.BlockSpec((None, 1, d), lambda i: (layer * 3 + sub, 0, 0)),
        ],
        out_specs=[
            pl.BlockSpec((tm, d), lambda i: (tile(i), 0)),
            pl.BlockSpec((n_s, d), lambda i: (0, 0)),
        ],
        out_shape=[jax.ShapeDtypeStruct((m, d), F32), jax.ShapeDtypeStruct((n_s, d), F32)],
        scratch_shapes=[
            pltpu.VMEM((FF_CONV_STEPS, d, FF_CONV_COLS), BF16),
            pltpu.VMEM((D_FF, d), BF16),
            pltpu.VMEM((tm, D_FF), BF16),
        ],
        compiler_params=pltpu.CompilerParams(
            dimension_semantics=("arbitrary",), vmem_limit_bytes=VMEM_LIMIT_BYTES),
        name="ffn",
    )(xp, xs, mod_p, mod_p, mod_p, mod_s, mod_s, mod_s, w_in, w_out, ln_g, ln_b)


def _attention_block(q_blk, keys, vals, mask, fill):
    outs = []
    for j in range(KV_A):
        kj = keys[:, j * HEAD_DIM:(j + 1) * HEAD_DIM].astype(BF16)
        vj = vals[:, j * HEAD_DIM:(j + 1) * HEAD_DIM].astype(BF16)
        qj = jnp.concatenate(
            [q_blk[:, (j * G_A + g) * HEAD_DIM:(j * G_A + g + 1) * HEAD_DIM] for g in range(G_A)],
            axis=0).astype(BF16)
        s = _dot_t(qj, kj)
        s = jnp.concatenate(
            [jnp.where(mask, s[g * WINDOW:(g + 1) * WINDOW], fill[j][g]) for g in range(G_A)], axis=0)
        p = jnp.exp2(s - jnp.max(s, axis=-1, keepdims=True))
        den = jnp.sum(p, axis=-1, keepdims=True)
        o = _dot(p.astype(BF16), vj) * (1.0 / den)
        outs.extend(o[g * WINDOW:(g + 1) * WINDOW] for g in range(G_A))
    return jnp.concatenate(outs, axis=-1)


def _mixer_prompt_kernel(
        x_ref, sh_ref, sc_ref, gt_ref, win_ref, wout_ref, cos_ref, sl_ref, sh2_ref, sink_ref,
        gg_ref, gb_ref, ws_ref, bs_ref, lb_ref, ng_ref, lng_ref, lnb_ref,
        y_ref, kwin_ref, vwin_ref, st_out_ref,
        kprev_ref, vprev_ref, st_ref,
        *, layer, tq, alpha):
    b_idx = pl.program_id(0)
    t_idx = pl.program_id(1)
    n_t = pl.num_programs(1)

    @pl.when(t_idx == 0)
    def _():
        kprev_ref[...] = jnp.zeros_like(kprev_ref)
        vprev_ref[...] = jnp.zeros_like(vprev_ref)
        st_ref[...] = jnp.zeros_like(st_ref)

    x = x_ref[...]
    shift = sh_ref[pl.ds(b_idx, 1), :]
    scale = sc_ref[pl.ds(b_idx, 1), :]
    gate = gt_ref[pl.ds(b_idx, 1), :]
    h = (x * (1.0 + scale) + shift).astype(BF16)
    proj = _dot(h, win_ref[...])

    cos, s_lo, s_hi = cos_ref[...], sl_ref[...], sh2_ref[...]
    q = _rope(proj[:, OFF_Q:OFF_K], cos, s_lo, s_hi) * (HEAD_DIM ** -0.5 * LOG2_E)
    k = _rope(proj[:, OFF_K:OFF_V], cos, s_lo, s_hi)
    v = proj[:, OFF_V:OFF_U]

    qi = lax.broadcasted_iota(jnp.int32, (WINDOW, 2 * WINDOW), 0)
    kj = lax.broadcasted_iota(jnp.int32, (WINDOW, 2 * WINDOW), 1)
    dist = kj - qi
    band = (dist - 1).astype(jnp.uint32) < jnp.uint32(WINDOW)
    col0 = lax.broadcasted_iota(jnp.int32, (1, 2 * WINDOW), 1) == 0
    fill = [[jnp.where(col0, sink_ref[layer, j * G_A + g] * LOG2_E, NEG_BIG) for g in range(G_A)]
            for j in range(KV_A)]
    first_row = lax.broadcasted_iota(jnp.int32, (WINDOW, D_KV), 0) == 0
    o_a = []
    for blk in range(tq // WINDOW):
        r0 = blk * WINDOW
        if blk == 0:
            k_prev, v_prev = kprev_ref[...], vprev_ref[...]
            lower = jnp.where(t_idx == 0, WINDOW, 0)
            mask = jnp.logical_and(band, kj >= lower)
        else:
            k_prev, v_prev = k[r0 - WINDOW:r0], v[r0 - WINDOW:r0]
            mask = band
        keys = jnp.concatenate([k_prev, k[r0:r0 + WINDOW]], axis=0)
        vals = jnp.concatenate([jnp.where(first_row, 0.0, v_prev), v[r0:r0 + WINDOW]], axis=0)
        o_a.append(_attention_block(q[r0:r0 + WINDOW], keys, vals, mask, fill))
    o_a = jnp.concatenate(o_a, axis=0)
    kprev_ref[...] = k[tq - WINDOW:]
    vprev_ref[...] = v[tq - WINDOW:]
    kwin_ref[...] = k[tq - WINDOW:]
    vwin_ref[...] = v[tq - WINDOW:]

    vn = _layer_norm(proj[:, OFF_GV:OFF_CQ], gg_ref[...], gb_ref[...])
    u = proj[:, OFF_U:OFF_GV]
    tr = lax.broadcasted_iota(jnp.int32, (CHUNK_B, CHUNK_B), 0)
    tc = lax.broadcasted_iota(jnp.int32, (CHUNK_B, CHUNK_B), 1)
    wm = [jnp.where(tr >= tc, ws_ref[g], 0.0).astype(BF16) for g in range(G_B)]
    o_b = []
    for c in range(tq // CHUNK_B):
        vc = vn[c * CHUNK_B:(c + 1) * CHUNK_B].astype(BF16)
        mixed = jnp.concatenate(
            [_dot(wm[g], vc[:, g * HEAD_DIM:(g + 1) * HEAD_DIM]) for g in range(G_B)], axis=-1)
        o_b.append(u[c * CHUNK_B:(c + 1) * CHUNK_B] * (mixed + bs_ref[...]))
    o_b = jnp.concatenate(o_b, axis=0)

    n_ch = tq // HG_CHUNK
    half = HG_CHUNK // 2
    cq = proj[:, OFF_CQ:OFF_CF]
    ci = proj[:, OFF_CI:OFF_CG]
    lb = _hgrn_lower_bound(lb_ref, layer)
    f = lb + (1.0 - lb) * jax.nn.sigmoid(proj[:, OFF_CF:OFF_CI])
    lf = jnp.log(f)
    kc = 1.0 - f

    sr = min(tq, MXU_DIM)
    rt = lax.broadcasted_iota(jnp.int32, (sr, sr), 0)
    ct = lax.broadcasted_iota(jnp.int32, (sr, sr), 1)
    ltri = jnp.where(ct <= rt, jnp.where(ct >= (rt & -HG_CHUNK), 1.0, 0.0), 0.0).astype(BF16)
    lf_hi = lf.astype(BF16)
    lf_r = lf - lf_hi.astype(F32)
    lf_mid = lf_r.astype(BF16)
    lf_lo = (lf_r - lf_mid.astype(F32)).astype(BF16)
    bsum = jnp.concatenate(
        [_dot(ltri, lf_hi[r:r + sr]) + _dot(ltri, lf_mid[r:r + sr]) + _dot(ltri, lf_lo[r:r + sr])
         for r in range(0, tq, sr)], axis=0) * LOG2_E

    def chunks(a):
        return a.reshape(n_ch, HG_CHUNK, D_C)

    b3, cq3, ci3 = chunks(bsum), chunks(cq), chunks(ci)
    b_end = b3[:, HG_CHUNK - 1:HG_CHUNK, :]
    eb = jnp.exp2(bsum)
    qb = cq * eb
    kb = kc * jnp.exp2(b_end - b3).reshape(tq, D_C)

    ones_bd = _head_block_ones()
    c3 = chunks(bsum - jnp.log2(jnp.maximum(kc, 0.0)))

    def pair_products(s, lo):
        e = cq3[:, lo:] * jnp.exp2(b3[:, lo:] - c3[:, s:s + 1, :])
        pos = lax.broadcasted_iota(jnp.int32, (n_ch, half, D_C), 1) + (s // half) * half
        grp = s // half - lo // half
        parts = [e[:, g * half:(g + 1) * half] for g in range((HG_CHUNK - lo) // half)]
        parts[grp] = jnp.where(pos >= s, parts[grp], 0.0)
        return parts[0] if len(parts) == 1 else jnp.concatenate(parts, axis=1)

    def head_sums(slab):
        return chunks(_dot(slab.reshape(tq, D_C).astype(BF16), ones_bd))

    o_all = jnp.zeros((n_ch, HG_CHUNK, D_C), F32)
    for s in range(half):
        o_all = o_all + head_sums(pair_products(s, 0)) * ci3[:, s:s + 1, :]
    o_up = jnp.zeros((n_ch, half, D_C), F32)
    for s in range(half, HG_CHUNK, 2):
        a3 = head_sums(jnp.concatenate([pair_products(s, half), pair_products(s + 1, half)], axis=1))
        o_up = o_up + a3[:, :half] * ci3[:, s:s + 1, :] + a3[:, half:] * ci3[:, s + 1:s + 2, :]
    o_c = (o_all + jnp.concatenate([jnp.zeros((n_ch, half, D_C), F32), o_up], axis=1)).reshape(tq, D_C)

    lane_head = lax.broadcasted_iota(jnp.int32, (tq, D_C), 1) >> HEAD_SHIFT
    im_h = [jnp.where(lane_head == hh, ci, 0.0).astype(BF16) for hh in range(H_C)]
    qb_h = [qb[:, hh * DK_C:(hh + 1) * DK_C].astype(BF16) for hh in range(H_C)]
    kb_h = [kb[:, hh * DK_C:(hh + 1) * DK_C].astype(BF16) for hh in range(H_C)]
    eb_t = eb.T
    lane_head_c = lax.broadcasted_iota(jnp.int32, (HG_CHUNK, D_C), 1) >> HEAD_SHIFT
    st = st_ref[...]
    o_inter = []
    for n in range(n_ch):
        r0, r1 = n * HG_CHUNK, (n + 1) * HG_CHUNK
        read = _dot(jnp.concatenate([a[r0:r1] for a in qb_h], axis=0), st.astype(BF16))
        o_n = jnp.zeros((HG_CHUNK, D_C), F32)
        for hh in range(H_C):
            o_n = o_n + jnp.where(lane_head_c == hh, read[hh * HG_CHUNK:(hh + 1) * HG_CHUNK], 0.0)
        o_inter.append(o_n)
        upd = _tdot(jnp.concatenate([a[r0:r1] for a in kb_h], axis=0),
                    jnp.concatenate([a[r0:r1] for a in im_h], axis=0))
        col = eb_t[:, r1 - 1:r1]
        decay = jnp.concatenate(
            [jnp.broadcast_to(col[hh * DK_C:(hh + 1) * DK_C], (DK_C, DV_C)) for hh in range(H_C)], axis=1)
        st = decay * st + upd
    st_ref[...] = st
    o_c = o_c + jnp.concatenate(o_inter, axis=0)

    @pl.when(t_idx == n_t - 1)
    def _():
        for hh in range(H_C):
            st_out_ref[hh * DK_C:(hh + 1) * DK_C, :] = st[:, hh * DV_C:(hh + 1) * DV_C]

    ms = _head_sum(o_c * o_c, ones_bd) * (1.0 / DV_C)
    o_c = o_c * lax.rsqrt(ms + RMS_EPS) * ng_ref[...] * _silu(proj[:, OFF_CG:])

    o_cat = jnp.concatenate([o_a, o_b, o_c], axis=-1).astype(BF16)
    br = tq // MIX_OUT_BLOCKS if tq % (MIX_OUT_BLOCKS * MXU_DIM) == 0 else tq
    for r in range(0, tq, br):
        z = alpha * x[r:r + br] + (1.0 + gate) * _dot(o_cat[r:r + br], wout_ref[...])
        y_ref[r:r + br, :] = _layer_norm(z, lng_ref[...], lnb_ref[...])


def _mixer_prompt(x, mods, layer, w_in, w_out, tables, sinks, gln_g, gln_b, ws, bs_exp, lb, ng_exp,
                  ln_g, ln_b, *, batch, seq, alpha):
    m, d = x.shape
    tq = min(MIX_ROWS, seq)
    assert seq % tq == 0 and tq % WINDOW == 0
    n_t = seq // tq
    depth = w_in.shape[0]
    cos, s_lo, s_hi = tables
    mod_spec = lambda j: pl.BlockSpec((None, MOD_ROWS, d), lambda b, t: (layer, 0, j))
    tab_spec = pl.BlockSpec((tq, LANES), lambda b, t: (t, 0))
    lay3 = lambda shape: pl.BlockSpec((None,) + shape, lambda b, t: (layer,) + (0,) * len(shape))
    return pl.pallas_call(
        functools.partial(_mixer_prompt_kernel, layer=layer, tq=tq, alpha=alpha),
        grid=(batch, n_t),
        in_specs=[
            pl.BlockSpec((tq, d), lambda b, t: (b * n_t + t, 0)),
            mod_spec(3), mod_spec(4), mod_spec(5),
            lay3((d, D_IN)), lay3((d, d)),
            tab_spec, tab_spec, tab_spec,
            pl.BlockSpec(memory_space=pltpu.SMEM),
            lay3((1, D_B)), lay3((1, D_B)),
            lay3((G_B, CHUNK_B, CHUNK_B)), lay3((CHUNK_B, D_B)),
            pl.BlockSpec((depth, D_C), lambda b, t: (0, 0)),
            lay3((1, D_C)),
            pl.BlockSpec((None, 1, d), lambda b, t: (layer * 3 + 1, 0, 0)),
            pl.BlockSpec((None, 1, d), lambda b, t: (layer * 3 + 1, 0, 0)),
        ],
        out_specs=[
            pl.BlockSpec((tq, d), lambda b, t: (b * n_t + t, 0)),
            pl.BlockSpec((None, WINDOW, D_KV), lambda b, t: (b, 0, 0)),
            pl.BlockSpec((None, WINDOW, D_KV), lambda b, t: (b, 0, 0)),
            pl.BlockSpec((None, D_C, DV_C), lambda b, t: (b, 0, 0)),
        ],
        out_shape=[
            jax.ShapeDtypeStruct((m, d), F32),
            jax.ShapeDtypeStruct((batch, WINDOW, D_KV), F32),
            jax.ShapeDtypeStruct((batch, WINDOW, D_KV), F32),
            jax.ShapeDtypeStruct((batch, D_C, DV_C), F32),
        ],
        scratch_shapes=[
            pltpu.VMEM((WINDOW, D_KV), F32),
            pltpu.VMEM((WINDOW, D_KV), F32),
            pltpu.VMEM((DK_C, D_C), F32),
        ],
        compiler_params=pltpu.CompilerParams(
            dimension_semantics=("arbitrary", "arbitrary"), vmem_limit_bytes=VMEM_LIMIT_BYTES),
        name="mixer_prompt",
    )(x, mods, mods, mods, w_in, w_out, cos, s_lo, s_hi, sinks, gln_g, gln_b, ws, bs_exp, lb, ng_exp,
      ln_g, ln_b)


def _mixer_sample_kernel(
        x_ref, sh_ref, sc_ref, gt_ref, win_ref, wout_ref, cos_ref, sl_ref, sh2_ref, sink_ref,
        gg_ref, gb_ref, w0_ref, bs_ref, lb_ref, ng_ref, lng_ref, lnb_ref, ck_ref, cv_ref, s0_ref,
        y_ref, knew_ref, vnew_ref, gv_ref, s1_ref,
        qt_ref, kn_ref, vn_ref, sn_ref, sc_scr, ot_ref, ft_ref, kt_ref, qc_ref, it_ref, ob_ref, og_ref, oc_ref,
        *, layer, n_s, nb, alpha):
    step = pl.program_id(0)
    rows_all = n_s * H_A

    @pl.when(step == 0)
    def _():
        x = x_ref[...]
        h = (x * (1.0 + sc_ref[...]) + sh_ref[...]).astype(BF16)
        proj = _dot(h, win_ref[...])
        cos, s_lo, s_hi = cos_ref[...], sl_ref[...], sh2_ref[...]
        q = _rope(proj[:, OFF_Q:OFF_K], cos, s_lo, s_hi) * (HEAD_DIM ** -0.5)
        k = _rope(proj[:, OFF_K:OFF_V], cos, s_lo, s_hi)
        v = proj[:, OFF_V:OFF_U]
        knew_ref[...] = k
        vnew_ref[...] = v
        zeros64 = jnp.zeros((n_s, HEAD_DIM), F32)
        for hd in range(H_A):
            qh = q[:, hd * HEAD_DIM:(hd + 1) * HEAD_DIM]
            row = jnp.concatenate([qh, zeros64] if hd < G_A else [zeros64, qh], axis=-1)
            qt_ref[pl.ds(hd, n_s, stride=H_A), :] = row
            kn_ref[pl.ds(hd, n_s, stride=H_A), :] = k
            vn_ref[pl.ds(hd, n_s, stride=H_A), :] = v
        s_new = jnp.sum(qt_ref[...] * kn_ref[...], axis=-1, keepdims=True)
        sn_ref[...] = jnp.broadcast_to(s_new, (rows_all, LANES))

        vn = _layer_norm(proj[:, OFF_GV:OFF_CQ], gg_ref[...], gb_ref[...])
        gv_ref[...] = vn
        ob_ref[...] = proj[:, OFF_U:OFF_GV] * (vn * w0_ref[...] + bs_ref[0:1, :])

        lb = _hgrn_lower_bound(lb_ref, layer)
        f = lb + (1.0 - lb) * jax.nn.sigmoid(proj[:, OFF_CF:OFF_CI])
        ft_ref[...] = f.T
        kt_ref[...] = (1.0 - f).T
        qc_ref[...] = proj[:, OFF_CQ:OFF_CF].T
        it_ref[...] = proj[:, OFF_CI:OFF_CG].T
        og_ref[...] = ng_ref[...] * _silu(proj[:, OFF_CG:])

    base = pl.multiple_of(step * (nb * H_A), nb * H_A)
    for b in range(nb):
        qrows = qt_ref[pl.ds(base + b * H_A, H_A), :].astype(BF16)
        sc_scr[b * H_A:(b + 1) * H_A, :] = _dot(qrows, ck_ref[b].astype(BF16))
    lane = lax.broadcasted_iota(jnp.int32, (nb * H_A, WINDOW), 1)
    s = jnp.where(lane == 0, NEG_BIG, sc_scr[...])
    s_new = sn_ref[pl.ds(base, nb * H_A), 0:1]
    hrow = lax.broadcasted_iota(jnp.int32, (nb * H_A, 1), 0) & (H_A - 1)
    sink = jnp.zeros((nb * H_A, 1), F32)
    for hd in range(H_A):
        sink = jnp.where(hrow == hd, sink_ref[layer, hd], sink)
    mx = jnp.maximum(jnp.maximum(jnp.max(s, axis=-1, keepdims=True), s_new), sink)
    p = jnp.exp(s - mx)
    p_new = jnp.exp(s_new - mx)
    den = jnp.sum(p, axis=-1, keepdims=True) + p_new + jnp.exp(sink - mx)
    sc_scr[...] = p
    for b in range(nb):
        prow = sc_scr[b * H_A:(b + 1) * H_A, :].astype(BF16)
        ot_ref[pl.ds(base + b * H_A, H_A), :] = _dot_t(prow, cv_ref[b].astype(BF16))
    rows = pl.ds(base, nb * H_A)
    ot_ref[rows, :] = (ot_ref[rows, :] + p_new * vn_ref[rows, :]) * (1.0 / den)

    hrow0 = pl.multiple_of(step * DK_C, DK_C)
    i_h = it_ref[pl.ds(hrow0, DV_C), :]
    acc = jnp.zeros((DV_C, n_s), F32)
    for kk in range(DK_C):
        f_k = ft_ref[pl.ds(hrow0 + kk, 1), :]
        k_k = kt_ref[pl.ds(hrow0 + kk, 1), :]
        q_k = qc_ref[pl.ds(hrow0 + kk, 1), :]
        s1 = f_k * s0_ref[kk] + k_k * i_h
        s1_ref[kk] = s1
        acc = acc + q_k * s1
    oc_ref[pl.ds(hrow0, DV_C), :] = acc

    @pl.when(step == pl.num_programs(0) - 1)
    def _():
        o_a = []
        for hd in range(H_A):
            r = ot_ref[pl.ds(hd, n_s, stride=H_A), :]
            o_a.append(r[:, :HEAD_DIM] if hd < G_A else r[:, HEAD_DIM:])
        o_c = oc_ref[...].T
        ms = _head_sum(o_c * o_c, _head_block_ones()) * (1.0 / DV_C)
        o_c = o_c * lax.rsqrt(ms + RMS_EPS) * og_ref[...]
        mix = _dot(jnp.concatenate(o_a + [ob_ref[...], o_c], axis=-1).astype(BF16), wout_ref[...])
        z = alpha * x_ref[...] + (1.0 + gt_ref[...]) * mix
        y_ref[...] = _layer_norm(z, lng_ref[...], lnb_ref[...])


def _mixer_sample(x, mods, layer, w_in, w_out, tables, sinks, gln_g, gln_b, w0_exp, bs_exp, lb, ng_exp,
                  ln_g, ln_b, cache_k, cache_v, state, *, alpha):
    n_s, d = x.shape
    assert n_s % (H_C * SUBLANES) == 0 and n_s % LANES == 0
    nb = n_s // H_C
    depth = w_in.shape[0]
    cos, s_lo, s_hi = tables
    mod_spec = lambda j: pl.BlockSpec((None, n_s, d), lambda i: (layer, 0, j))
    tab_spec = pl.BlockSpec((1, LANES), lambda i: (0, 0))
    lay3 = lambda shape: pl.BlockSpec((None,) + shape, lambda i: (layer,) + (0,) * len(shape))
    rows = pl.BlockSpec((n_s, d), lambda i: (0, 0))
    per_head = pltpu.VMEM((D_C, n_s), F32)
    per_row = pltpu.VMEM((n_s * H_A, LANES), F32)
    return pl.pallas_call(
        functools.partial(_mixer_sample_kernel, layer=layer, n_s=n_s, nb=nb, alpha=alpha),
        grid=(H_C,),
        in_specs=[
            rows,
            mod_spec(3), mod_spec(4), mod_spec(5),
            lay3((d, D_IN)), lay3((d, d)),
            tab_spec, tab_spec, tab_spec,
            pl.BlockSpec(memory_space=pltpu.SMEM),
            lay3((1, D_B)), lay3((1, D_B)), lay3((1, D_B)), lay3((CHUNK_B, D_B)),
            pl.BlockSpec((depth, D_C), lambda i: (0, 0)),
            lay3((1, D_C)),
            pl.BlockSpec((None, 1, d), lambda i: (layer * 3 + 1, 0, 0)),
            pl.BlockSpec((None, 1, d), lambda i: (layer * 3 + 1, 0, 0)),
            pl.BlockSpec((None, nb, D_KV, WINDOW), lambda i: (layer, i, 0, 0)),
            pl.BlockSpec((None, nb, D_KV, WINDOW), lambda i: (layer, i, 0, 0)),
            pl.BlockSpec((None, None, DK_C, DV_C, n_s), lambda i: (layer, i, 0, 0, 0)),
        ],
        out_specs=[
            rows,
            pl.BlockSpec((n_s, D_KV), lambda i: (0, 0)),
            pl.BlockSpec((n_s, D_KV), lambda i: (0, 0)),
            pl.BlockSpec((n_s, D_B), lambda i: (0, 0)),
            pl.BlockSpec((None, DK_C, DV_C, n_s), lambda i: (i, 0, 0, 0)),
        ],
        out_shape=[
            jax.ShapeDtypeStruct((n_s, d), F32),
            jax.ShapeDtypeStruct((n_s, D_KV), F32),
            jax.ShapeDtypeStruct((n_s, D_KV), F32),
            jax.ShapeDtypeStruct((n_s, D_B), F32),
            jax.ShapeDtypeStruct((H_C, DK_C, DV_C, n_s), F32),
        ],
        scratch_shapes=[
            per_row,
            per_row,
            per_row,
            per_row,
            pltpu.VMEM((nb * H_A, WINDOW), F32),
            per_row,
            per_head, per_head, per_head, per_head,
            pltpu.VMEM((n_s, D_B), F32),
            pltpu.VMEM((n_s, D_C), F32),
            per_head,
        ],
        compiler_params=pltpu.CompilerParams(
            dimension_semantics=("arbitrary",), vmem_limit_bytes=VMEM_LIMIT_BYTES),
        name="mixer_sample",
    )(x, mods, mods, mods, w_in, w_out, cos, s_lo, s_hi, sinks, gln_g, gln_b, w0_exp, bs_exp, lb, ng_exp,
      ln_g, ln_b, cache_k, cache_v, state)


def _rope_tables(positions):
    half = ROT_DIM // 2
    f32 = np.float32
    inv = (ROPE_THETA ** (-np.arange(half, dtype=f32) * 2.0 / ROT_DIM)).astype(f32)
    ang = (positions.astype(f32)[:, None] * inv[None, :]).astype(f32)
    cos, sin = np.cos(ang).astype(f32), np.sin(ang).astype(f32)
    n = positions.shape[0]
    rest = HEAD_DIM - ROT_DIM
    cos_h = np.concatenate([cos, cos, np.ones((n, rest), f32)], axis=-1)
    lo_h = np.concatenate([-sin, np.zeros((n, half + rest), f32)], axis=-1)
    hi_h = np.concatenate([np.zeros((n, half), f32), sin, np.zeros((n, rest), f32)], axis=-1)
    two = lambda a: jnp.asarray(np.concatenate([a, a], axis=-1))
    return two(cos_h), two(lo_h), two(hi_h)


def kernel(x_prompt, x_sample, cache_k, cache_v, state_hgrn, c_prompt, c_sample, w_in, w_out, attn_sinks,
           gmlp_ln_g, gmlp_ln_b, gmlp_ws, gmlp_bs, hgrn_lb, hgrn_norm_g, ffn1_in, ffn1_out, ffn2_in,
           ffn2_out, ada_w, ada_b, ln_g, ln_b):
    batch, seq, d = x_prompt.shape
    n_s, dec_seq, _ = x_sample.shape
    depth = w_in.shape[0]
    assert d == D_MODEL and dec_seq == 1 and batch <= MOD_ROWS
    assert cache_k.shape[2] == WINDOW
    alpha = (2 * depth) ** 0.25

    w_in_b, w_out_b = w_in.astype(BF16), w_out.astype(BF16)
    ln_g3 = ln_g.reshape(depth * 3, 1, d)
    ln_b3 = ln_b.reshape(depth * 3, 1, d)
    gln_g = gmlp_ln_g.reshape(depth, 1, D_B)
    gln_b = gmlp_ln_b.reshape(depth, 1, D_B)
    bs_exp = jnp.repeat(jnp.swapaxes(gmlp_bs, 1, 2), HEAD_DIM, axis=2)
    w0_exp = jnp.repeat(gmlp_ws[:, :, 0, 0], HEAD_DIM, axis=1).reshape(depth, 1, D_B)
    ng_exp = jnp.tile(hgrn_norm_g, (1, H_C)).reshape(depth, 1, D_C)
    tab_p = _rope_tables(np.arange(seq))
    tab_s = _rope_tables(PAST_LEN + np.arange(dec_seq))

    c_all = jnp.concatenate([c_prompt, jnp.zeros((MOD_ROWS - batch, d), F32), c_sample], axis=0)
    mod_p, mod_s = _adaln(c_all, ada_w, ada_b)

    ck = jnp.transpose(cache_k, (0, 1, 3, 4, 2)).reshape(depth, n_s, D_KV, WINDOW)
    cv = jnp.transpose(cache_v, (0, 1, 3, 4, 2)).reshape(depth, n_s, D_KV, WINDOW)
    s0 = jnp.transpose(state_hgrn, (0, 2, 3, 4, 1))

    xp = x_prompt.reshape(batch * seq, d)
    xs = x_sample.reshape(n_s, d)
    ffn = functools.partial(_ffn, rows_per_batch=seq, alpha=alpha)
    kw, vw, st_p, kn, vnw, gvs, st_s = [], [], [], [], [], [], []
    for l in range(depth):
        xp, xs = ffn(xp, xs, mod_p, mod_s, l, 0, ffn1_in, ffn1_out, ln_g3, ln_b3)
        xp, k_l, v_l, s_l = _mixer_prompt(
            xp, mod_p, l, w_in_b, w_out_b, tab_p, attn_sinks, gln_g, gln_b, gmlp_ws, bs_exp, hgrn_lb,
            ng_exp, ln_g3, ln_b3, batch=batch, seq=seq, alpha=alpha)
        kw.append(k_l), vw.append(v_l), st_p.append(s_l)
        xs, k_l, v_l, g_l, s_l = _mixer_sample(
            xs, mod_s, l, w_in_b, w_out_b, tab_s, attn_sinks, gln_g, gln_b, w0_exp, bs_exp, hgrn_lb,
            ng_exp, ln_g3, ln_b3, ck, cv, s0, alpha=alpha)
        kn.append(k_l), vnw.append(v_l), gvs.append(g_l), st_s.append(s_l)
        xp, xs = ffn(xp, xs, mod_p, mod_s, l, 2, ffn2_in, ffn2_out, ln_g3, ln_b3)

    stack = lambda parts, shape: jnp.stack(parts, axis=0).reshape((depth,) + shape)
    return (
        xp.reshape(batch, seq, d),
        xs.reshape(n_s, dec_seq, d),
        stack(kw, (batch, WINDOW, KV_A, HEAD_DIM)),
        stack(vw, (batch, WINDOW, KV_A, HEAD_DIM)),
        stack(st_p, (batch, H_C, DK_C, DV_C)),
        stack(kn, (n_s, dec_seq, KV_A, HEAD_DIM)),
        stack(vnw, (n_s, dec_seq, KV_A, HEAD_DIM)),
        stack(gvs, (n_s, dec_seq, D_B)),
        jnp.transpose(jnp.stack(st_s, axis=0), (0, 4, 1, 2, 3)),
    )
```
